```python
import math
import jax, jax.numpy as jnp
from jax import lax
import numpy as np

D_MODEL = 4096
BATCH = 4
SEQ = 4096
DEPTH = 2
DEC_BATCH = 32
DEC_SEQ = 16
PAST_LEN = 1024

CHUNK = 64
SSD_EXPAND = 2
SSD_D_INNER = SSD_EXPAND * D_MODEL
SSD_HEAD_DIM = 64
SSD_N_HEADS = SSD_D_INNER // SSD_HEAD_DIM
SSD_N_GROUPS = 8
SSD_HEADS_PER_GROUP = SSD_N_HEADS // SSD_N_GROUPS
SSD_D_STATE = 128
SSD_CONV_W = 4
SSD_CONV_DIM = SSD_D_INNER + 2 * SSD_N_GROUPS * SSD_D_STATE
SSD_IN_DIM = 2 * SSD_D_INNER + 2 * SSD_N_GROUPS * SSD_D_STATE + SSD_N_HEADS
S5_GROUP_CH = 16
S5_N_GROUPS = D_MODEL // S5_GROUP_CH
S5_STATE = 64
N_EXPERTS = 16
N_EXPERT_GROUPS = 4
EXPERTS_PER_GROUP = N_EXPERTS // N_EXPERT_GROUPS
TOP_K = 2
D_EXPERT = 1024
N_MIXERS = 2
N_SSD_LAYERS = (DEPTH + 1) // 2
N_S5_LAYERS = DEPTH // 2
DEEPNORM_ALPHA = (2 * DEPTH) ** 0.25
DEEPNORM_BETA = (8 * DEPTH) ** -0.25
LN_EPS = 1e-5
RMS_EPS = 1e-5

kernel_name = "hybrid_ssd_s5_moe_stream_step"


def layer_norm(x, g, b):
    xf = x.astype(jnp.float32)
    mu = jnp.mean(xf, axis=-1, keepdims=True)
    var = jnp.mean(jnp.square(xf - mu), axis=-1, keepdims=True)
    return ((xf - mu) * lax.rsqrt(var + LN_EPS) * g + b).astype(x.dtype)


def causal_depthwise_conv(x, buf, w, b):
    L = x.shape[1]
    xp = jnp.concatenate([buf.astype(x.dtype), x], axis=1)
    y = b
    for k in range(SSD_CONV_W):
        y = y + xp[:, k:k + L] * w[k]
    return y, xp[:, -(SSD_CONV_W - 1):]


def ssd_chunked_scan(xh, dt, a, bm, cm, h0):
    bsz, L = xh.shape[:2]
    q = min(CHUNK, L)
    nc = L // q

    def to_chunks(t):
        return jnp.moveaxis(t.reshape((bsz, nc, q) + t.shape[2:]), 1, 0)

    idx = jnp.arange(q)
    causal = (idx[:, None] >= idx[None, :])[None, :, :, None, None]

    def step(h, inp):
        xc, dtc, bc, cc = inp
        acum = jnp.cumsum(dtc * a, axis=1)
        seg = acum[:, :, None] - acum[:, None, :]
        decay = jnp.exp(jnp.where(causal, seg, -jnp.inf))
        cb = jnp.einsum("bign,bjgn->bijg", cc, bc)
        m = cb[..., None] * decay * dtc[:, None]
        y = jnp.einsum("bijgh,bjghp->bighp", m, xc)
        y = y + jnp.einsum("bign,bghpn->bighp", cc, h) * jnp.exp(acum)[..., None]
        xw = xc * (jnp.exp(acum[:, -1:] - acum) * dtc)[..., None]
        h = h * jnp.exp(acum[:, -1])[..., None, None] + jnp.einsum("bjgn,bjghp->bghpn", bc, xw)
        return h, y

    h, ys = lax.scan(step, h0, (to_chunks(xh), to_chunks(dt), to_chunks(bm), to_chunks(cm)))
    return jnp.moveaxis(ys, 0, 1).reshape(xh.shape), h


def ssd_mixer(x, conv_buf, h0, w_in, conv_w, conv_b, dt_bias, a_log, d_skip, norm_w, w_out):
    f32 = jnp.float32
    bsz, L, _ = x.shape
    G, Hg, P, N = SSD_N_GROUPS, SSD_HEADS_PER_GROUP, SSD_HEAD_DIM, SSD_D_STATE
    zxbcdt = x @ w_in
    z, xbc, dt = jnp.split(zxbcdt, [SSD_D_INNER, SSD_D_INNER + SSD_CONV_DIM], axis=-1)
    xbc, new_buf = causal_depthwise_conv(xbc, conv_buf, conv_w, conv_b)
    xbc = jax.nn.silu(xbc)
    xs, bm, cm = jnp.split(xbc, [SSD_D_INNER, SSD_D_INNER + G * N], axis=-1)
    xh = xs.astype(f32).reshape(bsz, L, G, Hg, P)
    bm = bm.astype(f32).reshape(bsz, L, G, N)
    cm = cm.astype(f32).reshape(bsz, L, G, N)
    dt = jax.nn.softplus(dt.astype(f32) + dt_bias.astype(f32)).reshape(bsz, L, G, Hg)
    a = -jnp.exp(a_log.astype(f32)).reshape(G, Hg)
    y, h = ssd_chunked_scan(xh, dt, a, bm, cm, h0.astype(f32).reshape(bsz, G, Hg, P, N))
    y = y + d_skip.astype(f32).reshape(G, Hg, 1) * xh
    y = y.reshape(bsz, L, SSD_D_INNER) * jax.nn.silu(z.astype(f32))
    yg = y.reshape(bsz, L, G, SSD_D_INNER // G)
    yg = yg * lax.rsqrt(jnp.mean(jnp.square(yg), axis=-1, keepdims=True) + RMS_EPS)
    y = (yg.reshape(bsz, L, SSD_D_INNER) * norm_w).astype(x.dtype)
    return y @ w_out, new_buf.astype(conv_buf.dtype), h.reshape(bsz, SSD_N_HEADS, P, N).astype(h0.dtype)


def _complex_affine_combine(e1, e2):
    a1r, a1i, b1r, b1i = e1
    a2r, a2i, b2r, b2i = e2
    return (a2r * a1r - a2i * a1i,
            a2r * a1i + a2i * a1r,
            a2r * b1r - a2i * b1i + b2r,
            a2r * b1i + a2i * b1r + b2i)


def s5_mixer(x, h0_re, h0_im, lam_re, lam_im, log_step, b_re, b_im, c_re, c_im, d_skip, w_glu):
    f32 = jnp.float32
    bsz, L, _ = x.shape
    q = min(CHUNK, L)
    nc = L // q
    lr, li = lam_re.astype(f32), lam_im.astype(f32)
    dstep = jnp.exp(log_step.astype(f32))[:, None]
    mag = jnp.exp(lr * dstep)
    ar, ai = mag * jnp.cos(li * dstep), mag * jnp.sin(li * dstep)
    den = lr * lr + li * li
    cr = ((ar - 1.0) * lr + ai * li) / den
    ci = (ai * lr - (ar - 1.0) * li) / den
    br, bi = b_re.astype(f32), b_im.astype(f32)
    bbr = cr[..., None] * br - ci[..., None] * bi
    bbi = cr[..., None] * bi + ci[..., None] * br
    cre, cim = c_re.astype(f32), c_im.astype(f32)
    u = x.astype(f32).reshape(bsz, nc, q, S5_N_GROUPS, S5_GROUP_CH)
    u = jnp.moveaxis(u, 1, 0)

    def step(carry, uc):
        hr0, hi0 = carry
        bur = jnp.einsum("bqgc,gnc->bqgn", uc, bbr)
        bui = jnp.einsum("bqgc,gnc->bqgn", uc, bbi)
        a_r = jnp.broadcast_to(ar, bur.shape)
        a_i = jnp.broadcast_to(ai, bur.shape)
        pr, pi, hr, hi = lax.associative_scan(_complex_affine_combine, (a_r, a_i, bur, bui), axis=1)
        hr, hi = (hr + pr * hr0[:, None] - pi * hi0[:, None],
                  hi + pr * hi0[:, None] + pi * hr0[:, None])
        y = jnp.einsum("bqgn,gcn->bqgc", hr, cre) - jnp.einsum("bqgn,gcn->bqgc", hi, cim)
        return (hr[:, -1], hi[:, -1]), y

    (hr, hi), ys = lax.scan(step, (h0_re.astype(f32), h0_im.astype(f32)), u)
    y = jnp.moveaxis(ys, 0, 1).reshape(bsz, L, D_MODEL) + d_skip.astype(f32) * x.astype(f32)
    g = jax.nn.gelu(y, approximate=False).astype(x.dtype)
    z1, z2 = jnp.split(g @ w_glu, 2, axis=-1)
    return z1 * jax.nn.sigmoid(z2), hr.astype(h0_re.dtype), hi.astype(h0_im.dtype)


def grouped_moe(h, router_w, router_b, w_gate, w_up, w_down):
    f32 = jnp.float32
    bsz, L, d = h.shape
    t = h.reshape(-1, d)
    logits = t.astype(f32) @ router_w.astype(f32) + router_b.astype(f32)
    probs = jax.nn.softmax(logits, axis=-1)
    group_score = jnp.max(probs.reshape(-1, N_EXPERT_GROUPS, EXPERTS_PER_GROUP), axis=-1)
    best_group = jnp.argmax(group_score, axis=-1)
    in_group = (jnp.arange(N_EXPERTS) // EXPERTS_PER_GROUP)[None, :] == best_group[:, None]
    top_p, top_i = lax.top_k(jnp.where(in_group, probs, -1.0), TOP_K)
    gates = top_p / jnp.sum(top_p, axis=-1, keepdims=True)
    dense_gates = jnp.sum(jax.nn.one_hot(top_i, N_EXPERTS, dtype=f32) * gates[..., None], axis=1)
    dense_gates = dense_gates.astype(h.dtype)
    out = jnp.zeros_like(t)
    for e in range(N_EXPERTS):
        he = jax.nn.silu(t @ w_gate[e]) * (t @ w_up[e])
        out = out + dense_gates[:, e:e + 1] * (he @ w_down[e])
    return out.reshape(bsz, L, d)


def trunk(x, conv_bufs, ssd_h, s5_re, s5_im, p):
    new_conv, new_ssd, new_re, new_im = [], [], [], []
    for i in range(DEPTH):
        j = i // N_MIXERS
        if i % N_MIXERS == 0:
            mix, cb, hs = ssd_mixer(x, conv_bufs[j], ssd_h[j], p["ssd_w_in"][j], p["ssd_conv_w"][j],
                                    p["ssd_conv_b"][j], p["ssd_dt_bias"][j], p["ssd_a_log"][j],
                                    p["ssd_d"][j], p["ssd_norm_w"][j], p["ssd_w_out"][j])
            new_conv.append(cb)
            new_ssd.append(hs)
        else:
            mix, hr, hi = s5_mixer(x, s5_re[j], s5_im[j], p["s5_lambda_re"][j], p["s5_lambda_im"][j],
                                   p["s5_log_step"][j], p["s5_b_re"][j], p["s5_b_im"][j],
                                   p["s5_c_re"][j], p["s5_c_im"][j], p["s5_d"][j], p["s5_w_glu"][j])
            new_re.append(hr)
            new_im.append(hi)
        x = layer_norm(DEEPNORM_ALPHA * x + mix, p["ln_mix_g"][i], p["ln_mix_b"][i])
        ffn = grouped_moe(x, p["router_w"], p["router_b"], p["moe_w_gate"][i], p["moe_w_up"][i],
                          p["moe_w_down"][i])
        x = layer_norm(DEEPNORM_ALPHA * x + ffn, p["ln_ffn_g"][i], p["ln_ffn_b"][i])
    return x, jnp.stack(new_conv), jnp.stack(new_ssd), jnp.stack(new_re), jnp.stack(new_im)


def setup_inputs(seed: int = 0) -> dict:
    key = jax.random.key(seed)
    ks = iter(jax.random.split(key, 40))
    f32 = jnp.float32

    def nrm(shape, std):
        return jax.random.normal(next(ks), shape, f32) * std

    def unif(shape, lo, hi):
        return jax.random.uniform(next(ks), shape, f32, lo, hi)

    NA, NB = N_SSD_LAYERS, N_S5_LAYERS
    x_prompt = nrm((BATCH, SEQ, D_MODEL), 1.0)
    x_sample = nrm((DEC_BATCH, DEC_SEQ, D_MODEL), 1.0)
    cache_ssd_conv = nrm((NA, DEC_BATCH, SSD_CONV_W - 1, SSD_CONV_DIM), 1.0)
    state_ssd = nrm((NA, DEC_BATCH, SSD_N_HEADS, SSD_HEAD_DIM, SSD_D_STATE), 0.1)
    state_s5_re = nrm((NB, DEC_BATCH, S5_N_GROUPS, S5_STATE), 0.1)
    state_s5_im = nrm((NB, DEC_BATCH, S5_N_GROUPS, S5_STATE), 0.1)
    ssd_w_in = nrm((NA, D_MODEL, SSD_IN_DIM), D_MODEL ** -0.5)
    ssd_conv_w = nrm((NA, SSD_CONV_W, SSD_CONV_DIM), SSD_CONV_W ** -0.5)
    ssd_conv_b = nrm((NA, SSD_CONV_DIM), 0.01)
    dt0 = jnp.exp(unif((NA, SSD_N_HEADS), math.log(1e-3), math.log(1e-1)))
    ssd_dt_bias = dt0 + jnp.log(-jnp.expm1(-dt0))
    ssd_a_log = jnp.log(unif((NA, SSD_N_HEADS), 1.0, 16.0))
    ssd_d = 1.0 + nrm((NA, SSD_N_HEADS), 0.01)
    ssd_norm_w = 1.0 + nrm((NA, SSD_D_INNER), 0.01)
    ssd_w_out = nrm((NA, SSD_D_INNER, D_MODEL), DEEPNORM_BETA * SSD_D_INNER ** -0.5)
    s5_lambda_re = -0.5 + nrm((NB, S5_N_GROUPS, S5_STATE), 0.01)
    s5_lambda_im = math.pi * jnp.arange(S5_STATE, dtype=f32) + nrm((NB, S5_N_GROUPS, S5_STATE), 0.01)
    s5_log_step = unif((NB, S5_N_GROUPS), math.log(1e-3), math.log(1e-1))
    s5_b_re = nrm((NB, S5_N_GROUPS, S5_STATE, S5_GROUP_CH), (2 * S5_GROUP_CH) ** -0.5)
    s5_b_im = nrm((NB, S5_N_GROUPS, S5_STATE, S5_GROUP_CH), (2 * S5_GROUP_CH) ** -0.5)
    s5_c_re = nrm((NB, S5_N_GROUPS, S5_GROUP_CH, S5_STATE), S5_STATE ** -0.5)
    s5_c_im = nrm((NB, S5_N_GROUPS, S5_GROUP_CH, S5_STATE), S5_STATE ** -0.5)
    s5_d = nrm((NB, D_MODEL), 1.0)
    s5_w_glu = nrm((NB, D_MODEL, 2 * D_MODEL), DEEPNORM_BETA * D_MODEL ** -0.5)
    ln_mix_g = 1.0 + nrm((DEPTH, D_MODEL), 0.01)
    ln_mix_b = nrm((DEPTH, D_MODEL), 0.01)
    ln_ffn_g = 1.0 + nrm((DEPTH, D_MODEL), 0.01)
    ln_ffn_b = nrm((DEPTH, D_MODEL), 0.01)
    router_w = nrm((D_MODEL, N_EXPERTS), D_MODEL ** -0.5)
    router_b = nrm((N_EXPERTS,), 0.01)
    moe_w_gate = nrm((DEPTH, N_EXPERTS, D_MODEL, D_EXPERT), D_MODEL ** -0.5)
    moe_w_up = nrm((DEPTH, N_EXPERTS, D_MODEL, D_EXPERT), D_MODEL ** -0.5)
    moe_w_down = nrm((DEPTH, N_EXPERTS, D_EXPERT, D_MODEL), DEEPNORM_BETA * D_EXPERT ** -0.5)
    return {
        "x_prompt": x_prompt, "x_sample": x_sample,
        "cache_ssd_conv": cache_ssd_conv, "state_ssd": state_ssd,
        "state_s5_re": state_s5_re, "state_s5_im": state_s5_im,
        "ssd_w_in": ssd_w_in, "ssd_conv_w": ssd_conv_w, "ssd_conv_b": ssd_conv_b,
        "ssd_dt_bias": ssd_dt_bias, "ssd_a_log": ssd_a_log, "ssd_d": ssd_d,
        "ssd_norm_w": ssd_norm_w, "ssd_w_out": ssd_w_out,
        "s5_lambda_re": s5_lambda_re, "s5_lambda_im": s5_lambda_im, "s5_log_step": s5_log_step,
        "s5_b_re": s5_b_re, "s5_b_im": s5_b_im, "s5_c_re": s5_c_re, "s5_c_im": s5_c_im,
        "s5_d": s5_d, "s5_w_glu": s5_w_glu,
        "ln_mix_g": ln_mix_g, "ln_mix_b": ln_mix_b, "ln_ffn_g": ln_ffn_g, "ln_ffn_b": ln_ffn_b,
        "router_w": router_w, "router_b": router_b,
        "moe_w_gate": moe_w_gate, "moe_w_up": moe_w_up, "moe_w_down": moe_w_down,
    }


def reference(x_prompt, x_sample, cache_ssd_conv, state_ssd, state_s5_re, state_s5_im,
              ssd_w_in, ssd_conv_w, ssd_conv_b, ssd_dt_bias, ssd_a_log, ssd_d, ssd_norm_w, ssd_w_out,
              s5_lambda_re, s5_lambda_im, s5_log_step, s5_b_re, s5_b_im, s5_c_re, s5_c_im, s5_d, s5_w_glu,
              ln_mix_g, ln_mix_b, ln_ffn_g, ln_ffn_b, router_w, router_b,
              moe_w_gate, moe_w_up, moe_w_down):
    params = {
        "ssd_w_in": ssd_w_in, "ssd_conv_w": ssd_conv_w, "ssd_conv_b": ssd_conv_b,
        "ssd_dt_bias": ssd_dt_bias, "ssd_a_log": ssd_a_log, "ssd_d": ssd_d,
        "ssd_norm_w": ssd_norm_w, "ssd_w_out": ssd_w_out,
        "s5_lambda_re": s5_lambda_re, "s5_lambda_im": s5_lambda_im, "s5_log_step": s5_log_step,
        "s5_b_re": s5_b_re, "s5_b_im": s5_b_im, "s5_c_re": s5_c_re, "s5_c_im": s5_c_im,
        "s5_d": s5_d, "s5_w_glu": s5_w_glu,
        "ln_mix_g": ln_mix_g, "ln_mix_b": ln_mix_b, "ln_ffn_g": ln_ffn_g, "ln_ffn_b": ln_ffn_b,
        "router_w": router_w, "router_b": router_b,
        "moe_w_gate": moe_w_gate, "moe_w_up": moe_w_up, "moe_w_down": moe_w_down,
    }
    nb = x_prompt.shape[0]
    dt = x_prompt.dtype
    zero_conv = jnp.zeros((N_SSD_LAYERS, nb, SSD_CONV_W - 1, SSD_CONV_DIM), dt)
    zero_ssd = jnp.zeros((N_SSD_LAYERS, nb, SSD_N_HEADS, SSD_HEAD_DIM, SSD_D_STATE), dt)
    zero_re = jnp.zeros((N_S5_LAYERS, nb, S5_N_GROUPS, S5_STATE), dt)
    zero_im = jnp.zeros((N_S5_LAYERS, nb, S5_N_GROUPS, S5_STATE), dt)
    y_prompt, conv_p, ssd_p, re_p, im_p = trunk(x_prompt, zero_conv, zero_ssd, zero_re, zero_im, params)
    y_sample, conv_s, ssd_s, re_s, im_s = trunk(x_sample, cache_ssd_conv, state_ssd,
                                                state_s5_re, state_s5_im, params)
    return (y_prompt, y_sample, conv_p, ssd_p, re_p, im_p, conv_s, ssd_s, re_s, im_s)
```

```python
import functools
import math

import jax
import jax.numpy as jnp
from jax import lax
from jax.experimental import pallas as pl
from jax.experimental.pallas import tpu as pltpu

F32 = jnp.float32
BF16 = jnp.bfloat16
I32 = jnp.int32

D_MODEL = 4096
BATCH, SEQ = 4, 4096
DEC_BATCH, DEC_SEQ = 32, 16
T_PROMPT = BATCH * SEQ
T_SAMPLE = DEC_BATCH * DEC_SEQ
T_ALL = T_PROMPT + T_SAMPLE
SSD_D_INNER = 8192
SSD_HEAD_DIM = 64
SSD_N_HEADS = 128
SSD_N_GROUPS = 8
SSD_HEADS_PER_GROUP = 16
SSD_D_STATE = 128
SSD_GROUP_CH = SSD_D_INNER // SSD_N_GROUPS
SSD_BC_DIM = SSD_N_GROUPS * SSD_D_STATE
SSD_CONV_DIM = SSD_D_INNER + 2 * SSD_BC_DIM
SSD_CONV_W = 4
S5_GROUP_CH = 16
S5_N_GROUPS = 256
S5_STATE = 64
S5_PACK_GROUPS = 8
S5_N_PACKS = S5_N_GROUPS // S5_PACK_GROUPS
S5_PACK_STATE = S5_PACK_GROUPS * S5_STATE
S5_Q = 16
N_EXPERTS = 16
EXPERTS_PER_GROUP = 4
D_EXPERT = 1024
DEPTH = 2
DEEPNORM_ALPHA = (2 * DEPTH) ** 0.25
LN_EPS = 1e-5
RMS_EPS = 1e-5

LANES = 128
SUBLANES = 8
VMEM_LIMIT_BYTES = 56 * 1024 * 1024

MM_TM = 768
MOE_TM = 256
MOE_ROWS = 2 * T_ALL + N_EXPERTS * MOE_TM
MOE_TILES = MOE_ROWS // MOE_TM
LN_TM = 128
SSD_Q_PROMPT = 128


def _cparams(semantics):
    return pltpu.CompilerParams(dimension_semantics=semantics, vmem_limit_bytes=VMEM_LIMIT_BYTES)


def _dot(a, b):
    return jnp.dot(a, b, preferred_element_type=F32)


def _dot_nt(a, b):
    return lax.dot_general(a, b, (((1,), (1,)), ((), ())), preferred_element_type=F32)


def _dot_tn(a, b):
    return lax.dot_general(a, b, (((0,), (0,)), ((), ())), preferred_element_type=F32)


def _split3(v):
    hi = v.astype(BF16)
    r = v - hi.astype(F32)
    mid = r.astype(BF16)
    lo = (r - mid.astype(F32)).astype(BF16)
    return hi, mid, lo


def _dot_sel(sel, v):
    hi, mid, lo = _split3(v)
    return _dot(sel, hi) + _dot(sel, mid) + _dot(sel, lo)


def _dot_sel_r(v, sel):
    hi, mid, lo = _split3(v)
    return _dot(hi, sel) + _dot(mid, sel) + _dot(lo, sel)


def _silu(x):
    return x * jax.nn.sigmoid(x)


def _gelu_exact(x):
    return 0.5 * x * (1.0 + lax.erf(x * (1.0 / math.sqrt(2.0))))


def _mm_kernel(x_ref, w_ref, o_ref):
    o_ref[...] = _dot(x_ref[...], w_ref[...]).astype(o_ref.dtype)


def matmul(x, w, *, tm, tn, out_dtype=F32):
    m, k = x.shape
    n = w.shape[1]
    return pl.pallas_call(
        _mm_kernel,
        out_shape=jax.ShapeDtypeStruct((m, n), out_dtype),
        grid=(n // tn, m // tm),
        in_specs=[pl.BlockSpec((tm, k), lambda j, i: (i, 0)),
                  pl.BlockSpec((k, tn), lambda j, i: (0, j))],
        out_specs=pl.BlockSpec((tm, tn), lambda j, i: (i, j)),
        compiler_params=_cparams(("parallel", "parallel")),
        name="matmul",
    )(x, w)


def _glu_kernel(x_ref, w1_ref, w2_ref, o_ref):
    x = x_ref[...]
    z1 = _dot(x, w1_ref[...])
    z2 = _dot(x, w2_ref[...])
    o_ref[...] = z1 * jax.nn.sigmoid(z2)


def glu_matmul(x, w1, w2, *, tm, tn):
    m, k = x.shape
    n = w1.shape[1]
    return pl.pallas_call(
        _glu_kernel,
        out_shape=jax.ShapeDtypeStruct((m, n), F32),
        grid=(n // tn, m // tm),
        in_specs=[pl.BlockSpec((tm, k), lambda j, i: (i, 0)),
                  pl.BlockSpec((k, tn), lambda j, i: (0, j)),
                  pl.BlockSpec((k, tn), lambda j, i: (0, j))],
        out_specs=pl.BlockSpec((tm, tn), lambda j, i: (i, j)),
        compiler_params=_cparams(("parallel", "parallel")),
        name="glu_matmul",
    )(x, w1, w2)


def _layer_norm(v, g, b):
    mu = jnp.mean(v, axis=-1, keepdims=True)
    d = v - mu
    var = jnp.mean(d * d, axis=-1, keepdims=True)
    return d * lax.rsqrt(var + LN_EPS) * g + b


def _route_rows(xb, rw, rb):
    tm = xb.shape[0]
    lane = lax.broadcasted_iota(I32, (tm, LANES), 1)
    live = lane < N_EXPERTS
    logits = _dot(xb, rw) + rb
    logits = jnp.where(live, logits, -jnp.inf)
    m = jnp.max(logits, axis=-1, keepdims=True)
    e = jnp.exp(logits - m)
    probs = e / jnp.sum(e, axis=-1, keepdims=True)
    best = None
    for g in range(N_EXPERTS // EXPERTS_PER_GROUP):
        in_g = (lane >= g * EXPERTS_PER_GROUP) & (lane < (g + 1) * EXPERTS_PER_GROUP)
        score = jnp.max(jnp.where(in_g, probs, -1.0), axis=-1, keepdims=True)
        if best is None:
            best, best_g = score, jnp.zeros((tm, 1), I32)
        else:
            upd = score > best
            best = jnp.where(upd, score, best)
            best_g = jnp.where(upd, g, best_g)
    lo = best_g * EXPERTS_PER_GROUP
    in_best = (lane >= lo) & (lane < lo + EXPERTS_PER_GROUP)
    cand = jnp.where(in_best, probs, -1.0)
    p1 = jnp.max(cand, axis=-1, keepdims=True)
    i1 = jnp.min(jnp.where(cand == p1, lane, LANES), axis=-1, keepdims=True)
    cand2 = jnp.where(lane == i1, -2.0, cand)
    p2 = jnp.max(cand2, axis=-1, keepdims=True)
    i2 = jnp.min(jnp.where(cand2 == p2, lane, LANES), axis=-1, keepdims=True)
    tot = p1 + p2
    idx = jnp.where(lane == 0, i1, jnp.where(lane == 1, i2, 0))
    gate = jnp.where(lane == 0, p1 / tot, jnp.where(lane == 1, p2 / tot, 0.0))
    return idx, gate


def _ln_router_kernel(x_ref, mix_ref, g_ref, b_ref, rw_ref, rb_ref, xo_ref, xb_ref, idx_ref, gate_ref):
    y = _layer_norm(DEEPNORM_ALPHA * x_ref[...] + mix_ref[...], g_ref[...], b_ref[...])
    xo_ref[...] = y
    yb = y.astype(BF16)
    xb_ref[...] = yb
    idx, gate = _route_rows(yb, rw_ref[...], rb_ref[...])
    idx_ref[...] = idx
    gate_ref[...] = gate


def ln_router(x, mix, g, b, rw, rb):
    t, d = x.shape
    tm = LN_TM
    row = pl.BlockSpec((tm, d), lambda i: (i, 0))
    vec = pl.BlockSpec((1, d), lambda i: (0, 0))
    small = pl.BlockSpec((tm, LANES), lambda i: (i, 0))
    return pl.pallas_call(
        _ln_router_kernel,
        out_shape=(jax.ShapeDtypeStruct((t, d), F32), jax.ShapeDtypeStruct((t, d), BF16),
                   jax.ShapeDtypeStruct((t, LANES), I32), jax.ShapeDtypeStruct((t, LANES), F32)),
        grid=(t // tm,),
        in_specs=[row, row, vec, vec, pl.BlockSpec((d, LANES), lambda i: (0, 0)),
                  pl.BlockSpec((1, LANES), lambda i: (0, 0))],
        out_specs=(row, row, small, small),
        compiler_params=_cparams(("parallel",)),
        name="ln_router",
    )(x, mix, g, b, rw, rb)


def _row_gather_start(src_hbm, row, dst, dst_row, sem):
    pltpu.make_async_copy(src_hbm.at[pl.ds(row, 1)], dst.at[pl.ds(dst_row, 1)], sem).start()


def _moe_up_kernel(te_ref, src_ref, nv_ref, x_hbm, wg_ref, wu_ref, h_ref, xbuf, sem):
    del te_ref
    i = pl.program_id(0)
    slot = lax.rem(i, 2)
    nvalid = nv_ref[0]

    def issue(tile, s):
        def body(r, carry):
            _row_gather_start(x_hbm, src_ref[tile * MOE_TM + r], xbuf.at[s], r, sem.at[s])
            return carry
        lax.fori_loop(0, MOE_TM, body, 0)

    @pl.when(i == 0)
    def _():
        issue(0, 0)

    @pl.when(i + 1 < nvalid)
    def _():
        issue(i + 1, 1 - slot)

    @pl.when(i < nvalid)
    def _():
        pltpu.make_async_copy(xbuf.at[slot], xbuf.at[slot], sem.at[slot]).wait()
        xb = xbuf[slot].astype(BF16)
        gate = _dot(xb, wg_ref[0])
        up = _dot(xb, wu_ref[0])
        h_ref[...] = (_silu(gate) * up).astype(BF16)

    @pl.when(i >= nvalid)
    def _():
        h_ref[...] = jnp.zeros_like(h_ref)


def moe_up(tile_expert, src_rows, nvalid, x, wg, wu):
    d = x.shape[1]
    grid_spec = pltpu.PrefetchScalarGridSpec(
        num_scalar_prefetch=3,
        grid=(MOE_TILES,),
        in_specs=[pl.BlockSpec(memory_space=pl.ANY),
                  pl.BlockSpec((1, d, D_EXPERT), lambda i, te, src, nv: (te[i], 0, 0)),
                  pl.BlockSpec((1, d, D_EXPERT), lambda i, te, src, nv: (te[i], 0, 0))],
        out_specs=pl.BlockSpec((MOE_TM, D_EXPERT), lambda i, te, src, nv: (i, 0)),
        scratch_shapes=[pltpu.VMEM((2, MOE_TM, d), F32), pltpu.SemaphoreType.DMA((2,))],
    )
    return pl.pallas_call(
        _moe_up_kernel,
        out_shape=jax.ShapeDtypeStruct((MOE_ROWS, D_EXPERT), BF16),
        grid_spec=grid_spec,
        compiler_params=_cparams(("arbitrary",)),
        name="moe_up",
    )(tile_expert, src_rows, nvalid, x, wg, wu)


def _moe_down_kernel(te_ref, nv_ref, h_ref, gate_ref, wd_ref, y_ref):
    del te_ref
    i = pl.program_id(0)

    @pl.when(i < nv_ref[0])
    def _():
        y_ref[...] = _dot(h_ref[...], wd_ref[0]) * gate_ref[...]

    @pl.when(i >= nv_ref[0])
    def _():
        y_ref[...] = jnp.zeros_like(y_ref)


def moe_down(tile_expert, nvalid, h, gate_sorted, wd):
    d = wd.shape[2]
    grid_spec = pltpu.PrefetchScalarGridSpec(
        num_scalar_prefetch=2,
        grid=(MOE_TILES,),
        in_specs=[pl.BlockSpec((MOE_TM, D_EXPERT), lambda i, te, nv: (i, 0)),
                  pl.BlockSpec((MOE_TM, 1), lambda i, te, nv: (i, 0)),
                  pl.BlockSpec((1, D_EXPERT, d), lambda i, te, nv: (te[i], 0, 0))],
        out_specs=pl.BlockSpec((MOE_TM, d), lambda i, te, nv: (i, 0)),
    )
    return pl.pallas_call(
        _moe_down_kernel,
        out_shape=jax.ShapeDtypeStruct((MOE_ROWS, d), F32),
        grid_spec=grid_spec,
        compiler_params=_cparams(("arbitrary",)),
        name="moe_down",
    )(tile_expert, nvalid, h, gate_sorted, wd)


def _combine_ln_kernel(pos_ref, x_ref, y_hbm, g_ref, b_ref, *rest, tm, tile0, ntiles, with_router):
    if with_router:
        rw_ref, rb_ref, xo_ref, xb_ref, idx_ref, gate_ref, ybuf, sem = rest
    else:
        xo_ref, ybuf, sem = rest
    i = pl.program_id(0)
    slot = lax.rem(i, 2)

    def issue(tile, s):
        base = 2 * (tile0 + tile) * tm

        def body(r, carry):
            _row_gather_start(y_hbm, pos_ref[base + 2 * r], ybuf.at[s, 0], r, sem.at[s])
            _row_gather_start(y_hbm, pos_ref[base + 2 * r + 1], ybuf.at[s, 1], r, sem.at[s])
            return carry
        lax.fori_loop(0, tm, body, 0)

    @pl.when(i == 0)
    def _():
        issue(0, 0)

    @pl.when(i + 1 < ntiles)
    def _():
        issue(i + 1, 1 - slot)

    pltpu.make_async_copy(ybuf.at[slot], ybuf.at[slot], sem.at[slot]).wait()
    ffn = ybuf[slot, 0] + ybuf[slot, 1]
    y = _layer_norm(DEEPNORM_ALPHA * x_ref[...] + ffn, g_ref[...], b_ref[...])
    xo_ref[...] = y
    if with_router:
        yb = y.astype(BF16)
        xb_ref[...] = yb
        idx, gate = _route_rows(yb, rw_ref[...], rb_ref[...])
        idx_ref[...] = idx
        gate_ref[...] = gate


def combine_ln(pos, x, y_sorted, g, b, *, row0=0, nrows=None, router=None):
    d = x.shape[1]
    tm = LN_TM
    nrows = x.shape[0] if nrows is None else nrows
    tile0 = row0 // tm
    ntiles = nrows // tm
    row_in = pl.BlockSpec((tm, d), lambda i, p: (tile0 + i, 0))
    row_out = pl.BlockSpec((tm, d), lambda i, p: (i, 0))
    vec = pl.BlockSpec((1, d), lambda i, p: (0, 0))
    small = pl.BlockSpec((tm, LANES), lambda i, p: (i, 0))
    in_specs = [row_in, pl.BlockSpec(memory_space=pl.ANY), vec, vec]
    args = [pos, x, y_sorted, g, b]
    out_shape = [jax.ShapeDtypeStruct((nrows, d), F32)]
    out_specs = [row_out]
    if router is not None:
        in_specs += [pl.BlockSpec((d, LANES), lambda i, p: (0, 0)), pl.BlockSpec((1, LANES), lambda i, p: (0, 0))]
        args += list(router)
        out_shape += [jax.ShapeDtypeStruct((nrows, d), BF16), jax.ShapeDtypeStruct((nrows, LANES), I32),
                      jax.ShapeDtypeStruct((nrows, LANES), F32)]
        out_specs += [row_out, small, small]
    grid_spec = pltpu.PrefetchScalarGridSpec(
        num_scalar_prefetch=1,
        grid=(ntiles,),
        in_specs=in_specs,
        out_specs=tuple(out_specs),
        scratch_shapes=[pltpu.VMEM((2, 2, tm, d), F32), pltpu.SemaphoreType.DMA((2,))],
    )
    kern = functools.partial(_combine_ln_kernel, tm=tm, tile0=tile0, ntiles=ntiles,
                             with_router=router is not None)
    return pl.pallas_call(
        kern,
        out_shape=tuple(out_shape),
        grid_spec=grid_spec,
        compiler_params=_cparams(("arbitrary",)),
        name="combine_ln",
    )(*args)


def _route_tables(idx, gate):
    t = idx.shape[0]
    e_flat = idx[:, :2].reshape(-1)
    g_flat = gate[:, :2].reshape(-1)
    onehot = (e_flat[:, None] == jnp.arange(N_EXPERTS, dtype=I32)[None, :]).astype(I32)
    csum = jnp.cumsum(onehot, axis=0)
    rank = jnp.take_along_axis(csum, e_flat[:, None], axis=1)[:, 0] - 1
    counts = csum[-1]
    padded = ((counts + MOE_TM - 1) // MOE_TM) * MOE_TM
    ends = jnp.cumsum(padded)
    starts = ends - padded
    pos = (starts[e_flat] + rank).astype(I32)
    src = jnp.zeros((MOE_ROWS,), I32).at[pos].set(jnp.arange(2 * t, dtype=I32) // 2)
    gate_sorted = jnp.zeros((MOE_ROWS,), F32).at[pos].set(g_flat).reshape(MOE_ROWS, 1)
    nvalid = (ends[-1] // MOE_TM).astype(I32)
    tile_start = jnp.arange(MOE_TILES, dtype=I32) * MOE_TM
    te = jnp.sum((tile_start[:, None] >= ends[None, :]).astype(I32), axis=1)
    te = jnp.minimum(te, te[nvalid - 1]).astype(I32)
    return pos, src, gate_sorted, te, nvalid.reshape(1)


def moe_layer(x_f32, idx, gate, wg, wu, wd):
    pos, src, gate_sorted, te, nvalid = _route_tables(idx, gate)
    h = moe_up(te, src, nvalid, x_f32, wg, wu)
    y_sorted = moe_down(te, nvalid, h, gate_sorted, wd)
    return pos, y_sorted


def _conv_kernel(x_ref, w_ref, b_ref, prev_ref, o_ref, halo):
    r = pl.program_id(2)

    @pl.when(r == 0)
    def _():
        halo[...] = prev_ref[0]

    x = x_ref[...]
    prev8 = halo[...]
    row8 = lax.broadcasted_iota(I32, prev8.shape, 0)
    acc = jnp.broadcast_to(b_ref[...], x.shape)
    for k in range(SSD_CONV_W):
        s = SSD_CONV_W - 1 - k
        if s == 0:
            xs = x
        else:
            xs = pltpu.roll(x, s, axis=0)
            top = jnp.where(row8 < s, pltpu.roll(prev8, s, axis=0), xs[:SUBLANES])
            xs = jnp.concatenate([top, xs[SUBLANES:]], axis=0)
        acc = acc + xs * w_ref[k:k + 1, :]
    halo[...] = x[x.shape[0] - SUBLANES:, :]
    o_ref[...] = _silu(acc)


def conv_silu(xbc, w, b, prev8, *, row0, nseq, seqlen, tr, out_alias=None):
    t, c = xbc.shape
    tc = 1024
    nrt = seqlen // tr
    rb0 = row0 // tr
    in_specs = [pl.BlockSpec((tr, tc), lambda j, s, r: (rb0 + s * nrt + r, j)),
                pl.BlockSpec((SSD_CONV_W, tc), lambda j, s, r: (0, j)),
                pl.BlockSpec((1, tc), lambda j, s, r: (0, j)),
                pl.BlockSpec((1, SUBLANES, tc), lambda j, s, r: (s, 0, j))]
    args = [xbc, w, b, prev8]
    aliases = {}
    kern = _conv_kernel
    if out_alias is not None:
        in_specs.append(pl.BlockSpec(memory_space=pl.ANY))
        args.append(out_alias)
        aliases = {4: 0}
        kern = lambda x, w_, b_, p, _a, o, h: _conv_kernel(x, w_, b_, p, o, h)
    return pl.pallas_call(
        kern,
        out_shape=jax.ShapeDtypeStruct((t, c), F32),
        grid=(c // tc, nseq, nrt),
        in_specs=in_specs,
        out_specs=pl.BlockSpec((tr, tc), lambda j, s, r: (rb0 + s * nrt + r, j)),
        scratch_shapes=[pltpu.VMEM((SUBLANES, tc), F32)],
        input_output_aliases=aliases,
        compiler_params=_cparams(("parallel", "parallel", "arbitrary")),
        name="conv_silu",
    )(*args)


def _dt_kernel(dt_ref, bias_ref, alog_ref, dt_o, dta_o):
    v = dt_ref[...] + bias_ref[...]
    sp = jnp.maximum(v, 0.0) + jnp.log1p(jnp.exp(-jnp.abs(v)))
    dt_o[...] = sp
    dta_o[...] = sp * (-jnp.exp(alog_ref[...]))


def dt_prep(dt_raw, bias, a_log):
    t, h = dt_raw.shape
    tr = t // 8
    row = pl.BlockSpec((tr, h), lambda i: (i, 0))
    vec = pl.BlockSpec((1, h), lambda i: (0, 0))
    return pl.pallas_call(
        _dt_kernel,
        out_shape=(jax.ShapeDtypeStruct((t, h), F32), jax.ShapeDtypeStruct((t, h), F32)),
        grid=(8,),
        in_specs=[row, vec, vec],
        out_specs=(row, row),
        compiler_params=_cparams(("parallel",)),
        name="dt_prep",
    )(dt_raw, bias, a_log)


def _ssd_kernel(x_ref, b_ref, c_ref, z_ref, dt_ref, dta_ref, dtt_ref, dtat_ref, h0_ref, dskip_ref, nw_ref,
                y_ref, hout_ref, state, *, q, nchunks):
    c = pl.program_id(2)
    hg = SSD_HEADS_PER_GROUP
    p = SSD_HEAD_DIM

    @pl.when(c == 0)
    def _():
        state[...] = h0_ref[0, 0].T

    x = x_ref[...]
    xb = x.astype(BF16)
    bm = b_ref[...].astype(BF16)
    cm = c_ref[...].astype(BF16)
    dt = dt_ref[0, 0]
    dta = dta_ref[0, 0]
    dtt = dtt_ref[0, 0]
    dtat = dtat_ref[0, 0]

    ri = lax.broadcasted_iota(I32, (q, q), 0)
    ci = lax.broadcasted_iota(I32, (q, q), 1)
    causal = ri >= ci
    lower = jnp.where(causal, 1.0, 0.0).astype(BF16)
    upper = jnp.where(ri <= ci, 1.0, 0.0).astype(BF16)
    acum = _dot_sel(lower, dta)
    acumt = _dot_sel_r(dtat, upper)
    alast = acum[q - 1:q, :]

    cb = _dot_nt(cm, bm)
    lane = lax.broadcasted_iota(I32, (q, 2 * p), 1)
    y_pairs = []
    for hp in range(hg // 2):
        ms = []
        for h in (2 * hp, 2 * hp + 1):
            seg = acum[:, h:h + 1] - acumt[h:h + 1, :]
            decay = jnp.exp(jnp.where(causal, seg, -jnp.inf))
            ms.append((cb * decay * dtt[h:h + 1, :]).astype(BF16))
        lhs = jnp.concatenate(ms, axis=1)
        xp = xb[:, 2 * p * hp:2 * p * (hp + 1)]
        zero = jnp.zeros_like(xp)
        rhs = jnp.concatenate([jnp.where(lane < p, xp, zero), jnp.where(lane >= p, xp, zero)], axis=0)
        y_pairs.append(_dot(lhs, rhs))
    y = jnp.concatenate(y_pairs, axis=1)

    hi = lax.broadcasted_iota(I32, (hg, hg * p), 0)
    li = lax.broadcasted_iota(I32, (hg, hg * p), 1)
    widen = jnp.where((li >= hi * p) & (li < (hi + 1) * p), 1.0, 0.0).astype(BF16)
    scales = jnp.concatenate([jnp.exp(acum), jnp.exp(alast - acum) * dt,
                              jnp.broadcast_to(jnp.exp(alast), (SUBLANES, hg))], axis=0)
    wide = _dot_sel_r(scales, widen)
    e_in = wide[:q]
    e_out = wide[q:2 * q]
    e_all = wide[2 * q:2 * q + 1]

    s_old = state[...]
    y = y + _dot(cm, s_old.astype(BF16)) * e_in
    xw = (x * e_out).astype(BF16)
    s_new = s_old * e_all + _dot_tn(bm, xw)
    state[...] = s_new

    y = y + dskip_ref[...] * x
    y = y * _silu(z_ref[...])
    y = y * lax.rsqrt(jnp.mean(y * y, axis=-1, keepdims=True) + RMS_EPS)
    y_ref[...] = (y * nw_ref[...]).astype(BF16)

    @pl.when(c == nchunks - 1)
    def _():
        hout_ref[0, 0] = s_new.T


def ssd_scan(xbc_act, z, dt4, dta4, dtt4, dtat4, h0, dskip, norm_w, *, row0, nseq, seqlen, q, out_alias=None):
    t = xbc_act.shape[0]
    g = SSD_N_GROUPS
    gc = SSD_GROUP_CH
    n = SSD_D_STATE
    nchunks = seqlen // q
    rb0 = row0 // q
    xcol0 = SSD_D_INNER // n

    def rows(b, gg, c):
        return rb0 + b * nchunks + c

    in_specs = [pl.BlockSpec((q, gc), lambda b, gg, c: (rows(b, gg, c), gg)),
                pl.BlockSpec((q, n), lambda b, gg, c: (rows(b, gg, c), xcol0 + gg)),
                pl.BlockSpec((q, n), lambda b, gg, c: (rows(b, gg, c), xcol0 + g + gg)),
                pl.BlockSpec((q, gc), lambda b, gg, c: (rows(b, gg, c), gg)),
                pl.BlockSpec((1, 1, q, 16), lambda b, gg, c: (gg, b * nchunks + c, 0, 0)),
                pl.BlockSpec((1, 1, q, 16), lambda b, gg, c: (gg, b * nchunks + c, 0, 0)),
                pl.BlockSpec((1, 1, 16, q), lambda b, gg, c: (gg, b * nchunks + c, 0, 0)),
                pl.BlockSpec((1, 1, 16, q), lambda b, gg, c: (gg, b * nchunks + c, 0, 0)),
                pl.BlockSpec((1, 1, gc, n), lambda b, gg, c: (b, gg, 0, 0)),
                pl.BlockSpec((1, gc), lambda b, gg, c: (0, gg)),
                pl.BlockSpec((1, gc), lambda b, gg, c: (0, gg))]
    args = [xbc_act, xbc_act, xbc_act, z, dt4, dta4, dtt4, dtat4, h0, dskip, norm_w]
    kern = functools.partial(_ssd_kernel, q=q, nchunks=nchunks)
    aliases = {}
    if out_alias is not None:
        in_specs.append(pl.BlockSpec(memory_space=pl.ANY))
        args.append(out_alias)
        aliases = {len(args) - 1: 0}
        base = kern
        kern = lambda *refs: base(*refs[:11], *refs[12:])
    return pl.pallas_call(
        kern,
        out_shape=(jax.ShapeDtypeStruct((t, SSD_D_INNER), BF16),
                   jax.ShapeDtypeStruct((nseq, g, gc, n), F32)),
        grid=(nseq, g, nchunks),
        in_specs=in_specs,
        out_specs=(pl.BlockSpec((q, gc), lambda b, gg, c: (rows(b, gg, c), gg)),
                   pl.BlockSpec((1, 1, gc, n), lambda b, gg, c: (b, gg, 0, 0))),
        scratch_shapes=[pltpu.VMEM((n, gc), F32)],
        input_output_aliases=aliases,
        compiler_params=_cparams(("parallel", "parallel", "arbitrary")),
        name="ssd_scan",
    )(*args)


def _dt_layouts(v, row0, nseq, seqlen, q):
    nb = nseq * seqlen // q
    part = v[row0:row0 + nseq * seqlen].reshape(nb, q, SSD_N_GROUPS, SSD_HEADS_PER_GROUP)
    return part.transpose(2, 0, 1, 3), part.transpose(2, 0, 3, 1)


def _s5_prep_kernel(lr_ref, li_ref, ls_ref, bre_ref, bim_ref, cre_ref, cim_ref, wst_ref, vt_ref, krev_ref, aq_ref):
    lr = lr_ref[0]
    li = li_ref[0]
    dstep = jnp.exp(ls_ref[0])
    mag = jnp.exp(lr * dstep)
    ar = mag * jnp.cos(li * dstep)
    ai = mag * jnp.sin(li * dstep)
    den = lr * lr + li * li
    cr = ((ar - 1.0) * lr + ai * li) / den
    ci = (ai * lr - (ar - 1.0) * li) / den
    bre = bre_ref[0]
    bim = bim_ref[0]
    bbr = cr * bre - ci * bim
    bbi = cr * bim + ci * bre
    bmat = jnp.concatenate([bbr, bbi], axis=1)
    cre = cre_ref[0]
    cim = cim_ref[0]

    def power(m):
        e = jnp.exp((m * lr) * dstep)
        return e * jnp.cos((m * li) * dstep), e * jnp.sin((m * li) * dstep)

    powers = [power(float(m)) for m in range(S5_Q + 1)]

    def out_rows(m):
        pr, pi = powers[m]
        return jnp.concatenate([cre * pr - cim * pi, -(cre * pi + cim * pr)], axis=1)

    for t in range(S5_Q):
        pr, pi = powers[S5_Q - 1 - t]
        blk = jnp.concatenate([pr * bbr - pi * bbi, pr * bbi + pi * bbr], axis=1)
        wst_ref[0, t * LANES:(t + 1) * LANES, :] = blk.astype(BF16)
        vt_ref[0, t * LANES:(t + 1) * LANES, :] = out_rows(t + 1).astype(BF16)
        lag = S5_Q - 1 - t
        kk = lax.dot_general(bmat, out_rows(lag), (((1,), (1,)), ((), ())),
                             precision=lax.Precision.HIGHEST, preferred_element_type=F32)
        krev_ref[0, t * LANES:(t + 1) * LANES, :] = kk.astype(BF16)
    pr, pi = powers[S5_Q]
    aq_ref[0] = jnp.concatenate([pr, pi], axis=1)


def s5_prep(lr, li, ls, bre, bim, cre, cim):
    np_ = S5_N_PACKS
    sp = S5_PACK_STATE
    vec = pl.BlockSpec((1, 1, sp), lambda i: (i, 0, 0))
    mat = pl.BlockSpec((1, LANES, sp), lambda i: (i, 0, 0))
    big = pl.BlockSpec((1, S5_Q * LANES, 2 * sp), lambda i: (i, 0, 0))
    return pl.pallas_call(
        _s5_prep_kernel,
        out_shape=(jax.ShapeDtypeStruct((np_, S5_Q * LANES, 2 * sp), BF16),
                   jax.ShapeDtypeStruct((np_, S5_Q * LANES, 2 * sp), BF16),
                   jax.ShapeDtypeStruct((np_, S5_Q * LANES, LANES), BF16),
                   jax.ShapeDtypeStruct((np_, 1, 2 * sp), F32)),
        grid=(np_,),
        in_specs=[vec, vec, vec, mat, mat, mat, mat],
        out_specs=(big, big, pl.BlockSpec((1, S5_Q * LANES, LANES), lambda i: (i, 0, 0)),
                   pl.BlockSpec((1, 1, 2 * sp), lambda i: (i, 0, 0))),
        compiler_params=_cparams(("parallel",)),
        name="s5_prep",
    )(lr, li, ls, bre, bim, cre, cim)


def _s5_kernel(x_ref, wst_ref, vt_ref, krev_ref, aq_ref, d_ref, h0_ref, g_ref, hout_ref, *scratch, nblk, chain):
    q = S5_Q
    sp = S5_PACK_STATE
    us = [x_ref[pl.ds(t, nblk, stride=q), :] for t in range(q)]
    ucat = jnp.concatenate([u.astype(BF16) for u in us], axis=1)
    s_in = _dot(ucat, wst_ref[0])
    aq = aq_ref[0]
    aqr, aqi = aq[:, :sp], aq[:, sp:]

    def advance(h, s):
        hr, hi = h[:, :sp], h[:, sp:]
        return jnp.concatenate([aqr * hr - aqi * hi + s[:, :sp], aqr * hi + aqi * hr + s[:, sp:]], axis=1)

    if chain:
        hin_s, s_s, y_s = scratch
        s_s[...] = s_in
        h0 = h0_ref[0, 0]
        hin_s[pl.ds(0, 1), :] = h0

        def body(r, h):
            hn = advance(h, s_s[pl.ds(r, 1), :])
            hin_s[pl.ds(r + 1, 1), :] = hn
            return hn

        hout_ref[0, 0] = lax.fori_loop(0, nblk, body, h0)
        hin = hin_s[pl.ds(0, nblk), :]
    else:
        (y_s,) = scratch
        hin = h0_ref[0]
        hout_ref[0] = advance(hin, s_in)
    yh = _dot_nt(hin.astype(BF16), vt_ref[0])
    d = d_ref[...]
    for t in range(q):
        y = yh[:, t * LANES:(t + 1) * LANES]
        y = y + _dot(ucat[:, :(t + 1) * LANES], krev_ref[0, (q - 1 - t) * LANES:, :])
        y = y + d * us[t]
        y_s[pl.ds(t, nblk, stride=q), :] = _gelu_exact(y)
    g_ref[...] = y_s[...].astype(BF16)


def s5_scan(x, wst, vt, krev, aq, dskip, h0, *, row0, nseq, seqlen, chain, out_alias=None):
    t, d = x.shape
    q = S5_Q
    sp2 = 2 * S5_PACK_STATE
    if chain:
        rows = seqlen
        nblk = seqlen // q
        grid = (S5_N_PACKS, nseq)
        rb0 = row0 // rows
        xmap = lambda p, b: (rb0 + b, p)
        wmap = lambda p, b: (p, 0, 0)
        dmap = lambda p, b: (0, p)
        hspec = pl.BlockSpec((1, 1, 1, sp2), lambda p, b: (b, p, 0, 0))
        hshape = (nseq, S5_N_PACKS, 1, sp2)
        scratch = [pltpu.VMEM((nblk + SUBLANES, sp2), F32), pltpu.VMEM((nblk, sp2), F32),
                   pltpu.VMEM((rows, LANES), F32)]
        sem = ("parallel", "arbitrary")
    else:
        rows = nseq * seqlen
        nblk = nseq
        assert seqlen == q
        grid = (S5_N_PACKS,)
        rb0 = row0 // rows
        xmap = lambda p: (rb0, p)
        wmap = lambda p: (p, 0, 0)
        dmap = lambda p: (0, p)
        hspec = pl.BlockSpec((1, nseq, sp2), lambda p: (p, 0, 0))
        hshape = (S5_N_PACKS, nseq, sp2)
        scratch = [pltpu.VMEM((rows, LANES), F32)]
        sem = ("parallel",)
    in_specs = [pl.BlockSpec((rows, LANES), xmap),
                pl.BlockSpec((1, q * LANES, sp2), wmap),
                pl.BlockSpec((1, q * LANES, sp2), wmap),
                pl.BlockSpec((1, q * LANES, LANES), wmap),
                pl.BlockSpec((1, 1, sp2), wmap),
                pl.BlockSpec((1, LANES), dmap),
                hspec]
    args = [x, wst, vt, krev, aq, dskip, h0]
    kern = functools.partial(_s5_kernel, nblk=nblk, chain=chain)
    aliases = {}
    if out_alias is not None:
        in_specs.append(pl.BlockSpec(memory_space=pl.ANY))
        args.append(out_alias)
        aliases = {len(args) - 1: 0}
        base = kern
        kern = lambda *refs: base(*refs[:7], *refs[8:])
    return pl.pallas_call(
        kern,
        out_shape=(jax.ShapeDtypeStruct((t, d), BF16), jax.ShapeDtypeStruct(hshape, F32)),
        grid=grid,
        in_specs=in_specs,
        out_specs=(pl.BlockSpec((rows, LANES), xmap), hspec),
        scratch_shapes=scratch,
        input_output_aliases=aliases,
        compiler_params=_cparams(sem),
        name="s5_scan",
    )(*args)


def _s5_block_diag(m):
    m = m.reshape(S5_N_PACKS, S5_PACK_GROUPS, S5_GROUP_CH, S5_STATE)
    eye = jnp.eye(S5_PACK_GROUPS, dtype=m.dtype)
    out = m[:, :, :, None, :] * eye[None, :, None, :, None]
    return out.reshape(S5_N_PACKS, S5_PACK_GROUPS * S5_GROUP_CH, S5_PACK_STATE)


def _s5_pack_state(re, im):
    b = re.shape[0]
    return jnp.concatenate([re.reshape(b, S5_N_PACKS, S5_PACK_STATE), im.reshape(b, S5_N_PACKS, S5_PACK_STATE)], axis=-1)


def _s5_unpack_state(h):
    b = h.shape[0]
    return (h[..., :S5_PACK_STATE].reshape(b, S5_N_GROUPS, S5_STATE),
            h[..., S5_PACK_STATE:].reshape(b, S5_N_GROUPS, S5_STATE))


def kernel(x_prompt, x_sample, cache_ssd_conv, state_ssd, state_s5_re, state_s5_im, ssd_w_in, ssd_conv_w, ssd_conv_b, ssd_dt_bias, ssd_a_log, ssd_d, ssd_norm_w, ssd_w_out, s5_lambda_re, s5_lambda_im, s5_log_step, s5_b_re, s5_b_im, s5_c_re, s5_c_im, s5_d, s5_w_glu, ln_mix_g, ln_mix_b, ln_ffn_g, ln_ffn_b, router_w, router_b, moe_w_gate, moe_w_up, moe_w_down):
    d = D_MODEL
    x0 = jnp.concatenate([x_prompt.reshape(T_PROMPT, d), x_sample.reshape(T_SAMPLE, d)], axis=0)
    x0b = x0.astype(BF16)
    rw = jnp.zeros((d, LANES), BF16).at[:, :N_EXPERTS].set(router_w.astype(BF16))
    rb = jnp.zeros((1, LANES), F32).at[0, :N_EXPERTS].set(router_b)
    router = (rw, rb)

    w_in = ssd_w_in[0].astype(BF16)
    z = matmul(x0b, w_in[:, :SSD_D_INNER], tm=MM_TM, tn=1024)
    xbc = matmul(x0b, w_in[:, SSD_D_INNER:SSD_D_INNER + SSD_CONV_DIM], tm=MM_TM, tn=1024)
    dt_raw = matmul(x0b, w_in[:, SSD_D_INNER + SSD_CONV_DIM:], tm=MM_TM, tn=LANES)

    conv_w = ssd_conv_w[0]
    conv_b = ssd_conv_b[0].reshape(1, SSD_CONV_DIM)
    prev_p = jnp.zeros((BATCH, SUBLANES, SSD_CONV_DIM), F32)
    prev_s = jnp.concatenate([jnp.zeros((DEC_BATCH, SUBLANES - (SSD_CONV_W - 1), SSD_CONV_DIM), F32),
                              cache_ssd_conv[0]], axis=1)
    xact = conv_silu(xbc, conv_w, conv_b, prev_p, row0=0, nseq=BATCH, seqlen=SEQ, tr=512)
    xact = conv_silu(xbc, conv_w, conv_b, prev_s, row0=T_PROMPT, nseq=DEC_BATCH, seqlen=DEC_SEQ, tr=DEC_SEQ,
                     out_alias=xact)

    dt_sp, dta = dt_prep(dt_raw, ssd_dt_bias[0].reshape(1, SSD_N_HEADS), ssd_a_log[0].reshape(1, SSD_N_HEADS))
    dskip = jnp.repeat(ssd_d[0], SSD_HEAD_DIM).reshape(1, SSD_D_INNER)
    norm_w = ssd_norm_w[0].reshape(1, SSD_D_INNER)
    h0_p = jnp.zeros((BATCH, SSD_N_GROUPS, SSD_GROUP_CH, SSD_D_STATE), F32)
    h0_s = state_ssd[0].reshape(DEC_BATCH, SSD_N_GROUPS, SSD_GROUP_CH, SSD_D_STATE)
    lay_p = _dt_layouts(dt_sp, 0, BATCH, SEQ, SSD_Q_PROMPT) + _dt_layouts(dta, 0, BATCH, SEQ, SSD_Q_PROMPT)
    lay_s = (_dt_layouts(dt_sp, T_PROMPT, DEC_BATCH, DEC_SEQ, DEC_SEQ)
             + _dt_layouts(dta, T_PROMPT, DEC_BATCH, DEC_SEQ, DEC_SEQ))
    ymix, ssd_p = ssd_scan(xact, z, lay_p[0], lay_p[2], lay_p[1], lay_p[3], h0_p, dskip, norm_w,
                           row0=0, nseq=BATCH, seqlen=SEQ, q=SSD_Q_PROMPT)
    ymix, ssd_s = ssd_scan(xact, z, lay_s[0], lay_s[2], lay_s[1], lay_s[3], h0_s, dskip, norm_w,
                           row0=T_PROMPT, nseq=DEC_BATCH, seqlen=DEC_SEQ, q=DEC_SEQ, out_alias=ymix)
    mix = matmul(ymix, ssd_w_out[0].astype(BF16), tm=MM_TM, tn=512)

    x1, x1b, idx, gate = ln_router(x0, mix, ln_mix_g[0].reshape(1, d), ln_mix_b[0].reshape(1, d), rw, rb)
    del x1b
    pos, y_sorted = moe_layer(x1, idx, gate, moe_w_gate[0].astype(BF16), moe_w_up[0].astype(BF16),
                              moe_w_down[0].astype(BF16))
    (x2,) = combine_ln(pos, x1, y_sorted, ln_ffn_g[0].reshape(1, d), ln_ffn_b[0].reshape(1, d))

    lr = s5_lambda_re[0].reshape(S5_N_PACKS, 1, S5_PACK_STATE)
    li = s5_lambda_im[0].reshape(S5_N_PACKS, 1, S5_PACK_STATE)
    ls = jnp.repeat(s5_log_step[0], S5_STATE).reshape(S5_N_PACKS, 1, S5_PACK_STATE)
    wst, vt, krev, aq = s5_prep(lr, li, ls,
                                _s5_block_diag(s5_b_re[0].transpose(0, 2, 1)),
                                _s5_block_diag(s5_b_im[0].transpose(0, 2, 1)),
                                _s5_block_diag(s5_c_re[0]), _s5_block_diag(s5_c_im[0]))
    s5_dskip = s5_d[0].reshape(1, d)
    hs_p = jnp.zeros((BATCH, S5_N_PACKS, 1, 2 * S5_PACK_STATE), F32)
    hs_s = _s5_pack_state(state_s5_re[0], state_s5_im[0]).transpose(1, 0, 2)
    gact, s5_p = s5_scan(x2, wst, vt, krev, aq, s5_dskip, hs_p, row0=0, nseq=BATCH, seqlen=SEQ, chain=True)
    gact, s5_s = s5_scan(x2, wst, vt, krev, aq, s5_dskip, hs_s, row0=T_PROMPT, nseq=DEC_BATCH, seqlen=DEC_SEQ,
                         chain=False, out_alias=gact)
    w_glu = s5_w_glu[0].astype(BF16)
    mix = glu_matmul(gact, w_glu[:, :d], w_glu[:, d:], tm=MM_TM, tn=512)

    x3, x3b, idx, gate = ln_router(x2, mix, ln_mix_g[1].reshape(1, d), ln_mix_b[1].reshape(1, d), rw, rb)
    del x3b
    pos, y_sorted = moe_layer(x3, idx, gate, moe_w_gate[1].astype(BF16), moe_w_up[1].astype(BF16),
                              moe_w_down[1].astype(BF16))
    g1 = ln_ffn_g[1].reshape(1, d)
    b1 = ln_ffn_b[1].reshape(1, d)
    (y_p,) = combine_ln(pos, x3, y_sorted, g1, b1, row0=0, nrows=T_PROMPT)
    (y_s,) = combine_ln(pos, x3, y_sorted, g1, b1, row0=T_PROMPT, nrows=T_SAMPLE)

    xbc_p = xbc[:T_PROMPT].reshape(BATCH, SEQ, SSD_CONV_DIM)
    xbc_s = xbc[T_PROMPT:].reshape(DEC_BATCH, DEC_SEQ, SSD_CONV_DIM)
    conv_p = xbc_p[:, SEQ - (SSD_CONV_W - 1):][None]
    conv_s = xbc_s[:, DEC_SEQ - (SSD_CONV_W - 1):][None]
    ssd_state_p = ssd_p.reshape(1, BATCH, SSD_N_HEADS, SSD_HEAD_DIM, SSD_D_STATE)
    ssd_state_s = ssd_s.reshape(1, DEC_BATCH, SSD_N_HEADS, SSD_HEAD_DIM, SSD_D_STATE)
    re_p, im_p = _s5_unpack_state(s5_p.reshape(BATCH, S5_N_PACKS, 2 * S5_PACK_STATE))
    re_s, im_s = _s5_unpack_state(s5_s.transpose(1, 0, 2))
    return (y_p.reshape(BATCH, SEQ, d), y_s.reshape(DEC_BATCH, DEC_SEQ, d),
            conv_p, ssd_state_p, re_p[None], im_p[None],
            conv_s, ssd_state_s, re_s[None], im_s[None])
```

```python
import functools
import math

import jax
import jax.numpy as jnp
from jax import lax
from jax.experimental import pallas as pl
from jax.experimental.pallas import tpu as pltpu

F32 = jnp.float32
BF16 = jnp.bfloat16
I32 = jnp.int32

D_MODEL = 4096
BATCH, SEQ = 4, 4096
DEC_BATCH, DEC_SEQ = 32, 16
T_PROMPT = BATCH * SEQ
T_SAMPLE = DEC_BATCH * DEC_SEQ
T_ALL = T_PROMPT + T_SAMPLE
SSD_D_INNER = 8192
SSD_HEAD_DIM = 64
SSD_N_HEADS = 128
SSD_N_GROUPS = 8
SSD_HEADS_PER_GROUP = 16
SSD_D_STATE = 128
SSD_GROUP_CH = SSD_D_INNER // SSD_N_GROUPS
SSD_BC_DIM = SSD_N_GROUPS * SSD_D_STATE
SSD_CONV_DIM = SSD_D_INNER + 2 * SSD_BC_DIM
SSD_CONV_W = 4
S5_GROUP_CH = 16
S5_N_GROUPS = 256
S5_STATE = 64
S5_PACK_GROUPS = 8
S5_N_PACKS = S5_N_GROUPS // S5_PACK_GROUPS
S5_PACK_STATE = S5_PACK_GROUPS * S5_STATE
S5_Q = 16
S5_CHAIN_SEQS = 2
N_EXPERTS = 16
EXPERTS_PER_GROUP = 4
D_EXPERT = 1024
DEPTH = 2
DEEPNORM_ALPHA = (2 * DEPTH) ** 0.25
LN_EPS = 1e-5
RMS_EPS = 1e-5

LANES = 128
SUBLANES = 8
VMEM_LIMIT_BYTES = 56 * 1024 * 1024

MM_TM = 768
MOE_TM = 256
MOE_ROWS = 2 * T_ALL + N_EXPERTS * MOE_TM
MOE_TILES = MOE_ROWS // MOE_TM
LN_TM = 128
SSD_Q_PROMPT = 128
SSD_GROUPS_PER_STEP = 2


def _cparams(semantics):
    return pltpu.CompilerParams(dimension_semantics=semantics, vmem_limit_bytes=VMEM_LIMIT_BYTES)


def _dot(a, b):
    return jnp.dot(a, b, preferred_element_type=F32)


def _dot_nt(a, b):
    return lax.dot_general(a, b, (((1,), (1,)), ((), ())), preferred_element_type=F32)


def _dot_tn(a, b):
    return lax.dot_general(a, b, (((0,), (0,)), ((), ())), preferred_element_type=F32)


def _split3(v):
    hi = v.astype(BF16)
    r = v - hi.astype(F32)
    mid = r.astype(BF16)
    lo = (r - mid.astype(F32)).astype(BF16)
    return hi, mid, lo


def _dot_sel(sel, v):
    hi, mid, lo = _split3(v)
    return _dot(sel, hi) + _dot(sel, mid) + _dot(sel, lo)


def _dot_sel_r(v, sel):
    hi, mid, lo = _split3(v)
    return _dot(hi, sel) + _dot(mid, sel) + _dot(lo, sel)


def _silu(x):
    return x * jax.nn.sigmoid(x)


def _gelu_exact(x):
    return 0.5 * x * (1.0 + lax.erf(x * (1.0 / math.sqrt(2.0))))


def _mm_kernel(x_ref, w_ref, o_ref):
    o_ref[...] = _dot(x_ref[...], w_ref[...]).astype(o_ref.dtype)


def matmul(x, w, *, tm, tn, col0=0, ncols=None, out_dtype=F32):
    m, k = x.shape
    n = w.shape[1] if ncols is None else ncols
    c0 = col0 // tn
    return pl.pallas_call(
        _mm_kernel,
        out_shape=jax.ShapeDtypeStruct((m, n), out_dtype),
        grid=(n // tn, m // tm),
        in_specs=[pl.BlockSpec((tm, k), lambda j, i: (i, 0)),
                  pl.BlockSpec((k, tn), lambda j, i: (0, c0 + j))],
        out_specs=pl.BlockSpec((tm, tn), lambda j, i: (i, j)),
        compiler_params=_cparams(("parallel", "parallel")),
        name="matmul",
    )(x, w)


def _glu_kernel(x_ref, w1_ref, w2_ref, o_ref):
    x = x_ref[...]
    z1 = _dot(x, w1_ref[...])
    z2 = _dot(x, w2_ref[...])
    o_ref[...] = z1 * jax.nn.sigmoid(z2)


def glu_matmul(x, w, *, tm, tn):
    m, k = x.shape
    n = w.shape[1] // 2
    half = n // tn
    return pl.pallas_call(
        _glu_kernel,
        out_shape=jax.ShapeDtypeStruct((m, n), F32),
        grid=(n // tn, m // tm),
        in_specs=[pl.BlockSpec((tm, k), lambda j, i: (i, 0)),
                  pl.BlockSpec((k, tn), lambda j, i: (0, j)),
                  pl.BlockSpec((k, tn), lambda j, i: (0, half + j))],
        out_specs=pl.BlockSpec((tm, tn), lambda j, i: (i, j)),
        compiler_params=_cparams(("parallel", "parallel")),
        name="glu_matmul",
    )(x, w, w)


def _layer_norm(v, g, b):
    mu = jnp.mean(v, axis=-1, keepdims=True)
    d = v - mu
    var = jnp.mean(d * d, axis=-1, keepdims=True)
    return d * lax.rsqrt(var + LN_EPS) * g + b


def _route_rows(xb, rw, rb):
    tm = xb.shape[0]
    lane = lax.broadcasted_iota(I32, (tm, LANES), 1)
    live = lane < N_EXPERTS
    logits = _dot(xb, rw) + rb
    logits = jnp.where(live, logits, -jnp.inf)
    m = jnp.max(logits, axis=-1, keepdims=True)
    e = jnp.exp(logits - m)
    probs = e / jnp.sum(e, axis=-1, keepdims=True)
    best = None
    for g in range(N_EXPERTS // EXPERTS_PER_GROUP):
        in_g = (lane >= g * EXPERTS_PER_GROUP) & (lane < (g + 1) * EXPERTS_PER_GROUP)
        score = jnp.max(jnp.where(in_g, probs, -1.0), axis=-1, keepdims=True)
        if best is None:
            best, best_g = score, jnp.zeros((tm, 1), I32)
        else:
            upd = score > best
            best = jnp.where(upd, score, best)
            best_g = jnp.where(upd, g, best_g)
    lo = best_g * EXPERTS_PER_GROUP
    in_best = (lane >= lo) & (lane < lo + EXPERTS_PER_GROUP)
    cand = jnp.where(in_best, probs, -1.0)
    p1 = jnp.max(cand, axis=-1, keepdims=True)
    i1 = jnp.min(jnp.where(cand == p1, lane, LANES), axis=-1, keepdims=True)
    cand2 = jnp.where(lane == i1, -2.0, cand)
    p2 = jnp.max(cand2, axis=-1, keepdims=True)
    i2 = jnp.min(jnp.where(cand2 == p2, lane, LANES), axis=-1, keepdims=True)
    tot = p1 + p2
    idx = jnp.where(lane == 0, i1, jnp.where(lane == 1, i2, 0))
    gate = jnp.where(lane == 0, p1 / tot, jnp.where(lane == 1, p2 / tot, 0.0))
    return idx, gate


def _ln_router_kernel(x_ref, mix_ref, g_ref, b_ref, rw_ref, rb_ref, xo_ref, xb_ref, idx_ref, gate_ref):
    y = _layer_norm(DEEPNORM_ALPHA * x_ref[...] + mix_ref[...], g_ref[...], b_ref[...])
    xo_ref[...] = y
    yb = y.astype(BF16)
    xb_ref[...] = yb
    idx, gate = _route_rows(yb, rw_ref[...], rb_ref[...])
    idx_ref[...] = idx
    gate_ref[...] = gate


def ln_router(x, mix, g, b, rw, rb):
    t, d = x.shape
    tm = LN_TM
    row = pl.BlockSpec((tm, d), lambda i: (i, 0))
    vec = pl.BlockSpec((1, d), lambda i: (0, 0))
    small = pl.BlockSpec((tm, LANES), lambda i: (i, 0))
    return pl.pallas_call(
        _ln_router_kernel,
        out_shape=(jax.ShapeDtypeStruct((t, d), F32), jax.ShapeDtypeStruct((t, d), BF16),
                   jax.ShapeDtypeStruct((t, LANES), I32), jax.ShapeDtypeStruct((t, LANES), F32)),
        grid=(t // tm,),
        in_specs=[row, row, vec, vec, pl.BlockSpec((d, LANES), lambda i: (0, 0)),
                  pl.BlockSpec((1, LANES), lambda i: (0, 0))],
        out_specs=(row, row, small, small),
        compiler_params=_cparams(("parallel",)),
        name="ln_router",
    )(x, mix, g, b, rw, rb)


def _row_gather_start(src_hbm, row, dst, dst_row, sem):
    pltpu.make_async_copy(src_hbm.at[pl.ds(row, 1)], dst.at[pl.ds(dst_row, 1)], sem).start()


def _slot_wait(buf, slot, sem):
    pltpu.make_async_copy(buf.at[slot], buf.at[slot], sem.at[slot]).wait()


def _moe_up_kernel(te_ref, src_ref, nv_ref, x_hbm, wg_ref, wu_ref, h_ref, xbuf, sem):
    del te_ref
    i = pl.program_id(0)
    slot = lax.rem(i, 2)
    nvalid = nv_ref[0]

    @pl.when(i == 0)
    def _():
        def body(r, carry):
            _row_gather_start(x_hbm, src_ref[r], xbuf.at[0], r, sem.at[0])
            return carry
        lax.fori_loop(0, MOE_TM, body, 0)

    @pl.when(i <= nvalid)
    def _():
        _slot_wait(xbuf, slot, sem)

    @pl.when(i < nvalid)
    def _():
        base = (i + 1) * MOE_TM
        for r in range(MOE_TM):
            _row_gather_start(x_hbm, src_ref[base + r], xbuf.at[1 - slot], r, sem.at[1 - slot])
        xb = xbuf[slot].astype(BF16)
        gate = _dot(xb, wg_ref[0, 0])
        up = _dot(xb, wu_ref[0, 0])
        h_ref[...] = (_silu(gate) * up).astype(BF16)

    @pl.when(i >= nvalid)
    def _():
        h_ref[...] = jnp.zeros_like(h_ref)


def moe_up(tile_expert, src_rows, nvalid, x, wg, wu, layer):
    d = x.shape[1]
    assert (2 * x.shape[0] + N_EXPERTS * (MOE_TM - 1)) // MOE_TM < MOE_TILES
    wspec = pl.BlockSpec((1, 1, d, D_EXPERT), lambda i, te, src, nv: (layer, te[i], 0, 0))
    grid_spec = pltpu.PrefetchScalarGridSpec(
        num_scalar_prefetch=3,
        grid=(MOE_TILES,),
        in_specs=[pl.BlockSpec(memory_space=pl.ANY), wspec, wspec],
        out_specs=pl.BlockSpec((MOE_TM, D_EXPERT), lambda i, te, src, nv: (i, 0)),
        scratch_shapes=[pltpu.VMEM((2, MOE_TM, d), F32), pltpu.SemaphoreType.DMA((2,))],
    )
    return pl.pallas_call(
        _moe_up_kernel,
        out_shape=jax.ShapeDtypeStruct((MOE_ROWS, D_EXPERT), BF16),
        grid_spec=grid_spec,
        compiler_params=_cparams(("arbitrary",)),
        name="moe_up",
    )(tile_expert, src_rows, nvalid, x, wg, wu)


def _moe_down_kernel(te_ref, nv_ref, h_ref, wd_ref, y_ref):
    del te_ref
    i = pl.program_id(0)

    @pl.when(i < nv_ref[0])
    def _():
        y_ref[...] = _dot(h_ref[...], wd_ref[0, 0])

    @pl.when(i >= nv_ref[0])
    def _():
        y_ref[...] = jnp.zeros_like(y_ref)


def moe_down(tile_expert, nvalid, h, wd, layer):
    d = wd.shape[3]
    grid_spec = pltpu.PrefetchScalarGridSpec(
        num_scalar_prefetch=2,
        grid=(MOE_TILES,),
        in_specs=[pl.BlockSpec((MOE_TM, D_EXPERT), lambda i, te, nv: (i, 0)),
                  pl.BlockSpec((1, 1, D_EXPERT, d), lambda i, te, nv: (layer, te[i], 0, 0))],
        out_specs=pl.BlockSpec((MOE_TM, d), lambda i, te, nv: (i, 0)),
    )
    return pl.pallas_call(
        _moe_down_kernel,
        out_shape=jax.ShapeDtypeStruct((MOE_ROWS, d), F32),
        grid_spec=grid_spec,
        compiler_params=_cparams(("arbitrary",)),
        name="moe_down",
    )(tile_expert, nvalid, h, wd)


def _combine_ln_kernel(pos_ref, x_ref, gsel_ref, y_hbm, g_ref, b_ref, *rest, tm, tile0, ntiles, with_router):
    if with_router:
        rw_ref, rb_ref, xo_ref, xb_ref, idx_ref, gate_ref, ybuf, sem = rest
    else:
        xo_ref, ybuf, sem = rest
    i = pl.program_id(0)
    slot = lax.rem(i, 2)

    @pl.when(i == 0)
    def _():
        def body(r, carry):
            _row_gather_start(y_hbm, pos_ref[2 * tile0 * tm + 2 * r], ybuf.at[0, 0], r, sem.at[0])
            _row_gather_start(y_hbm, pos_ref[2 * tile0 * tm + 2 * r + 1], ybuf.at[0, 1], r, sem.at[0])
            return carry
        lax.fori_loop(0, tm, body, 0)

    _slot_wait(ybuf, slot, sem)
    base = 2 * (tile0 + jnp.minimum(i + 1, ntiles - 1)) * tm
    for r in range(tm):
        _row_gather_start(y_hbm, pos_ref[base + 2 * r], ybuf.at[1 - slot, 0], r, sem.at[1 - slot])
        _row_gather_start(y_hbm, pos_ref[base + 2 * r + 1], ybuf.at[1 - slot, 1], r, sem.at[1 - slot])
    gsel = gsel_ref[...]
    ffn = gsel[:, 0:1] * ybuf[slot, 0] + gsel[:, 1:2] * ybuf[slot, 1]
    y = _layer_norm(DEEPNORM_ALPHA * x_ref[...] + ffn, g_ref[...], b_ref[...])
    xo_ref[...] = y
    if with_router:
        yb = y.astype(BF16)
        xb_ref[...] = yb
        idx, gate = _route_rows(yb, rw_ref[...], rb_ref[...])
        idx_ref[...] = idx
        gate_ref[...] = gate

    @pl.when(i == ntiles - 1)
    def _():
        _slot_wait(ybuf, 1 - slot, sem)


def combine_ln(pos, x, gate_sel, y_sorted, g, b, *, row0=0, nrows=None, router=None):
    d = x.shape[1]
    tm = LN_TM
    nrows = x.shape[0] if nrows is None else nrows
    tile0 = row0 // tm
    ntiles = nrows // tm
    row_in = pl.BlockSpec((tm, d), lambda i, p: (tile0 + i, 0))
    row_out = pl.BlockSpec((tm, d), lambda i, p: (i, 0))
    vec = pl.BlockSpec((1, d), lambda i, p: (0, 0))
    small = pl.BlockSpec((tm, LANES), lambda i, p: (i, 0))
    in_specs = [row_in, pl.BlockSpec((tm, LANES), lambda i, p: (tile0 + i, 0)),
                pl.BlockSpec(memory_space=pl.ANY), vec, vec]
    args = [pos, x, gate_sel, y_sorted, g, b]
    out_shape = [jax.ShapeDtypeStruct((nrows, d), F32)]
    out_specs = [row_out]
    if router is not None:
        in_specs += [pl.BlockSpec((d, LANES), lambda i, p: (0, 0)), pl.BlockSpec((1, LANES), lambda i, p: (0, 0))]
        args += list(router)
        out_shape += [jax.ShapeDtypeStruct((nrows, d), BF16), jax.ShapeDtypeStruct((nrows, LANES), I32),
                      jax.ShapeDtypeStruct((nrows, LANES), F32)]
        out_specs += [row_out, small, small]
    grid_spec = pltpu.PrefetchScalarGridSpec(
        num_scalar_prefetch=1,
        grid=(ntiles,),
        in_specs=in_specs,
        out_specs=tuple(out_specs),
        scratch_shapes=[pltpu.VMEM((2, 2, tm, d), F32), pltpu.SemaphoreType.DMA((2,))],
    )
    kern = functools.partial(_combine_ln_kernel, tm=tm, tile0=tile0, ntiles=ntiles,
                             with_router=router is not None)
    return pl.pallas_call(
        kern,
        out_shape=tuple(out_shape),
        grid_spec=grid_spec,
        compiler_params=_cparams(("arbitrary",)),
        name="combine_ln",
    )(*args)


def _route_tables(idx):
    t = idx.shape[0]
    e_flat = idx[:, :2].reshape(-1)
    onehot = (e_flat[:, None] == jnp.arange(N_EXPERTS, dtype=I32)[None, :]).astype(I32)
    csum = jnp.cumsum(onehot, axis=0)
    rank = jnp.take_along_axis(csum, e_flat[:, None], axis=1)[:, 0] - 1
    counts = csum[-1]
    padded = ((counts + MOE_TM - 1) // MOE_TM) * MOE_TM
    ends = jnp.cumsum(padded)
    starts = ends - padded
    pos = (starts[e_flat] + rank).astype(I32)
    src = jnp.zeros((MOE_ROWS,), I32).at[pos].set(jnp.arange(2 * t, dtype=I32) // 2)
    nvalid = (ends[-1] // MOE_TM).astype(I32)
    tile_start = jnp.arange(MOE_TILES, dtype=I32) * MOE_TM
    te = jnp.sum((tile_start[:, None] >= ends[None, :]).astype(I32), axis=1)
    te = jnp.minimum(te, te[nvalid - 1]).astype(I32)
    return pos, src, te, nvalid.reshape(1)


def moe_layer(x_f32, idx, wg, wu, wd, layer):
    pos, src, te, nvalid = _route_tables(idx)
    h = moe_up(te, src, nvalid, x_f32, wg, wu, layer)
    y_sorted = moe_down(te, nvalid, h, wd, layer)
    return pos, y_sorted


def _conv_kernel(x_ref, w_ref, b_ref, prev_ref, o_ref, halo):
    r = pl.program_id(2)

    @pl.when(r == 0)
    def _():
        halo[...] = prev_ref[0]

    x = x_ref[...]
    prev8 = halo[...]
    row8 = lax.broadcasted_iota(I32, prev8.shape, 0)
    acc = jnp.broadcast_to(b_ref[...], x.shape)
    for k in range(SSD_CONV_W):
        s = SSD_CONV_W - 1 - k
        if s == 0:
            xs = x
        else:
            xs = pltpu.roll(x, s, axis=0)
            top = jnp.where(row8 < s, pltpu.roll(prev8, s, axis=0), xs[:SUBLANES])
            xs = jnp.concatenate([top, xs[SUBLANES:]], axis=0)
        acc = acc + xs * w_ref[k:k + 1, :]
    halo[...] = x[x.shape[0] - SUBLANES:, :]
    o_ref[...] = _silu(acc)


def conv_silu(xbc, w, b, prev8, *, row0, nseq, seqlen, tr, out_alias=None):
    t, c = xbc.shape
    tc = 1024
    nrt = seqlen // tr
    rb0 = row0 // tr
    in_specs = [pl.BlockSpec((tr, tc), lambda j, s, r: (rb0 + s * nrt + r, j)),
                pl.BlockSpec((SSD_CONV_W, tc), lambda j, s, r: (0, j)),
                pl.BlockSpec((1, tc), lambda j, s, r: (0, j)),
                pl.BlockSpec((1, SUBLANES, tc), lambda j, s, r: (s, 0, j))]
    args = [xbc, w, b, prev8]
    aliases = {}
    kern = _conv_kernel
    if out_alias is not None:
        in_specs.append(pl.BlockSpec(memory_space=pl.ANY))
        args.append(out_alias)
        aliases = {4: 0}
        kern = lambda x, w_, b_, p, _a, o, h: _conv_kernel(x, w_, b_, p, o, h)
    return pl.pallas_call(
        kern,
        out_shape=jax.ShapeDtypeStruct((t, c), F32),
        grid=(c // tc, nseq, nrt),
        in_specs=in_specs,
        out_specs=pl.BlockSpec((tr, tc), lambda j, s, r: (rb0 + s * nrt + r, j)),
        scratch_shapes=[pltpu.VMEM((SUBLANES, tc), F32)],
        input_output_aliases=aliases,
        compiler_params=_cparams(("parallel", "parallel", "arbitrary")),
        name="conv_silu",
    )(*args)


def _dt_kernel(dt_ref, bias_ref, alog_ref, dt_o, dta_o):
    v = dt_ref[...] + bias_ref[...]
    sp = jnp.maximum(v, 0.0) + jnp.log1p(jnp.exp(-jnp.abs(v)))
    dt_o[...] = sp
    dta_o[...] = sp * (-jnp.exp(alog_ref[...]))


def dt_prep(dt_raw, bias, a_log):
    t, h = dt_raw.shape
    tr = t // 8
    row = pl.BlockSpec((tr, h), lambda i: (i, 0))
    vec = pl.BlockSpec((1, h), lambda i: (0, 0))
    return pl.pallas_call(
        _dt_kernel,
        out_shape=(jax.ShapeDtypeStruct((t, h), F32), jax.ShapeDtypeStruct((t, h), F32)),
        grid=(8,),
        in_specs=[row, vec, vec],
        out_specs=(row, row),
        compiler_params=_cparams(("parallel",)),
        name="dt_prep",
    )(dt_raw, bias, a_log)


def _ssd_kernel(x_ref, b_ref, c_ref, z_ref, dt_ref, dta_ref, dtt_ref, dtat_ref, h0_ref, dskip_ref, nw_ref,
                y_ref, hout_ref, state, *, q, nchunks):
    c = pl.program_id(2)
    gc = SSD_GROUP_CH
    n = SSD_D_STATE

    @pl.when(c == 0)
    def _():
        for k in range(SSD_GROUPS_PER_STEP):
            state[k] = h0_ref[0, k].T

    new_states = []
    for k in range(SSD_GROUPS_PER_STEP):
        cols = slice(k * gc, (k + 1) * gc)
        y, s_new = _ssd_group(
            x_ref[:, cols], b_ref[:, k * n:(k + 1) * n], c_ref[:, k * n:(k + 1) * n], z_ref[:, cols],
            dt_ref[k, 0], dta_ref[k, 0], dtt_ref[k, 0], dtat_ref[k, 0], state[k],
            dskip_ref[:, cols], nw_ref[:, cols], q=q)
        y_ref[:, cols] = y
        state[k] = s_new
        new_states.append(s_new)

    @pl.when(c == nchunks - 1)
    def _():
        for k in range(SSD_GROUPS_PER_STEP):
            hout_ref[0, k] = new_states[k].T


def _ssd_group(x, bm, cm, z, dt, dta, dtt, dtat, s_old, dskip, nw, *, q):
    hg = SSD_HEADS_PER_GROUP
    p = SSD_HEAD_DIM
    xb = x.astype(BF16)
    bm = bm.astype(BF16)
    cm = cm.astype(BF16)

    ri = lax.broadcasted_iota(I32, (q, q), 0)
    ci = lax.broadcasted_iota(I32, (q, q), 1)
    causal = ri >= ci
    lower = jnp.where(causal, 1.0, 0.0).astype(BF16)
    upper = jnp.where(ri <= ci, 1.0, 0.0).astype(BF16)
    acum = _dot_sel(lower, dta)
    acumt = _dot_sel_r(dtat, upper)
    alast = acum[q - 1:q, :]

    cb = _dot_nt(cm, bm)
    lane = lax.broadcasted_iota(I32, (q, 2 * p), 1)
    y_pairs = []
    for hp in range(hg // 2):
        ms = []
        for h in (2 * hp, 2 * hp + 1):
            seg = acum[:, h:h + 1] - acumt[h:h + 1, :]
            decay = jnp.exp(jnp.where(causal, seg, -jnp.inf))
            ms.append((cb * decay * dtt[h:h + 1, :]).astype(BF16))
        lhs = jnp.concatenate(ms, axis=1)
        xp = xb[:, 2 * p * hp:2 * p * (hp + 1)]
        zero = jnp.zeros_like(xp)
        rhs = jnp.concatenate([jnp.where(lane < p, xp, zero), jnp.where(lane >= p, xp, zero)], axis=0)
        y_pairs.append(_dot(lhs, rhs))
    y = jnp.concatenate(y_pairs, axis=1)

    hi = lax.broadcasted_iota(I32, (hg, hg * p), 0)
    li = lax.broadcasted_iota(I32, (hg, hg * p), 1)
    widen = jnp.where((li >= hi * p) & (li < (hi + 1) * p), 1.0, 0.0).astype(BF16)
    scales = jnp.concatenate([jnp.exp(acum), jnp.exp(alast - acum) * dt,
                              jnp.broadcast_to(jnp.exp(alast), (SUBLANES, hg))], axis=0)
    wide = _dot_sel_r(scales, widen)
    e_in = wide[:q]
    e_out = wide[q:2 * q]
    e_all = wide[2 * q:2 * q + 1]

    y = y + _dot(cm, s_old.astype(BF16)) * e_in
    xw = (x * e_out).astype(BF16)
    s_new = s_old * e_all + _dot_tn(bm, xw)

    y = y + dskip * x
    y = y * _silu(z)
    y = y * lax.rsqrt(jnp.mean(y * y, axis=-1, keepdims=True) + RMS_EPS)
    return (y * nw).astype(BF16), s_new


def ssd_scan(xbc_act, z, dt4, dta4, dtt4, dtat4, h0, dskip, norm_w, *, row0, nseq, seqlen, q, out_alias=None):
    t = xbc_act.shape[0]
    gps = SSD_GROUPS_PER_STEP
    g = SSD_N_GROUPS
    gc = gps * SSD_GROUP_CH
    n = SSD_D_STATE
    nchunks = seqlen // q
    rb0 = row0 // q
    bcol0 = SSD_D_INNER // (gps * n)
    ccol0 = (SSD_D_INNER + SSD_BC_DIM) // (gps * n)

    def rows(b, gg, c):
        return rb0 + b * nchunks + c

    in_specs = [pl.BlockSpec((q, gc), lambda b, gg, c: (rows(b, gg, c), gg)),
                pl.BlockSpec((q, gps * n), lambda b, gg, c: (rows(b, gg, c), bcol0 + gg)),
                pl.BlockSpec((q, gps * n), lambda b, gg, c: (rows(b, gg, c), ccol0 + gg)),
                pl.BlockSpec((q, gc), lambda b, gg, c: (rows(b, gg, c), gg)),
                pl.BlockSpec((gps, 1, q, 16), lambda b, gg, c: (gg, b * nchunks + c, 0, 0)),
                pl.BlockSpec((gps, 1, q, 16), lambda b, gg, c: (gg, b * nchunks + c, 0, 0)),
                pl.BlockSpec((gps, 1, 16, q), lambda b, gg, c: (gg, b * nchunks + c, 0, 0)),
                pl.BlockSpec((gps, 1, 16, q), lambda b, gg, c: (gg, b * nchunks + c, 0, 0)),
                pl.BlockSpec((1, gps, SSD_GROUP_CH, n), lambda b, gg, c: (b, gg, 0, 0)),
                pl.BlockSpec((1, gc), lambda b, gg, c: (0, gg)),
                pl.BlockSpec((1, gc), lambda b, gg, c: (0, gg))]
    args = [xbc_act, xbc_act, xbc_act, z, dt4, dta4, dtt4, dtat4, h0, dskip, norm_w]
    kern = functools.partial(_ssd_kernel, q=q, nchunks=nchunks)
    aliases = {}
    if out_alias is not None:
        in_specs.append(pl.BlockSpec(memory_space=pl.ANY))
        args.append(out_alias)
        aliases = {len(args) - 1: 0}
        base = kern
        kern = lambda *refs: base(*refs[:11], *refs[12:])
    return pl.pallas_call(
        kern,
        out_shape=(jax.ShapeDtypeStruct((t, SSD_D_INNER), BF16),
                   jax.ShapeDtypeStruct((nseq, g, SSD_GROUP_CH, n), F32)),
        grid=(nseq, g // gps, nchunks),
        in_specs=in_specs,
        out_specs=(pl.BlockSpec((q, gc), lambda b, gg, c: (rows(b, gg, c), gg)),
                   pl.BlockSpec((1, gps, SSD_GROUP_CH, n), lambda b, gg, c: (b, gg, 0, 0))),
        scratch_shapes=[pltpu.VMEM((gps, n, SSD_GROUP_CH), F32)],
        input_output_aliases=aliases,
        compiler_params=_cparams(("parallel", "parallel", "arbitrary")),
        name="ssd_scan",
    )(*args)


def _dt_layouts(v, row0, nseq, seqlen, q):
    nb = nseq * seqlen // q
    part = v[row0:row0 + nseq * seqlen].reshape(nb, q, SSD_N_GROUPS, SSD_HEADS_PER_GROUP)
    return part.transpose(2, 0, 1, 3), part.transpose(2, 0, 3, 1)


def _s5_prep_kernel(lr_ref, li_ref, ls_ref, bre_ref, bim_ref, cre_ref, cim_ref, wst_ref, vt_ref, krev_ref, aq_ref):
    lr = lr_ref[0]
    li = li_ref[0]
    dstep = jnp.exp(ls_ref[0])
    mag = jnp.exp(lr * dstep)
    ar = mag * jnp.cos(li * dstep)
    ai = mag * jnp.sin(li * dstep)
    den = lr * lr + li * li
    cr = ((ar - 1.0) * lr + ai * li) / den
    ci = (ai * lr - (ar - 1.0) * li) / den
    bre = bre_ref[0]
    bim = bim_ref[0]
    bbr = cr * bre - ci * bim
    bbi = cr * bim + ci * bre
    bmat = jnp.concatenate([bbr, bbi], axis=1)
    cre = cre_ref[0]
    cim = cim_ref[0]

    def power(m):
        e = jnp.exp((m * lr) * dstep)
        return e * jnp.cos((m * li) * dstep), e * jnp.sin((m * li) * dstep)

    powers = [power(float(m)) for m in range(S5_Q + 1)]

    def out_rows(m):
        pr, pi = powers[m]
        return jnp.concatenate([cre * pr - cim * pi, -(cre * pi + cim * pr)], axis=1)

    for t in range(S5_Q):
        pr, pi = powers[S5_Q - 1 - t]
        blk = jnp.concatenate([pr * bbr - pi * bbi, pr * bbi + pi * bbr], axis=1)
        wst_ref[0, t * LANES:(t + 1) * LANES, :] = blk.astype(BF16)
        vt_ref[0, t * LANES:(t + 1) * LANES, :] = out_rows(t + 1).astype(BF16)
        lag = S5_Q - 1 - t
        kk = lax.dot_general(bmat, out_rows(lag), (((1,), (1,)), ((), ())),
                             precision=lax.Precision.HIGHEST, preferred_element_type=F32)
        krev_ref[0, t * LANES:(t + 1) * LANES, :] = kk.astype(BF16)
    pr, pi = powers[S5_Q]
    aq_ref[0] = jnp.concatenate([pr, pi], axis=1)


def s5_prep(lr, li, ls, bre, bim, cre, cim):
    np_ = S5_N_PACKS
    sp = S5_PACK_STATE
    vec = pl.BlockSpec((1, 1, sp), lambda i: (i, 0, 0))
    mat = pl.BlockSpec((1, LANES, sp), lambda i: (i, 0, 0))
    big = pl.BlockSpec((1, S5_Q * LANES, 2 * sp), lambda i: (i, 0, 0))
    return pl.pallas_call(
        _s5_prep_kernel,
        out_shape=(jax.ShapeDtypeStruct((np_, S5_Q * LANES, 2 * sp), BF16),
                   jax.ShapeDtypeStruct((np_, S5_Q * LANES, 2 * sp), BF16),
                   jax.ShapeDtypeStruct((np_, S5_Q * LANES, LANES), BF16),
                   jax.ShapeDtypeStruct((np_, 1, 2 * sp), F32)),
        grid=(np_,),
        in_specs=[vec, vec, vec, mat, mat, mat, mat],
        out_specs=(big, big, pl.BlockSpec((1, S5_Q * LANES, LANES), lambda i: (i, 0, 0)),
                   pl.BlockSpec((1, 1, 2 * sp), lambda i: (i, 0, 0))),
        compiler_params=_cparams(("parallel",)),
        name="s5_prep",
    )(lr, li, ls, bre, bim, cre, cim)


def _s5_kernel(x_ref, wst_ref, vt_ref, krev_ref, aq_ref, d_ref, h0_ref, g_ref, hout_ref, *scratch,
               nblk, nseq, chain):
    q = S5_Q
    sp = S5_PACK_STATE
    us = [x_ref[pl.ds(t, nblk, stride=q), :] for t in range(q)]
    ucat = jnp.concatenate([u.astype(BF16) for u in us], axis=1)
    s_in = _dot(ucat, wst_ref[0])
    aq = aq_ref[0]
    aqr, aqi = aq[:, :sp], aq[:, sp:]

    def advance(h, s):
        hr, hi = h[:, :sp], h[:, sp:]
        return jnp.concatenate([aqr * hr - aqi * hi + s[:, :sp], aqr * hi + aqi * hr + s[:, sp:]], axis=1)

    if chain:
        hin_s, s_s, y_s = scratch
        per_seq = nblk // nseq
        s_s[...] = s_in

        def body(c, hs):
            out = []
            for s in range(nseq):
                row = s * per_seq + c
                hin_s[pl.ds(row, 1), :] = hs[s]
                out.append(advance(hs[s], s_s[pl.ds(row, 1), :]))
            return tuple(out)

        h0 = h0_ref[0, 0]
        hs = lax.fori_loop(0, per_seq, body, tuple(h0[s:s + 1] for s in range(nseq)))
        hout_ref[0, 0] = jnp.concatenate(hs, axis=0)
        hin = hin_s[...]
    else:
        (y_s,) = scratch
        hin = h0_ref[0, 0]
        hout_ref[0, 0] = advance(hin, s_in)
    hinb = hin.astype(BF16)
    d = d_ref[...]
    for t in range(q):
        y = _dot_nt(hinb, vt_ref[0, t * LANES:(t + 1) * LANES, :])
        y = y + _dot(ucat[:, :(t + 1) * LANES], krev_ref[0, (q - 1 - t) * LANES:, :])
        y = y + d * us[t]
        y_s[pl.ds(t, nblk, stride=q), :] = _gelu_exact(y)
    g_ref[...] = y_s[...].astype(BF16)


def s5_scan(x, wst, vt, krev, aq, dskip, h0, *, row0, nseq, seqlen, chain, out_alias=None):
    t, d = x.shape
    q = S5_Q
    sp2 = 2 * S5_PACK_STATE
    spb = S5_CHAIN_SEQS if chain else nseq
    nstep = nseq // spb
    rows = spb * seqlen
    nblk = rows // q
    assert chain or seqlen == q
    grid = (S5_N_PACKS, nstep)
    rb0 = row0 // rows
    xmap = lambda p, b: (rb0 + b, p)
    wmap = lambda p, b: (p, 0, 0)
    dmap = lambda p, b: (0, p)
    hspec = pl.BlockSpec((1, 1, spb, sp2), lambda p, b: (p, b, 0, 0))
    hshape = (S5_N_PACKS, nstep, spb, sp2)
    scratch = [pltpu.VMEM((rows, LANES), F32)]
    if chain:
        scratch = [pltpu.VMEM((nblk, sp2), F32), pltpu.VMEM((nblk, sp2), F32)] + scratch
    sem = ("parallel", "parallel")
    in_specs = [pl.BlockSpec((rows, LANES), xmap),
                pl.BlockSpec((1, q * LANES, sp2), wmap),
                pl.BlockSpec((1, q * LANES, sp2), wmap),
                pl.BlockSpec((1, q * LANES, LANES), wmap),
                pl.BlockSpec((1, 1, sp2), wmap),
                pl.BlockSpec((1, LANES), dmap),
                hspec]
    args = [x, wst, vt, krev, aq, dskip, h0]
    kern = functools.partial(_s5_kernel, nblk=nblk, nseq=spb, chain=chain)
    aliases = {}
    if out_alias is not None:
        in_specs.append(pl.BlockSpec(memory_space=pl.ANY))
        args.append(out_alias)
        aliases = {len(args) - 1: 0}
        base = kern
        kern = lambda *refs: base(*refs[:7], *refs[8:])
    return pl.pallas_call(
        kern,
        out_shape=(jax.ShapeDtypeStruct((t, d), BF16), jax.ShapeDtypeStruct(hshape, F32)),
        grid=grid,
        in_specs=in_specs,
        out_specs=(pl.BlockSpec((rows, LANES), xmap), hspec),
        scratch_shapes=scratch,
        input_output_aliases=aliases,
        compiler_params=_cparams(sem),
        name="s5_scan",
    )(*args)


def _s5_block_diag(m):
    m = m.reshape(S5_N_PACKS, S5_PACK_GROUPS, S5_GROUP_CH, S5_STATE)
    eye = jnp.eye(S5_PACK_GROUPS, dtype=m.dtype)
    out = m[:, :, :, None, :] * eye[None, :, None, :, None]
    return out.reshape(S5_N_PACKS, S5_PACK_GROUPS * S5_GROUP_CH, S5_PACK_STATE)


def _s5_pack_state(re, im):
    b = re.shape[0]
    return jnp.concatenate([re.reshape(b, S5_N_PACKS, S5_PACK_STATE), im.reshape(b, S5_N_PACKS, S5_PACK_STATE)], axis=-1)


def _s5_unpack_state(h):
    b = h.shape[0]
    return (h[..., :S5_PACK_STATE].reshape(b, S5_N_GROUPS, S5_STATE),
            h[..., S5_PACK_STATE:].reshape(b, S5_N_GROUPS, S5_STATE))


def kernel(x_prompt, x_sample, cache_ssd_conv, state_ssd, state_s5_re, state_s5_im, ssd_w_in, ssd_conv_w, ssd_conv_b, ssd_dt_bias, ssd_a_log, ssd_d, ssd_norm_w, ssd_w_out, s5_lambda_re, s5_lambda_im, s5_log_step, s5_b_re, s5_b_im, s5_c_re, s5_c_im, s5_d, s5_w_glu, ln_mix_g, ln_mix_b, ln_ffn_g, ln_ffn_b, router_w, router_b, moe_w_gate, moe_w_up, moe_w_down):
    d = D_MODEL
    x0 = jnp.concatenate([x_prompt.reshape(T_PROMPT, d), x_sample.reshape(T_SAMPLE, d)], axis=0)
    x0b = x0.astype(BF16)
    rw = jnp.zeros((d, LANES), BF16).at[:, :N_EXPERTS].set(router_w.astype(BF16))
    rb = jnp.zeros((1, LANES), F32).at[0, :N_EXPERTS].set(router_b)
    router = (rw, rb)

    w_in = ssd_w_in[0].astype(BF16)
    z = matmul(x0b, w_in, tm=MM_TM, tn=1024, col0=0, ncols=SSD_D_INNER)
    xbc = matmul(x0b, w_in, tm=MM_TM, tn=1024, col0=SSD_D_INNER, ncols=SSD_CONV_DIM)
    dt_raw = matmul(x0b, w_in, tm=MM_TM, tn=LANES, col0=SSD_D_INNER + SSD_CONV_DIM, ncols=SSD_N_HEADS)

    conv_w = ssd_conv_w[0]
    conv_b = ssd_conv_b[0].reshape(1, SSD_CONV_DIM)
    prev_p = jnp.zeros((BATCH, SUBLANES, SSD_CONV_DIM), F32)
    prev_s = jnp.concatenate([jnp.zeros((DEC_BATCH, SUBLANES - (SSD_CONV_W - 1), SSD_CONV_DIM), F32),
                              cache_ssd_conv[0]], axis=1)
    xact = conv_silu(xbc, conv_w, conv_b, prev_p, row0=0, nseq=BATCH, seqlen=SEQ, tr=512)
    xact = conv_silu(xbc, conv_w, conv_b, prev_s, row0=T_PROMPT, nseq=DEC_BATCH, seqlen=DEC_SEQ, tr=DEC_SEQ,
                     out_alias=xact)

    dt_sp, dta = dt_prep(dt_raw, ssd_dt_bias[0].reshape(1, SSD_N_HEADS), ssd_a_log[0].reshape(1, SSD_N_HEADS))
    dskip = jnp.repeat(ssd_d[0], SSD_HEAD_DIM).reshape(1, SSD_D_INNER)
    norm_w = ssd_norm_w[0].reshape(1, SSD_D_INNER)
    h0_p = jnp.zeros((BATCH, SSD_N_GROUPS, SSD_GROUP_CH, SSD_D_STATE), F32)
    h0_s = state_ssd[0].reshape(DEC_BATCH, SSD_N_GROUPS, SSD_GROUP_CH, SSD_D_STATE)
    lay_p = _dt_layouts(dt_sp, 0, BATCH, SEQ, SSD_Q_PROMPT) + _dt_layouts(dta, 0, BATCH, SEQ, SSD_Q_PROMPT)
    lay_s = (_dt_layouts(dt_sp, T_PROMPT, DEC_BATCH, DEC_SEQ, DEC_SEQ)
             + _dt_layouts(dta, T_PROMPT, DEC_BATCH, DEC_SEQ, DEC_SEQ))
    ymix, ssd_p = ssd_scan(xact, z, lay_p[0], lay_p[2], lay_p[1], lay_p[3], h0_p, dskip, norm_w,
                           row0=0, nseq=BATCH, seqlen=SEQ, q=SSD_Q_PROMPT)
    ymix, ssd_s = ssd_scan(xact, z, lay_s[0], lay_s[2], lay_s[1], lay_s[3], h0_s, dskip, norm_w,
                           row0=T_PROMPT, nseq=DEC_BATCH, seqlen=DEC_SEQ, q=DEC_SEQ, out_alias=ymix)
    mix = matmul(ymix, ssd_w_out[0].astype(BF16), tm=MM_TM, tn=512)

    wg = moe_w_gate.astype(BF16)
    wu = moe_w_up.astype(BF16)
    wd = moe_w_down.astype(BF16)
    x1, x1b, idx, gate = ln_router(x0, mix, ln_mix_g[0].reshape(1, d), ln_mix_b[0].reshape(1, d), rw, rb)
    del x1b
    pos, y_sorted = moe_layer(x1, idx, wg, wu, wd, 0)
    (x2,) = combine_ln(pos, x1, gate, y_sorted, ln_ffn_g[0].reshape(1, d), ln_ffn_b[0].reshape(1, d))

    lr = s5_lambda_re[0].reshape(S5_N_PACKS, 1, S5_PACK_STATE)
    li = s5_lambda_im[0].reshape(S5_N_PACKS, 1, S5_PACK_STATE)
    ls = jnp.repeat(s5_log_step[0], S5_STATE).reshape(S5_N_PACKS, 1, S5_PACK_STATE)
    wst, vt, krev, aq = s5_prep(lr, li, ls,
                                _s5_block_diag(s5_b_re[0].transpose(0, 2, 1)),
                                _s5_block_diag(s5_b_im[0].transpose(0, 2, 1)),
                                _s5_block_diag(s5_c_re[0]), _s5_block_diag(s5_c_im[0]))
    s5_dskip = s5_d[0].reshape(1, d)
    sp2 = 2 * S5_PACK_STATE
    hs_p = jnp.zeros((S5_N_PACKS, BATCH // S5_CHAIN_SEQS, S5_CHAIN_SEQS, sp2), F32)
    hs_s = _s5_pack_state(state_s5_re[0], state_s5_im[0]).transpose(1, 0, 2)[:, None]
    gact, s5_p = s5_scan(x2, wst, vt, krev, aq, s5_dskip, hs_p, row0=0, nseq=BATCH, seqlen=SEQ, chain=True)
    gact, s5_s = s5_scan(x2, wst, vt, krev, aq, s5_dskip, hs_s, row0=T_PROMPT, nseq=DEC_BATCH, seqlen=DEC_SEQ,
                         chain=False, out_alias=gact)
    mix = glu_matmul(gact, s5_w_glu[0].astype(BF16), tm=MM_TM, tn=512)

    x3, x3b, idx, gate = ln_router(x2, mix, ln_mix_g[1].reshape(1, d), ln_mix_b[1].reshape(1, d), rw, rb)
    del x3b
    pos, y_sorted = moe_layer(x3, idx, wg, wu, wd, 1)
    g1 = ln_ffn_g[1].reshape(1, d)
    b1 = ln_ffn_b[1].reshape(1, d)
    (y_p,) = combine_ln(pos, x3, gate, y_sorted, g1, b1, row0=0, nrows=T_PROMPT)
    (y_s,) = combine_ln(pos, x3, gate, y_sorted, g1, b1, row0=T_PROMPT, nrows=T_SAMPLE)

    keep = SSD_CONV_W - 1
    conv_p = jnp.stack([xbc[(b + 1) * SEQ - keep:(b + 1) * SEQ] for b in range(BATCH)])[None]
    conv_s = xbc[T_PROMPT:].reshape(DEC_BATCH, DEC_SEQ, SSD_CONV_DIM)[:, DEC_SEQ - keep:][None]
    ssd_state_p = ssd_p.reshape(1, BATCH, SSD_N_HEADS, SSD_HEAD_DIM, SSD_D_STATE)
    ssd_state_s = ssd_s.reshape(1, DEC_BATCH, SSD_N_HEADS, SSD_HEAD_DIM, SSD_D_STATE)
    re_p, im_p = _s5_unpack_state(s5_p.reshape(S5_N_PACKS, BATCH, sp2).transpose(1, 0, 2))
    re_s, im_s = _s5_unpack_state(s5_s[:, 0].transpose(1, 0, 2))
    return (y_p.reshape(BATCH, SEQ, d), y_s.reshape(DEC_BATCH, DEC_SEQ, d),
            conv_p, ssd_state_p, re_p[None], im_p[None],
            conv_s, ssd_state_s, re_s[None], im_s[None])
```

```python
import functools
import math

import jax
import jax.numpy as jnp
from jax import lax
from jax.experimental import pallas as pl
from jax.experimental.pallas import tpu as pltpu

F32 = jnp.float32
BF16 = jnp.bfloat16
I32 = jnp.int32

D_MODEL = 4096
BATCH, SEQ = 4, 4096
DEC_BATCH, DEC_SEQ = 32, 16
T_PROMPT = BATCH * SEQ
T_SAMPLE = DEC_BATCH * DEC_SEQ
T_ALL = T_PROMPT + T_SAMPLE
SSD_D_INNER = 8192
SSD_HEAD_DIM = 64
SSD_N_HEADS = 128
SSD_N_GROUPS = 8
SSD_HEADS_PER_GROUP = 16
SSD_D_STATE = 128
SSD_GROUP_CH = SSD_D_INNER // SSD_N_GROUPS
SSD_BC_DIM = SSD_N_GROUPS * SSD_D_STATE
SSD_CONV_DIM = SSD_D_INNER + 2 * SSD_BC_DIM
SSD_CONV_W = 4
S5_GROUP_CH = 16
S5_N_GROUPS = 256
S5_STATE = 64
S5_PACK_GROUPS = 8
S5_N_PACKS = S5_N_GROUPS // S5_PACK_GROUPS
S5_PACK_STATE = S5_PACK_GROUPS * S5_STATE
S5_Q = 16
S5_CHAIN_SEQS = 2
N_EXPERTS = 16
EXPERTS_PER_GROUP = 4
D_EXPERT = 1024
DEPTH = 2
DEEPNORM_ALPHA = (2 * DEPTH) ** 0.25
LN_EPS = 1e-5
RMS_EPS = 1e-5

LANES = 128
SUBLANES = 8
VMEM_LIMIT_BYTES = 56 * 1024 * 1024

MM_TM = 768
MOE_TM = 256
MOE_ROWS = 2 * T_ALL + N_EXPERTS * MOE_TM
MOE_TILES = MOE_ROWS // MOE_TM
LN_TM = 128
SSD_Q_PROMPT = 128
SSD_GROUPS_PER_STEP = 2


def _cparams(semantics):
    return pltpu.CompilerParams(dimension_semantics=semantics, vmem_limit_bytes=VMEM_LIMIT_BYTES)


def _dot(a, b):
    return jnp.dot(a, b, preferred_element_type=F32)


def _dot_nt(a, b):
    return lax.dot_general(a, b, (((1,), (1,)), ((), ())), preferred_element_type=F32)


def _dot_tn(a, b):
    return lax.dot_general(a, b, (((0,), (0,)), ((), ())), preferred_element_type=F32)


def _split3(v):
    hi = v.astype(BF16)
    r = v - hi.astype(F32)
    mid = r.astype(BF16)
    lo = (r - mid.astype(F32)).astype(BF16)
    return hi, mid, lo


def _dot_sel(sel, v):
    hi, mid, lo = _split3(v)
    return _dot(sel, hi) + _dot(sel, mid) + _dot(sel, lo)


def _dot_sel_r(v, sel):
    hi, mid, lo = _split3(v)
    return _dot(hi, sel) + _dot(mid, sel) + _dot(lo, sel)


def _dot_sel_r2(v, sel):
    hi = v.astype(BF16)
    lo = (v - hi.astype(F32)).astype(BF16)
    return _dot(hi, sel) + _dot(lo, sel)


def _silu(x):
    return x * jax.nn.sigmoid(x)


def _gelu_exact(x):
    return 0.5 * x * (1.0 + lax.erf(x * (1.0 / math.sqrt(2.0))))


def _mm_kernel(x_ref, w_ref, o_ref):
    o_ref[...] = _dot(x_ref[...], w_ref[...]).astype(o_ref.dtype)


def matmul(x, w, *, tm, tn, col0=0, ncols=None, out_dtype=F32):
    m, k = x.shape
    n = w.shape[1] if ncols is None else ncols
    c0 = col0 // tn
    return pl.pallas_call(
        _mm_kernel,
        out_shape=jax.ShapeDtypeStruct((m, n), out_dtype),
        grid=(n // tn, m // tm),
        in_specs=[pl.BlockSpec((tm, k), lambda j, i: (i, 0)),
                  pl.BlockSpec((k, tn), lambda j, i: (0, c0 + j))],
        out_specs=pl.BlockSpec((tm, tn), lambda j, i: (i, j)),
        compiler_params=_cparams(("parallel", "parallel")),
        name="matmul",
    )(x, w)


def _glu_kernel(x_ref, w1_ref, w2_ref, o_ref):
    x = x_ref[...]
    z1 = _dot(x, w1_ref[...])
    z2 = _dot(x, w2_ref[...])
    o_ref[...] = z1 * jax.nn.sigmoid(z2)


def glu_matmul(x, w, *, tm, tn):
    m, k = x.shape
    n = w.shape[1] // 2
    half = n // tn
    return pl.pallas_call(
        _glu_kernel,
        out_shape=jax.ShapeDtypeStruct((m, n), F32),
        grid=(n // tn, m // tm),
        in_specs=[pl.BlockSpec((tm, k), lambda j, i: (i, 0)),
                  pl.BlockSpec((k, tn), lambda j, i: (0, j)),
                  pl.BlockSpec((k, tn), lambda j, i: (0, half + j))],
        out_specs=pl.BlockSpec((tm, tn), lambda j, i: (i, j)),
        compiler_params=_cparams(("parallel", "parallel")),
        name="glu_matmul",
    )(x, w, w)


def _layer_norm(v, g, b):
    mu = jnp.mean(v, axis=-1, keepdims=True)
    d = v - mu
    var = jnp.mean(d * d, axis=-1, keepdims=True)
    return d * lax.rsqrt(var + LN_EPS) * g + b


def _route_rows(xb, rw, rb):
    tm = xb.shape[0]
    lane = lax.broadcasted_iota(I32, (tm, LANES), 1)
    live = lane < N_EXPERTS
    logits = _dot(xb, rw) + rb
    logits = jnp.where(live, logits, -jnp.inf)
    m = jnp.max(logits, axis=-1, keepdims=True)
    e = jnp.exp(logits - m)
    probs = e / jnp.sum(e, axis=-1, keepdims=True)
    best = None
    for g in range(N_EXPERTS // EXPERTS_PER_GROUP):
        in_g = (lane >= g * EXPERTS_PER_GROUP) & (lane < (g + 1) * EXPERTS_PER_GROUP)
        score = jnp.max(jnp.where(in_g, probs, -1.0), axis=-1, keepdims=True)
        if best is None:
            best, best_g = score, jnp.zeros((tm, 1), I32)
        else:
            upd = score > best
            best = jnp.where(upd, score, best)
            best_g = jnp.where(upd, g, best_g)
    lo = best_g * EXPERTS_PER_GROUP
    in_best = (lane >= lo) & (lane < lo + EXPERTS_PER_GROUP)
    cand = jnp.where(in_best, probs, -1.0)
    p1 = jnp.max(cand, axis=-1, keepdims=True)
    i1 = jnp.min(jnp.where(cand == p1, lane, LANES), axis=-1, keepdims=True)
    cand2 = jnp.where(lane == i1, -2.0, cand)
    p2 = jnp.max(cand2, axis=-1, keepdims=True)
    i2 = jnp.min(jnp.where(cand2 == p2, lane, LANES), axis=-1, keepdims=True)
    tot = p1 + p2
    idx = jnp.where(lane == 0, i1, jnp.where(lane == 1, i2, 0))
    gate = jnp.where(lane == 0, p1 / tot, jnp.where(lane == 1, p2 / tot, 0.0))
    return idx, gate


def _row_part_specs(parts, tm, d):
    specs, firsts, t0 = [], [], 0
    for p in parts:
        nt = p.shape[0] // tm
        specs.append(pl.BlockSpec((tm, d), lambda i, *_, t0=t0, nt=nt: (jnp.clip(i - t0, 0, nt - 1), 0)))
        firsts.append(t0)
        t0 += nt
    return specs, firsts


def _select_part(i, refs, firsts, store):
    for k, ref in enumerate(refs):
        hi = firsts[k + 1] if k + 1 < len(refs) else None
        cond = i >= firsts[k] if hi is None else (i >= firsts[k]) & (i < hi)

        @pl.when(cond)
        def _(ref=ref):
            store(ref[...])


def _ln_router_kernel(*refs, firsts):
    n = len(firsts)
    x_refs = refs[:n]
    mix_ref, g_ref, b_ref, rw_ref, rb_ref, xo_ref, idx_ref, gate_ref = refs[n:]

    def finish(x):
        y = _layer_norm(DEEPNORM_ALPHA * x + mix_ref[...], g_ref[...], b_ref[...])
        xo_ref[...] = y
        idx, gate = _route_rows(y.astype(BF16), rw_ref[...], rb_ref[...])
        idx_ref[...] = idx
        gate_ref[...] = gate

    _select_part(pl.program_id(0), x_refs, firsts, finish)


def ln_router(x_parts, mix, g, b, rw, rb):
    t, d = mix.shape
    tm = LN_TM
    x_specs, firsts = _row_part_specs(x_parts, tm, d)
    row = pl.BlockSpec((tm, d), lambda i: (i, 0))
    vec = pl.BlockSpec((1, d), lambda i: (0, 0))
    small = pl.BlockSpec((tm, LANES), lambda i: (i, 0))
    return pl.pallas_call(
        functools.partial(_ln_router_kernel, firsts=tuple(firsts)),
        out_shape=(jax.ShapeDtypeStruct((t, d), F32),
                   jax.ShapeDtypeStruct((t, LANES), I32), jax.ShapeDtypeStruct((t, LANES), F32)),
        grid=(t // tm,),
        in_specs=x_specs + [row, vec, vec, pl.BlockSpec((d, LANES), lambda i: (0, 0)),
                            pl.BlockSpec((1, LANES), lambda i: (0, 0))],
        out_specs=(row, small, small),
        compiler_params=_cparams(("parallel",)),
        name="ln_router",
    )(*x_parts, mix, g, b, rw, rb)


def _cast_rows_kernel(*refs, firsts):
    o_ref = refs[-1]

    def store(v):
        o_ref[...] = v.astype(o_ref.dtype)

    _select_part(pl.program_id(0), refs[:-1], firsts, store)


def cast_rows(x_parts, dtype, tm):
    d = x_parts[0].shape[1]
    t = sum(p.shape[0] for p in x_parts)
    x_specs, firsts = _row_part_specs(x_parts, tm, d)
    return pl.pallas_call(
        functools.partial(_cast_rows_kernel, firsts=tuple(firsts)),
        out_shape=jax.ShapeDtypeStruct((t, d), dtype),
        grid=(t // tm,),
        in_specs=x_specs,
        out_specs=pl.BlockSpec((tm, d), lambda i: (i, 0)),
        compiler_params=_cparams(("parallel",)),
        name="cast_rows",
    )(*x_parts)


def _cast_kernel(x_ref, o_ref):
    o_ref[...] = x_ref[...].astype(o_ref.dtype)


def cast_weights(w, dtype, rows_per_block):
    nl, ne, r, c = w.shape
    spec = pl.BlockSpec((1, 1, rows_per_block, c), lambda l, e, j: (l, e, j, 0))
    return pl.pallas_call(
        _cast_kernel,
        out_shape=jax.ShapeDtypeStruct(w.shape, dtype),
        grid=(nl, ne, r // rows_per_block),
        in_specs=[spec],
        out_specs=spec,
        compiler_params=_cparams(("parallel", "parallel", "parallel")),
        name="cast_weights",
    )(w)


def _row_gather_start(src_hbm, row, dst, dst_row, sem, priority=0):
    pltpu.make_async_copy(src_hbm.at[pl.ds(row, 1)], dst.at[pl.ds(dst_row, 1)], sem).start(priority=priority)


def _slot_wait(buf, slot, sem):
    pltpu.make_async_copy(buf.at[slot], buf.at[slot], sem.at[slot]).wait()


def _moe_up_kernel(te_ref, src_ref, nv_ref, x_hbm, wg_ref, wu_ref, h_ref, xbuf, sem):
    del te_ref
    i = pl.program_id(0)
    slot = lax.rem(i, 2)
    nvalid = nv_ref[0]

    @pl.when(i == 0)
    def _():
        def body(r, carry):
            _row_gather_start(x_hbm, src_ref[r], xbuf.at[0], r, sem.at[0])
            return carry
        lax.fori_loop(0, MOE_TM, body, 0)

    @pl.when(i <= nvalid)
    def _():
        _slot_wait(xbuf, slot, sem)

    @pl.when(i < nvalid)
    def _():
        base = (i + 1) * MOE_TM
        for r in range(MOE_TM):
            _row_gather_start(x_hbm, src_ref[base + r], xbuf.at[1 - slot], r, sem.at[1 - slot], priority=r % 2)
        xb = xbuf[slot].astype(BF16)
        gate = _dot(xb, wg_ref[0, 0])
        up = _dot(xb, wu_ref[0, 0])
        h_ref[...] = (_silu(gate) * up).astype(BF16)

    @pl.when(i >= nvalid)
    def _():
        h_ref[...] = jnp.zeros_like(h_ref)


def moe_up(tile_expert, src_rows, nvalid, x, wg, wu, layer):
    d = x.shape[1]
    assert (2 * x.shape[0] + N_EXPERTS * (MOE_TM - 1)) // MOE_TM < MOE_TILES
    wspec = pl.BlockSpec((1, 1, d, D_EXPERT), lambda i, te, src, nv: (layer, te[i], 0, 0))
    grid_spec = pltpu.PrefetchScalarGridSpec(
        num_scalar_prefetch=3,
        grid=(MOE_TILES,),
        in_specs=[pl.BlockSpec(memory_space=pl.ANY), wspec, wspec],
        out_specs=pl.BlockSpec((MOE_TM, D_EXPERT), lambda i, te, src, nv: (i, 0)),
        scratch_shapes=[pltpu.VMEM((2, MOE_TM, d), F32), pltpu.SemaphoreType.DMA((2,))],
    )
    return pl.pallas_call(
        _moe_up_kernel,
        out_shape=jax.ShapeDtypeStruct((MOE_ROWS, D_EXPERT), BF16),
        grid_spec=grid_spec,
        compiler_params=_cparams(("arbitrary",)),
        name="moe_up",
    )(tile_expert, src_rows, nvalid, x, wg, wu)


def _moe_down_kernel(te_ref, nv_ref, h_ref, wd_ref, y_ref):
    del te_ref
    i = pl.program_id(0)

    @pl.when(i < nv_ref[0])
    def _():
        y_ref[...] = _dot(h_ref[...], wd_ref[0, 0])

    @pl.when(i >= nv_ref[0])
    def _():
        y_ref[...] = jnp.zeros_like(y_ref)


def moe_down(tile_expert, nvalid, h, wd, layer):
    d = wd.shape[3]
    grid_spec = pltpu.PrefetchScalarGridSpec(
        num_scalar_prefetch=2,
        grid=(MOE_TILES,),
        in_specs=[pl.BlockSpec((MOE_TM, D_EXPERT), lambda i, te, nv: (i, 0)),
                  pl.BlockSpec((1, 1, D_EXPERT, d), lambda i, te, nv: (layer, te[i], 0, 0))],
        out_specs=pl.BlockSpec((MOE_TM, d), lambda i, te, nv: (i, 0)),
    )
    return pl.pallas_call(
        _moe_down_kernel,
        out_shape=jax.ShapeDtypeStruct((MOE_ROWS, d), F32),
        grid_spec=grid_spec,
        compiler_params=_cparams(("arbitrary",)),
        name="moe_down",
    )(tile_expert, nvalid, h, wd)


def _combine_ln_kernel(pos_ref, x_ref, gsel_ref, y_hbm, g_ref, b_ref, *rest, tm, tile0, ntiles, with_router):
    if with_router:
        rw_ref, rb_ref, xo_ref, xb_ref, idx_ref, gate_ref, ybuf, sem = rest
    else:
        xo_ref, ybuf, sem = rest
    i = pl.program_id(0)
    slot = lax.rem(i, 2)

    @pl.when(i == 0)
    def _():
        def body(r, carry):
            _row_gather_start(y_hbm, pos_ref[2 * tile0 * tm + 2 * r], ybuf.at[0, 0], r, sem.at[0])
            _row_gather_start(y_hbm, pos_ref[2 * tile0 * tm + 2 * r + 1], ybuf.at[0, 1], r, sem.at[0])
            return carry
        lax.fori_loop(0, tm, body, 0)

    _slot_wait(ybuf, slot, sem)
    base = 2 * (tile0 + jnp.minimum(i + 1, ntiles - 1)) * tm
    for r in range(tm):
        _row_gather_start(y_hbm, pos_ref[base + 2 * r], ybuf.at[1 - slot, 0], r, sem.at[1 - slot], priority=0)
        _row_gather_start(y_hbm, pos_ref[base + 2 * r + 1], ybuf.at[1 - slot, 1], r, sem.at[1 - slot], priority=1)
    gsel = gsel_ref[...]
    ffn = gsel[:, 0:1] * ybuf[slot, 0] + gsel[:, 1:2] * ybuf[slot, 1]
    y = _layer_norm(DEEPNORM_ALPHA * x_ref[...] + ffn, g_ref[...], b_ref[...])
    xo_ref[...] = y
    if with_router:
        yb = y.astype(BF16)
        xb_ref[...] = yb
        idx, gate = _route_rows(yb, rw_ref[...], rb_ref[...])
        idx_ref[...] = idx
        gate_ref[...] = gate

    @pl.when(i == ntiles - 1)
    def _():
        _slot_wait(ybuf, 1 - slot, sem)


def combine_ln(pos, x, gate_sel, y_sorted, g, b, *, row0=0, nrows=None, router=None):
    d = x.shape[1]
    tm = LN_TM
    nrows = x.shape[0] if nrows is None else nrows
    tile0 = row0 // tm
    ntiles = nrows // tm
    row_in = pl.BlockSpec((tm, d), lambda i, p: (tile0 + i, 0))
    row_out = pl.BlockSpec((tm, d), lambda i, p: (i, 0))
    vec = pl.BlockSpec((1, d), lambda i, p: (0, 0))
    small = pl.BlockSpec((tm, LANES), lambda i, p: (i, 0))
    in_specs = [row_in, pl.BlockSpec((tm, LANES), lambda i, p: (tile0 + i, 0)),
                pl.BlockSpec(memory_space=pl.ANY), vec, vec]
    args = [pos, x, gate_sel, y_sorted, g, b]
    out_shape = [jax.ShapeDtypeStruct((nrows, d), F32)]
    out_specs = [row_out]
    if router is not None:
        in_specs += [pl.BlockSpec((d, LANES), lambda i, p: (0, 0)), pl.BlockSpec((1, LANES), lambda i, p: (0, 0))]
        args += list(router)
        out_shape += [jax.ShapeDtypeStruct((nrows, d), BF16), jax.ShapeDtypeStruct((nrows, LANES), I32),
                      jax.ShapeDtypeStruct((nrows, LANES), F32)]
        out_specs += [row_out, small, small]
    grid_spec = pltpu.PrefetchScalarGridSpec(
        num_scalar_prefetch=1,
        grid=(ntiles,),
        in_specs=in_specs,
        out_specs=tuple(out_specs),
        scratch_shapes=[pltpu.VMEM((2, 2, tm, d), F32), pltpu.SemaphoreType.DMA((2,))],
    )
    kern = functools.partial(_combine_ln_kernel, tm=tm, tile0=tile0, ntiles=ntiles,
                             with_router=router is not None)
    return pl.pallas_call(
        kern,
        out_shape=tuple(out_shape),
        grid_spec=grid_spec,
        compiler_params=_cparams(("arbitrary",)),
        name="combine_ln",
    )(*args)


def _route_tables(idx):
    t = idx.shape[0]
    e_flat = idx[:, :2].reshape(-1)
    onehot = (e_flat[:, None] == jnp.arange(N_EXPERTS, dtype=I32)[None, :]).astype(I32)
    csum = jnp.cumsum(onehot, axis=0)
    rank = jnp.take_along_axis(csum, e_flat[:, None], axis=1)[:, 0] - 1
    counts = csum[-1]
    padded = ((counts + MOE_TM - 1) // MOE_TM) * MOE_TM
    ends = jnp.cumsum(padded)
    starts = ends - padded
    pos = (starts[e_flat] + rank).astype(I32)
    src = jnp.zeros((MOE_ROWS,), I32).at[pos].set(jnp.arange(2 * t, dtype=I32) // 2)
    nvalid = (ends[-1] // MOE_TM).astype(I32)
    tile_start = jnp.arange(MOE_TILES, dtype=I32) * MOE_TM
    te = jnp.sum((tile_start[:, None] >= ends[None, :]).astype(I32), axis=1)
    te = jnp.minimum(te, te[nvalid - 1]).astype(I32)
    return pos, src, te, nvalid.reshape(1)


def moe_layer(x_f32, idx, wg, wu, wd, layer):
    pos, src, te, nvalid = _route_tables(idx)
    h = moe_up(te, src, nvalid, x_f32, wg, wu, layer)
    y_sorted = moe_down(te, nvalid, h, wd, layer)
    return pos, y_sorted


def _conv_silu(x, prev8, w, b):
    row8 = lax.broadcasted_iota(I32, prev8.shape, 0)
    acc = jnp.broadcast_to(b, x.shape)
    for k in range(SSD_CONV_W):
        s = SSD_CONV_W - 1 - k
        if s == 0:
            xs = x
        else:
            xs = pltpu.roll(x, s, axis=0)
            top = jnp.where(row8 < s, pltpu.roll(prev8, s, axis=0), xs[:SUBLANES])
            xs = jnp.concatenate([top, xs[SUBLANES:]], axis=0)
        acc = acc + xs * w[k:k + 1, :]
    return _silu(acc)


def _dt_kernel(dt_ref, bias_ref, alog_ref, dt_o, dta_o):
    v = dt_ref[...] + bias_ref[...]
    sp = jnp.maximum(v, 0.0) + jnp.log1p(jnp.exp(-jnp.abs(v)))
    dt_o[...] = sp
    dta_o[...] = sp * (-jnp.exp(alog_ref[...]))


def dt_prep(dt_raw, bias, a_log):
    t, h = dt_raw.shape
    tr = t // 8
    row = pl.BlockSpec((tr, h), lambda i: (i, 0))
    vec = pl.BlockSpec((1, h), lambda i: (0, 0))
    return pl.pallas_call(
        _dt_kernel,
        out_shape=(jax.ShapeDtypeStruct((t, h), F32), jax.ShapeDtypeStruct((t, h), F32)),
        grid=(8,),
        in_specs=[row, vec, vec],
        out_specs=(row, row),
        compiler_params=_cparams(("parallel",)),
        name="dt_prep",
    )(dt_raw, bias, a_log)


def _ssd_kernel(x_ref, b_ref, c_ref, px_ref, pb_ref, pc_ref, wx_ref, wb_ref, wc_ref, bx_ref, bb_ref, bc_ref,
                z_ref, dt_ref, dta_ref, dtt_ref, dtat_ref, h0_ref, dskip_ref, nw_ref,
                y_ref, hout_ref, state, halo_x, halo_b, halo_c, *, q, nchunks):
    c = pl.program_id(2)
    gc = SSD_GROUP_CH
    n = SSD_D_STATE
    streams = ((x_ref, px_ref, wx_ref, bx_ref, halo_x), (b_ref, pb_ref, wb_ref, bb_ref, halo_b),
               (c_ref, pc_ref, wc_ref, bc_ref, halo_c))

    @pl.when(c == 0)
    def _():
        for k in range(SSD_GROUPS_PER_STEP):
            state[k] = h0_ref[0, k].T
        for _, prev_ref, _, _, halo in streams:
            halo[...] = prev_ref[0]

    acts = []
    for raw_ref, _, w_ref, bias_ref, halo in streams:
        raw = raw_ref[...]
        acts.append(_conv_silu(raw, halo[...], w_ref[...], bias_ref[...]))
        halo[...] = raw[q - SUBLANES:, :]
    xa, ba, ca = acts

    new_states = []
    for k in range(SSD_GROUPS_PER_STEP):
        cols = slice(k * gc, (k + 1) * gc)
        y, s_new = _ssd_group(
            xa[:, cols], ba[:, k * n:(k + 1) * n], ca[:, k * n:(k + 1) * n], z_ref[:, cols],
            dt_ref[k, 0], dta_ref[k, 0], dtt_ref[k, 0], dtat_ref[k, 0], state[k],
            dskip_ref[:, cols], nw_ref[:, cols], q=q)
        y_ref[:, cols] = y
        state[k] = s_new
        new_states.append(s_new)

    @pl.when(c == nchunks - 1)
    def _():
        for k in range(SSD_GROUPS_PER_STEP):
            hout_ref[0, k] = new_states[k].T


def _ssd_group(x, bm, cm, z, dt, dta, dtt, dtat, s_old, dskip, nw, *, q):
    hg = SSD_HEADS_PER_GROUP
    p = SSD_HEAD_DIM
    xb = x.astype(BF16)
    bm = bm.astype(BF16)
    cm = cm.astype(BF16)

    ri = lax.broadcasted_iota(I32, (q, q), 0)
    ci = lax.broadcasted_iota(I32, (q, q), 1)
    causal = ri >= ci
    lower = jnp.where(causal, 1.0, 0.0).astype(BF16)
    upper = jnp.where(ri <= ci, 1.0, 0.0).astype(BF16)
    acum = _dot_sel(lower, dta)
    acumt = _dot_sel_r(dtat, upper)
    alast = acum[q - 1:q, :]

    cb = _dot_nt(cm, bm)
    lane = lax.broadcasted_iota(I32, (q, 2 * p), 1)
    y_pairs = []
    for hp in range(hg // 2):
        ms = []
        for h in (2 * hp, 2 * hp + 1):
            seg = acum[:, h:h + 1] - acumt[h:h + 1, :]
            decay = jnp.exp(jnp.where(causal, seg, -jnp.inf))
            ms.append((cb * decay * dtt[h:h + 1, :]).astype(BF16))
        lhs = jnp.concatenate(ms, axis=1)
        xp = xb[:, 2 * p * hp:2 * p * (hp + 1)]
        zero = jnp.zeros_like(xp)
        rhs = jnp.concatenate([jnp.where(lane < p, xp, zero), jnp.where(lane >= p, xp, zero)], axis=0)
        y_pairs.append(_dot(lhs, rhs))
    y = jnp.concatenate(y_pairs, axis=1)

    hi = lax.broadcasted_iota(I32, (hg, hg * p), 0)
    li = lax.broadcasted_iota(I32, (hg, hg * p), 1)
    widen = jnp.where((li >= hi * p) & (li < (hi + 1) * p), 1.0, 0.0).astype(BF16)
    scales = jnp.concatenate([jnp.exp(acum), jnp.exp(alast - acum) * dt,
                              jnp.broadcast_to(jnp.exp(alast), (SUBLANES, hg))], axis=0)
    wide = _dot_sel_r2(scales, widen)
    e_in = wide[:q]
    e_out = wide[q:2 * q]
    e_all = wide[2 * q:2 * q + 1]

    y = y + _dot(cm, s_old.astype(BF16)) * e_in
    xw = (x * e_out).astype(BF16)
    s_new = s_old * e_all + _dot_tn(bm, xw)

    y = y + dskip * x
    y = y * _silu(z)
    y = y * lax.rsqrt(jnp.mean(y * y, axis=-1, keepdims=True) + RMS_EPS)
    return (y * nw).astype(BF16), s_new


def ssd_scan(xbc, conv_prev, conv_w, conv_b, z, dt4, dta4, dtt4, dtat4, h0, dskip, norm_w, *,
             row0, nseq, seqlen, q, out_alias=None):
    t = xbc.shape[0]
    gps = SSD_GROUPS_PER_STEP
    g = SSD_N_GROUPS
    gc = gps * SSD_GROUP_CH
    n = SSD_D_STATE
    nchunks = seqlen // q
    rb0 = row0 // q
    bcol0 = SSD_D_INNER // (gps * n)
    ccol0 = (SSD_D_INNER + SSD_BC_DIM) // (gps * n)

    def rows(b, gg, c):
        return rb0 + b * nchunks + c

    def col_specs(shape_of, index_of):
        return [pl.BlockSpec(shape_of(gc), index_of(0)), pl.BlockSpec(shape_of(gps * n), index_of(bcol0)),
                pl.BlockSpec(shape_of(gps * n), index_of(ccol0))]

    in_specs = (col_specs(lambda w: (q, w), lambda c0: lambda b, gg, c: (rows(b, gg, c), c0 + gg))
                + col_specs(lambda w: (1, SUBLANES, w), lambda c0: lambda b, gg, c: (b, 0, c0 + gg))
                + col_specs(lambda w: (SSD_CONV_W, w), lambda c0: lambda b, gg, c: (0, c0 + gg))
                + col_specs(lambda w: (1, w), lambda c0: lambda b, gg, c: (0, c0 + gg)))
    in_specs += [pl.BlockSpec((q, gc), lambda b, gg, c: (rows(b, gg, c), gg)),
                pl.BlockSpec((gps, 1, q, 16), lambda b, gg, c: (gg, b * nchunks + c, 0, 0)),
                pl.BlockSpec((gps, 1, q, 16), lambda b, gg, c: (gg, b * nchunks + c, 0, 0)),
                pl.BlockSpec((gps, 1, 16, q), lambda b, gg, c: (gg, b * nchunks + c, 0, 0)),
                pl.BlockSpec((gps, 1, 16, q), lambda b, gg, c: (gg, b * nchunks + c, 0, 0)),
                pl.BlockSpec((1, gps, SSD_GROUP_CH, n), lambda b, gg, c: (b, gg, 0, 0)),
                pl.BlockSpec((1, gc), lambda b, gg, c: (0, gg)),
                pl.BlockSpec((1, gc), lambda b, gg, c: (0, gg))]
    args = [xbc] * 3 + [conv_prev] * 3 + [conv_w] * 3 + [conv_b] * 3 + [z, dt4, dta4, dtt4, dtat4, h0, dskip, norm_w]
    nin = len(args)
    kern = functools.partial(_ssd_kernel, q=q, nchunks=nchunks)
    aliases = {}
    if out_alias is not None:
        in_specs.append(pl.BlockSpec(memory_space=pl.ANY))
        args.append(out_alias)
        aliases = {nin: 0}
        base = kern
        kern = lambda *refs: base(*refs[:nin], *refs[nin + 1:])
    return pl.pallas_call(
        kern,
        out_shape=(jax.ShapeDtypeStruct((t, SSD_D_INNER), BF16),
                   jax.ShapeDtypeStruct((nseq, g, SSD_GROUP_CH, n), F32)),
        grid=(nseq, g // gps, nchunks),
        in_specs=in_specs,
        out_specs=(pl.BlockSpec((q, gc), lambda b, gg, c: (rows(b, gg, c), gg)),
                   pl.BlockSpec((1, gps, SSD_GROUP_CH, n), lambda b, gg, c: (b, gg, 0, 0))),
        scratch_shapes=[pltpu.VMEM((gps, n, SSD_GROUP_CH), F32), pltpu.VMEM((SUBLANES, gc), F32),
                        pltpu.VMEM((SUBLANES, gps * n), F32), pltpu.VMEM((SUBLANES, gps * n), F32)],
        input_output_aliases=aliases,
        compiler_params=_cparams(("parallel", "parallel", "arbitrary")),
        name="ssd_scan",
    )(*args)


def _dt_layouts(v, row0, nseq, seqlen, q):
    nb = nseq * seqlen // q
    part = v[row0:row0 + nseq * seqlen].reshape(nb, q, SSD_N_GROUPS, SSD_HEADS_PER_GROUP)
    return part.transpose(2, 0, 1, 3), part.transpose(2, 0, 3, 1)


def _s5_prep_kernel(lr_ref, li_ref, ls_ref, bre_ref, bim_ref, cre_ref, cim_ref, wst_ref, vt_ref, krev_ref, aq_ref):
    lr = lr_ref[0]
    li = li_ref[0]
    dstep = jnp.exp(ls_ref[0])
    mag = jnp.exp(lr * dstep)
    ar = mag * jnp.cos(li * dstep)
    ai = mag * jnp.sin(li * dstep)
    den = lr * lr + li * li
    cr = ((ar - 1.0) * lr + ai * li) / den
    ci = (ai * lr - (ar - 1.0) * li) / den
    bre = bre_ref[0]
    bim = bim_ref[0]
    bbr = cr * bre - ci * bim
    bbi = cr * bim + ci * bre
    bmat = jnp.concatenate([bbr, bbi], axis=1)
    cre = cre_ref[0]
    cim = cim_ref[0]

    def power(m):
        e = jnp.exp((m * lr) * dstep)
        return e * jnp.cos((m * li) * dstep), e * jnp.sin((m * li) * dstep)

    powers = [power(float(m)) for m in range(S5_Q + 1)]

    def out_rows(m):
        pr, pi = powers[m]
        return jnp.concatenate([cre * pr - cim * pi, -(cre * pi + cim * pr)], axis=1)

    for t in range(S5_Q):
        pr, pi = powers[S5_Q - 1 - t]
        blk = jnp.concatenate([pr * bbr - pi * bbi, pr * bbi + pi * bbr], axis=1)
        wst_ref[0, t * LANES:(t + 1) * LANES, :] = blk.astype(BF16)
        vt_ref[0, t * LANES:(t + 1) * LANES, :] = out_rows(t + 1).astype(BF16)
        lag = S5_Q - 1 - t
        kk = lax.dot_general(bmat, out_rows(lag), (((1,), (1,)), ((), ())),
                             precision=lax.Precision.HIGHEST, preferred_element_type=F32)
        krev_ref[0, t * LANES:(t + 1) * LANES, :] = kk.astype(BF16)
    pr, pi = powers[S5_Q]
    aq_ref[0] = jnp.concatenate([pr, pi], axis=1)


def s5_prep(lr, li, ls, bre, bim, cre, cim):
    np_ = S5_N_PACKS
    sp = S5_PACK_STATE
    vec = pl.BlockSpec((1, 1, sp), lambda i: (i, 0, 0))
    mat = pl.BlockSpec((1, LANES, sp), lambda i: (i, 0, 0))
    big = pl.BlockSpec((1, S5_Q * LANES, 2 * sp), lambda i: (i, 0, 0))
    return pl.pallas_call(
        _s5_prep_kernel,
        out_shape=(jax.ShapeDtypeStruct((np_, S5_Q * LANES, 2 * sp), BF16),
                   jax.ShapeDtypeStruct((np_, S5_Q * LANES, 2 * sp), BF16),
                   jax.ShapeDtypeStruct((np_, S5_Q * LANES, LANES), BF16),
                   jax.ShapeDtypeStruct((np_, 1, 2 * sp), F32)),
        grid=(np_,),
        in_specs=[vec, vec, vec, mat, mat, mat, mat],
        out_specs=(big, big, pl.BlockSpec((1, S5_Q * LANES, LANES), lambda i: (i, 0, 0)),
                   pl.BlockSpec((1, 1, 2 * sp), lambda i: (i, 0, 0))),
        compiler_params=_cparams(("parallel",)),
        name="s5_prep",
    )(lr, li, ls, bre, bim, cre, cim)


def _s5_kernel(x_ref, wst_ref, vt_ref, krev_ref, aq_ref, d_ref, h0_ref, g_ref, hout_ref, *scratch,
               nblk, nseq, chain):
    q = S5_Q
    sp = S5_PACK_STATE
    us = [x_ref[pl.ds(t, nblk, stride=q), :] for t in range(q)]
    ucat = jnp.concatenate([u.astype(BF16) for u in us], axis=1)
    s_in = _dot(ucat, wst_ref[0])
    aq = aq_ref[0]
    aqr, aqi = aq[:, :sp], aq[:, sp:]

    def advance(h, s):
        hr, hi = h[:, :sp], h[:, sp:]
        return jnp.concatenate([aqr * hr - aqi * hi + s[:, :sp], aqr * hi + aqi * hr + s[:, sp:]], axis=1)

    if chain:
        hin_s, s_s, y_s = scratch
        per_seq = nblk // nseq
        s_s[...] = s_in

        def body(c, hs):
            out = []
            for s in range(nseq):
                row = s * per_seq + c
                hin_s[pl.ds(row, 1), :] = hs[s]
                out.append(advance(hs[s], s_s[pl.ds(row, 1), :]))
            return tuple(out)

        h0 = h0_ref[0, 0]
        hs = lax.fori_loop(0, per_seq, body, tuple(h0[s:s + 1] for s in range(nseq)))
        hout_ref[0, 0] = jnp.concatenate(hs, axis=0)
        hin = hin_s[...]
    else:
        (y_s,) = scratch
        hin = h0_ref[0, 0]
        hout_ref[0, 0] = advance(hin, s_in)
    hinb = hin.astype(BF16)
    d = d_ref[...]
    for t in range(q):
        y = _dot_nt(hinb, vt_ref[0, t * LANES:(t + 1) * LANES, :])
        y = y + _dot(ucat[:, :(t + 1) * LANES], krev_ref[0, (q - 1 - t) * LANES:, :])
        y = y + d * us[t]
        y_s[pl.ds(t, nblk, stride=q), :] = _gelu_exact(y)
    g_ref[...] = y_s[...].astype(BF16)


def s5_scan(x, wst, vt, krev, aq, dskip, h0, *, row0, nseq, seqlen, chain, out_alias=None):
    t, d = x.shape
    q = S5_Q
    sp2 = 2 * S5_PACK_STATE
    spb = S5_CHAIN_SEQS if chain else nseq
    nstep = nseq // spb
    rows = spb * seqlen
    nblk = rows // q
    assert chain or seqlen == q
    grid = (S5_N_PACKS, nstep)
    rb0 = row0 // rows
    xmap = lambda p, b: (rb0 + b, p)
    wmap = lambda p, b: (p, 0, 0)
    dmap = lambda p, b: (0, p)
    hspec = pl.BlockSpec((1, 1, spb, sp2), lambda p, b: (p, b, 0, 0))
    hshape = (S5_N_PACKS, nstep, spb, sp2)
    scratch = [pltpu.VMEM((rows, LANES), F32)]
    if chain:
        scratch = [pltpu.VMEM((nblk, sp2), F32), pltpu.VMEM((nblk, sp2), F32)] + scratch
    sem = ("parallel", "parallel")
    in_specs = [pl.BlockSpec((rows, LANES), xmap),
                pl.BlockSpec((1, q * LANES, sp2), wmap),
                pl.BlockSpec((1, q * LANES, sp2), wmap),
                pl.BlockSpec((1, q * LANES, LANES), wmap),
                pl.BlockSpec((1, 1, sp2), wmap),
                pl.BlockSpec((1, LANES), dmap),
                hspec]
    args = [x, wst, vt, krev, aq, dskip, h0]
    kern = functools.partial(_s5_kernel, nblk=nblk, nseq=spb, chain=chain)
    aliases = {}
    if out_alias is not None:
        in_specs.append(pl.BlockSpec(memory_space=pl.ANY))
        args.append(out_alias)
        aliases = {len(args) - 1: 0}
        base = kern
        kern = lambda *refs: base(*refs[:7], *refs[8:])
    return pl.pallas_call(
        kern,
        out_shape=(jax.ShapeDtypeStruct((t, d), BF16), jax.ShapeDtypeStruct(hshape, F32)),
        grid=grid,
        in_specs=in_specs,
        out_specs=(pl.BlockSpec((rows, LANES), xmap), hspec),
        scratch_shapes=scratch,
        input_output_aliases=aliases,
        compiler_params=_cparams(sem),
        name="s5_scan",
    )(*args)


def _s5_block_diag(m):
    m = m.reshape(S5_N_PACKS, S5_PACK_GROUPS, S5_GROUP_CH, S5_STATE)
    eye = jnp.eye(S5_PACK_GROUPS, dtype=m.dtype)
    out = m[:, :, :, None, :] * eye[None, :, None, :, None]
    return out.reshape(S5_N_PACKS, S5_PACK_GROUPS * S5_GROUP_CH, S5_PACK_STATE)


def _s5_pack_state(re, im):
    b = re.shape[0]
    return jnp.concatenate([re.reshape(b, S5_N_PACKS, S5_PACK_STATE), im.reshape(b, S5_N_PACKS, S5_PACK_STATE)], axis=-1)


def _s5_unpack_state(h):
    b = h.shape[0]
    return (h[..., :S5_PACK_STATE].reshape(b, S5_N_GROUPS, S5_STATE),
            h[..., S5_PACK_STATE:].reshape(b, S5_N_GROUPS, S5_STATE))


def kernel(x_prompt, x_sample, cache_ssd_conv, state_ssd, state_s5_re, state_s5_im, ssd_w_in, ssd_conv_w, ssd_conv_b, ssd_dt_bias, ssd_a_log, ssd_d, ssd_norm_w, ssd_w_out, s5_lambda_re, s5_lambda_im, s5_log_step, s5_b_re, s5_b_im, s5_c_re, s5_c_im, s5_d, s5_w_glu, ln_mix_g, ln_mix_b, ln_ffn_g, ln_ffn_b, router_w, router_b, moe_w_gate, moe_w_up, moe_w_down):
    d = D_MODEL
    x0_parts = [x_prompt.reshape(T_PROMPT, d), x_sample.reshape(T_SAMPLE, d)]
    x0b = cast_rows(x0_parts, BF16, tm=T_SAMPLE)
    rw = jnp.zeros((d, LANES), BF16).at[:, :N_EXPERTS].set(router_w.astype(BF16))
    rb = jnp.zeros((1, LANES), F32).at[0, :N_EXPERTS].set(router_b)

    w_in = ssd_w_in[0].astype(BF16)
    z = matmul(x0b, w_in, tm=MM_TM, tn=1024, col0=0, ncols=SSD_D_INNER)
    xbc = matmul(x0b, w_in, tm=MM_TM, tn=1024, col0=SSD_D_INNER, ncols=SSD_CONV_DIM)
    dt_raw = matmul(x0b, w_in, tm=MM_TM, tn=LANES, col0=SSD_D_INNER + SSD_CONV_DIM, ncols=SSD_N_HEADS)

    conv_w = ssd_conv_w[0]
    conv_b = ssd_conv_b[0].reshape(1, SSD_CONV_DIM)
    prev_p = jnp.zeros((BATCH, SUBLANES, SSD_CONV_DIM), F32)
    prev_s = jnp.concatenate([jnp.zeros((DEC_BATCH, SUBLANES - (SSD_CONV_W - 1), SSD_CONV_DIM), F32),
                              cache_ssd_conv[0]], axis=1)

    dt_sp, dta = dt_prep(dt_raw, ssd_dt_bias[0].reshape(1, SSD_N_HEADS), ssd_a_log[0].reshape(1, SSD_N_HEADS))
    dskip = jnp.repeat(ssd_d[0], SSD_HEAD_DIM).reshape(1, SSD_D_INNER)
    norm_w = ssd_norm_w[0].reshape(1, SSD_D_INNER)
    h0_p = jnp.zeros((BATCH, SSD_N_GROUPS, SSD_GROUP_CH, SSD_D_STATE), F32)
    h0_s = state_ssd[0].reshape(DEC_BATCH, SSD_N_GROUPS, SSD_GROUP_CH, SSD_D_STATE)
    lay_p = _dt_layouts(dt_sp, 0, BATCH, SEQ, SSD_Q_PROMPT) + _dt_layouts(dta, 0, BATCH, SEQ, SSD_Q_PROMPT)
    lay_s = (_dt_layouts(dt_sp, T_PROMPT, DEC_BATCH, DEC_SEQ, DEC_SEQ)
             + _dt_layouts(dta, T_PROMPT, DEC_BATCH, DEC_SEQ, DEC_SEQ))
    ymix, ssd_p = ssd_scan(xbc, prev_p, conv_w, conv_b, z, lay_p[0], lay_p[2], lay_p[1], lay_p[3], h0_p, dskip,
                           norm_w, row0=0, nseq=BATCH, seqlen=SEQ, q=SSD_Q_PROMPT)
    ymix, ssd_s = ssd_scan(xbc, prev_s, conv_w, conv_b, z, lay_s[0], lay_s[2], lay_s[1], lay_s[3], h0_s, dskip,
                           norm_w, row0=T_PROMPT, nseq=DEC_BATCH, seqlen=DEC_SEQ, q=DEC_SEQ, out_alias=ymix)
    mix = matmul(ymix, ssd_w_out[0].astype(BF16), tm=MM_TM, tn=512)

    wg = cast_weights(moe_w_gate, BF16, rows_per_block=D_MODEL // 2)
    wu = cast_weights(moe_w_up, BF16, rows_per_block=D_MODEL // 2)
    wd = cast_weights(moe_w_down, BF16, rows_per_block=D_EXPERT // 2)
    x1, idx, gate = ln_router(x0_parts, mix, ln_mix_g[0].reshape(1, d), ln_mix_b[0].reshape(1, d), rw, rb)
    pos, y_sorted = moe_layer(x1, idx, wg, wu, wd, 0)
    (x2,) = combine_ln(pos, x1, gate, y_sorted, ln_ffn_g[0].reshape(1, d), ln_ffn_b[0].reshape(1, d))

    lr = s5_lambda_re[0].reshape(S5_N_PACKS, 1, S5_PACK_STATE)
    li = s5_lambda_im[0].reshape(S5_N_PACKS, 1, S5_PACK_STATE)
    ls = jnp.repeat(s5_log_step[0], S5_STATE).reshape(S5_N_PACKS, 1, S5_PACK_STATE)
    wst, vt, krev, aq = s5_prep(lr, li, ls,
                                _s5_block_diag(s5_b_re[0].transpose(0, 2, 1)),
                                _s5_block_diag(s5_b_im[0].transpose(0, 2, 1)),
                                _s5_block_diag(s5_c_re[0]), _s5_block_diag(s5_c_im[0]))
    s5_dskip = s5_d[0].reshape(1, d)
    sp2 = 2 * S5_PACK_STATE
    hs_p = jnp.zeros((S5_N_PACKS, BATCH // S5_CHAIN_SEQS, S5_CHAIN_SEQS, sp2), F32)
    hs_s = _s5_pack_state(state_s5_re[0], state_s5_im[0]).transpose(1, 0, 2)[:, None]
    gact, s5_p = s5_scan(x2, wst, vt, krev, aq, s5_dskip, hs_p, row0=0, nseq=BATCH, seqlen=SEQ, chain=True)
    gact, s5_s = s5_scan(x2, wst, vt, krev, aq, s5_dskip, hs_s, row0=T_PROMPT, nseq=DEC_BATCH, seqlen=DEC_SEQ,
                         chain=False, out_alias=gact)
    mix = glu_matmul(gact, s5_w_glu[0].astype(BF16), tm=MM_TM, tn=512)

    x3, idx, gate = ln_router([x2], mix, ln_mix_g[1].reshape(1, d), ln_mix_b[1].reshape(1, d), rw, rb)
    pos, y_sorted = moe_layer(x3, idx, wg, wu, wd, 1)
    g1 = ln_ffn_g[1].reshape(1, d)
    b1 = ln_ffn_b[1].reshape(1, d)
    (y_p,) = combine_ln(pos, x3, gate, y_sorted, g1, b1, row0=0, nrows=T_PROMPT)
    (y_s,) = combine_ln(pos, x3, gate, y_sorted, g1, b1, row0=T_PROMPT, nrows=T_SAMPLE)

    keep = SSD_CONV_W - 1
    conv_p = jnp.stack([xbc[(b + 1) * SEQ - keep:(b + 1) * SEQ] for b in range(BATCH)])[None]
    conv_s = xbc[T_PROMPT:].reshape(DEC_BATCH, DEC_SEQ, SSD_CONV_DIM)[:, DEC_SEQ - keep:][None]
    ssd_state_p = ssd_p.reshape(1, BATCH, SSD_N_HEADS, SSD_HEAD_DIM, SSD_D_STATE)
    ssd_state_s = ssd_s.reshape(1, DEC_BATCH, SSD_N_HEADS, SSD_HEAD_DIM, SSD_D_STATE)
    re_p, im_p = _s5_unpack_state(s5_p.reshape(S5_N_PACKS, BATCH, sp2).transpose(1, 0, 2))
    re_s, im_s = _s5_unpack_state(s5_s[:, 0].transpose(1, 0, 2))
    return (y_p.reshape(BATCH, SEQ, d), y_s.reshape(DEC_BATCH, DEC_SEQ, d),
            conv_p, ssd_state_p, re_p[None], im_p[None],
            conv_s, ssd_state_s, re_s[None], im_s[None])
```

```python
import functools
import math

import jax
import jax.numpy as jnp
from jax import lax
from jax.experimental import pallas as pl
from jax.experimental.pallas import tpu as pltpu

F32 = jnp.float32
BF16 = jnp.bfloat16
I32 = jnp.int32

D_MODEL = 4096
BATCH, SEQ = 4, 4096
DEC_BATCH, DEC_SEQ = 32, 16
T_PROMPT = BATCH * SEQ
T_SAMPLE = DEC_BATCH * DEC_SEQ
T_ALL = T_PROMPT + T_SAMPLE
SSD_D_INNER = 8192
SSD_HEAD_DIM = 64
SSD_N_HEADS = 128
SSD_N_GROUPS = 8
SSD_HEADS_PER_GROUP = 16
SSD_D_STATE = 128
SSD_GROUP_CH = SSD_D_INNER // SSD_N_GROUPS
SSD_BC_DIM = SSD_N_GROUPS * SSD_D_STATE
SSD_CONV_DIM = SSD_D_INNER + 2 * SSD_BC_DIM
SSD_CONV_W = 4
S5_GROUP_CH = 16
S5_N_GROUPS = 256
S5_STATE = 64
S5_PACK_GROUPS = 8
S5_N_PACKS = S5_N_GROUPS // S5_PACK_GROUPS
S5_PACK_STATE = S5_PACK_GROUPS * S5_STATE
S5_Q = 16
S5_CHAIN_SEQS = 2
N_EXPERTS = 16
EXPERTS_PER_GROUP = 4
D_EXPERT = 1024
DEPTH = 2
DEEPNORM_ALPHA = (2 * DEPTH) ** 0.25
LN_EPS = 1e-5
RMS_EPS = 1e-5

LANES = 128
SUBLANES = 8
VMEM_LIMIT_BYTES = 56 * 1024 * 1024

MM_TM = 768
MOE_TM = 256
MOE_ROWS = 2 * T_ALL + N_EXPERTS * MOE_TM
MOE_TILES = MOE_ROWS // MOE_TM
LN_TM = 128
SSD_Q_PROMPT = 128
SSD_GROUPS_PER_STEP = 4


def _cparams(semantics):
    return pltpu.CompilerParams(dimension_semantics=semantics, vmem_limit_bytes=VMEM_LIMIT_BYTES)


def _dot(a, b):
    return jnp.dot(a, b, preferred_element_type=F32)


def _dot_nt(a, b):
    return lax.dot_general(a, b, (((1,), (1,)), ((), ())), preferred_element_type=F32)


def _dot_tn(a, b):
    return lax.dot_general(a, b, (((0,), (0,)), ((), ())), preferred_element_type=F32)


def _split3(v):
    hi = v.astype(BF16)
    r = v - hi.astype(F32)
    mid = r.astype(BF16)
    lo = (r - mid.astype(F32)).astype(BF16)
    return hi, mid, lo


def _dot_sel(sel, v):
    hi, mid, lo = _split3(v)
    return _dot(sel, hi) + _dot(sel, mid) + _dot(sel, lo)


def _dot_sel_r(v, sel):
    hi, mid, lo = _split3(v)
    return _dot(hi, sel) + _dot(mid, sel) + _dot(lo, sel)


def _dot_sel_r2(v, sel):
    hi = v.astype(BF16)
    lo = (v - hi.astype(F32)).astype(BF16)
    return _dot(hi, sel) + _dot(lo, sel)


def _silu(x):
    return x * jax.nn.sigmoid(x)


def _gelu_exact(x):
    return 0.5 * x * (1.0 + lax.erf(x * (1.0 / math.sqrt(2.0))))


def _mm_kernel(x_ref, w_ref, o_ref):
    o_ref[...] = _dot(x_ref[...], w_ref[...]).astype(o_ref.dtype)


def matmul(x, w, *, tm, tn, col0=0, ncols=None, out_dtype=F32):
    m, k = x.shape
    n = w.shape[1] if ncols is None else ncols
    c0 = col0 // tn
    return pl.pallas_call(
        _mm_kernel,
        out_shape=jax.ShapeDtypeStruct((m, n), out_dtype),
        grid=(n // tn, m // tm),
        in_specs=[pl.BlockSpec((tm, k), lambda j, i: (i, 0)),
                  pl.BlockSpec((k, tn), lambda j, i: (0, c0 + j))],
        out_specs=pl.BlockSpec((tm, tn), lambda j, i: (i, j)),
        compiler_params=_cparams(("parallel", "parallel")),
        name="matmul",
    )(x, w)


def _glu_kernel(x_ref, w1_ref, w2_ref, o_ref):
    x = x_ref[...]
    z1 = _dot(x, w1_ref[...])
    z2 = _dot(x, w2_ref[...])
    o_ref[...] = z1 * jax.nn.sigmoid(z2)


def glu_matmul(x, w, *, tm, tn):
    m, k = x.shape
    n = w.shape[1] // 2
    half = n // tn
    return pl.pallas_call(
        _glu_kernel,
        out_shape=jax.ShapeDtypeStruct((m, n), F32),
        grid=(n // tn, m // tm),
        in_specs=[pl.BlockSpec((tm, k), lambda j, i: (i, 0)),
                  pl.BlockSpec((k, tn), lambda j, i: (0, j)),
                  pl.BlockSpec((k, tn), lambda j, i: (0, half + j))],
        out_specs=pl.BlockSpec((tm, tn), lambda j, i: (i, j)),
        compiler_params=_cparams(("parallel", "parallel")),
        name="glu_matmul",
    )(x, w, w)


def _layer_norm(v, g, b):
    mu = jnp.mean(v, axis=-1, keepdims=True)
    d = v - mu
    var = jnp.mean(d * d, axis=-1, keepdims=True)
    return d * lax.rsqrt(var + LN_EPS) * g + b


def _route_rows(xb, rw, rb):
    tm = xb.shape[0]
    lane = lax.broadcasted_iota(I32, (tm, LANES), 1)
    live = lane < N_EXPERTS
    logits = _dot(xb, rw) + rb
    logits = jnp.where(live, logits, -jnp.inf)
    m = jnp.max(logits, axis=-1, keepdims=True)
    e = jnp.exp(logits - m)
    probs = e / jnp.sum(e, axis=-1, keepdims=True)
    best = None
    for g in range(N_EXPERTS // EXPERTS_PER_GROUP):
        in_g = (lane >= g * EXPERTS_PER_GROUP) & (lane < (g + 1) * EXPERTS_PER_GROUP)
        score = jnp.max(jnp.where(in_g, probs, -1.0), axis=-1, keepdims=True)
        if best is None:
            best, best_g = score, jnp.zeros((tm, 1), I32)
        else:
            upd = score > best
            best = jnp.where(upd, score, best)
            best_g = jnp.where(upd, g, best_g)
    lo = best_g * EXPERTS_PER_GROUP
    in_best = (lane >= lo) & (lane < lo + EXPERTS_PER_GROUP)
    cand = jnp.where(in_best, probs, -1.0)
    p1 = jnp.max(cand, axis=-1, keepdims=True)
    i1 = jnp.min(jnp.where(cand == p1, lane, LANES), axis=-1, keepdims=True)
    cand2 = jnp.where(lane == i1, -2.0, cand)
    p2 = jnp.max(cand2, axis=-1, keepdims=True)
    i2 = jnp.min(jnp.where(cand2 == p2, lane, LANES), axis=-1, keepdims=True)
    tot = p1 + p2
    idx = jnp.where(lane == 0, i1, jnp.where(lane == 1, i2, 0))
    gate = jnp.where(lane == 0, p1 / tot, jnp.where(lane == 1, p2 / tot, 0.0))
    return idx, gate


def _row_part_specs(parts, tm, d):
    specs, firsts, t0 = [], [], 0
    for p in parts:
        nt = p.shape[0] // tm
        specs.append(pl.BlockSpec((tm, d), lambda i, *_, t0=t0, nt=nt: (jnp.clip(i - t0, 0, nt - 1), 0)))
        firsts.append(t0)
        t0 += nt
    return specs, firsts


def _select_part(i, refs, firsts, store):
    for k, ref in enumerate(refs):
        hi = firsts[k + 1] if k + 1 < len(refs) else None
        cond = i >= firsts[k] if hi is None else (i >= firsts[k]) & (i < hi)

        @pl.when(cond)
        def _(ref=ref):
            store(ref[...])


def _ln_router_kernel(*refs, firsts):
    n = len(firsts)
    x_refs = refs[:n]
    mix_ref, g_ref, b_ref, rw_ref, rb_ref, xo_ref, idx_ref, gate_ref = refs[n:]

    def finish(x):
        y = _layer_norm(DEEPNORM_ALPHA * x + mix_ref[...], g_ref[...], b_ref[...])
        xo_ref[...] = y
        idx, gate = _route_rows(y.astype(BF16), rw_ref[...], rb_ref[...])
        idx_ref[...] = idx
        gate_ref[...] = gate

    _select_part(pl.program_id(0), x_refs, firsts, finish)


def ln_router(x_parts, mix, g, b, rw, rb):
    t, d = mix.shape
    tm = LN_TM
    x_specs, firsts = _row_part_specs(x_parts, tm, d)
    row = pl.BlockSpec((tm, d), lambda i: (i, 0))
    vec = pl.BlockSpec((1, d), lambda i: (0, 0))
    small = pl.BlockSpec((tm, LANES), lambda i: (i, 0))
    return pl.pallas_call(
        functools.partial(_ln_router_kernel, firsts=tuple(firsts)),
        out_shape=(jax.ShapeDtypeStruct((t, d), F32),
                   jax.ShapeDtypeStruct((t, LANES), I32), jax.ShapeDtypeStruct((t, LANES), F32)),
        grid=(t // tm,),
        in_specs=x_specs + [row, vec, vec, pl.BlockSpec((d, LANES), lambda i: (0, 0)),
                            pl.BlockSpec((1, LANES), lambda i: (0, 0))],
        out_specs=(row, small, small),
        compiler_params=_cparams(("parallel",)),
        name="ln_router",
    )(*x_parts, mix, g, b, rw, rb)


def _cast_rows_kernel(*refs, firsts):
    o_ref = refs[-1]

    def store(v):
        o_ref[...] = v.astype(o_ref.dtype)

    _select_part(pl.program_id(0), refs[:-1], firsts, store)


def cast_rows(x_parts, dtype, tm):
    d = x_parts[0].shape[1]
    t = sum(p.shape[0] for p in x_parts)
    x_specs, firsts = _row_part_specs(x_parts, tm, d)
    return pl.pallas_call(
        functools.partial(_cast_rows_kernel, firsts=tuple(firsts)),
        out_shape=jax.ShapeDtypeStruct((t, d), dtype),
        grid=(t // tm,),
        in_specs=x_specs,
        out_specs=pl.BlockSpec((tm, d), lambda i: (i, 0)),
        compiler_params=_cparams(("parallel",)),
        name="cast_rows",
    )(*x_parts)


def _cast_kernel(x_ref, o_ref):
    o_ref[...] = x_ref[...].astype(o_ref.dtype)


def cast_weights(w, dtype, rows_per_block):
    nl, ne, r, c = w.shape
    spec = pl.BlockSpec((1, 1, rows_per_block, c), lambda l, e, j: (l, e, j, 0))
    return pl.pallas_call(
        _cast_kernel,
        out_shape=jax.ShapeDtypeStruct(w.shape, dtype),
        grid=(nl, ne, r // rows_per_block),
        in_specs=[spec],
        out_specs=spec,
        compiler_params=_cparams(("parallel", "parallel", "parallel")),
        name="cast_weights",
    )(w)


def _row_gather_start(src_hbm, row, dst, dst_row, sem, priority=0):
    pltpu.make_async_copy(src_hbm.at[pl.ds(row, 1)], dst.at[pl.ds(dst_row, 1)], sem).start(priority=priority)


def _slot_wait(buf, slot, sem):
    pltpu.make_async_copy(buf.at[slot], buf.at[slot], sem.at[slot]).wait()


def _moe_up_kernel(te_ref, src_ref, nv_ref, x_hbm, wg_ref, wu_ref, h_ref, xbuf, sem):
    del te_ref
    i = pl.program_id(0)
    slot = lax.rem(i, 2)
    nvalid = nv_ref[0]

    @pl.when(i == 0)
    def _():
        def body(r, carry):
            _row_gather_start(x_hbm, src_ref[r], xbuf.at[0], r, sem.at[0])
            return carry
        lax.fori_loop(0, MOE_TM, body, 0)

    @pl.when(i <= nvalid)
    def _():
        _slot_wait(xbuf, slot, sem)

    @pl.when(i < nvalid)
    def _():
        base = (i + 1) * MOE_TM
        for r in range(MOE_TM):
            _row_gather_start(x_hbm, src_ref[base + r], xbuf.at[1 - slot], r, sem.at[1 - slot], priority=r % 2)
        xb = xbuf[slot].astype(BF16)
        gate = _dot(xb, wg_ref[0, 0])
        up = _dot(xb, wu_ref[0, 0])
        h_ref[...] = (_silu(gate) * up).astype(BF16)

    @pl.when(i >= nvalid)
    def _():
        h_ref[...] = jnp.zeros_like(h_ref)


def moe_up(tile_expert, src_rows, nvalid, x, wg, wu, layer):
    d = x.shape[1]
    assert (2 * x.shape[0] + N_EXPERTS * (MOE_TM - 1)) // MOE_TM < MOE_TILES
    wspec = pl.BlockSpec((1, 1, d, D_EXPERT), lambda i, te, src, nv: (layer, te[i], 0, 0))
    grid_spec = pltpu.PrefetchScalarGridSpec(
        num_scalar_prefetch=3,
        grid=(MOE_TILES,),
        in_specs=[pl.BlockSpec(memory_space=pl.ANY), wspec, wspec],
        out_specs=pl.BlockSpec((MOE_TM, D_EXPERT), lambda i, te, src, nv: (i, 0)),
        scratch_shapes=[pltpu.VMEM((2, MOE_TM, d), F32), pltpu.SemaphoreType.DMA((2,))],
    )
    return pl.pallas_call(
        _moe_up_kernel,
        out_shape=jax.ShapeDtypeStruct((MOE_ROWS, D_EXPERT), BF16),
        grid_spec=grid_spec,
        compiler_params=_cparams(("arbitrary",)),
        name="moe_up",
    )(tile_expert, src_rows, nvalid, x, wg, wu)


def _moe_down_kernel(te_ref, nv_ref, h_ref, wd_ref, y_ref, wbf):
    i = pl.program_id(0)
    new_expert = (i == 0) | (te_ref[i] != te_ref[jnp.maximum(i - 1, 0)])

    @pl.when(new_expert & (i < nv_ref[0]))
    def _():
        wbf[...] = wd_ref[0, 0].astype(BF16)

    @pl.when(i < nv_ref[0])
    def _():
        y_ref[...] = _dot(h_ref[...], wbf[...])

    @pl.when(i >= nv_ref[0])
    def _():
        y_ref[...] = jnp.zeros_like(y_ref)


def moe_down(tile_expert, nvalid, h, wd, layer):
    d = wd.shape[3]
    grid_spec = pltpu.PrefetchScalarGridSpec(
        num_scalar_prefetch=2,
        grid=(MOE_TILES,),
        in_specs=[pl.BlockSpec((MOE_TM, D_EXPERT), lambda i, te, nv: (i, 0)),
                  pl.BlockSpec((1, 1, D_EXPERT, d), lambda i, te, nv: (layer, te[i], 0, 0))],
        out_specs=pl.BlockSpec((MOE_TM, d), lambda i, te, nv: (i, 0)),
        scratch_shapes=[pltpu.VMEM((D_EXPERT, d), BF16)],
    )
    return pl.pallas_call(
        _moe_down_kernel,
        out_shape=jax.ShapeDtypeStruct((MOE_ROWS, d), F32),
        grid_spec=grid_spec,
        compiler_params=_cparams(("arbitrary",)),
        name="moe_down",
    )(tile_expert, nvalid, h, wd)


def _combine_ln_kernel(pos_ref, x_ref, gsel_ref, y_hbm, g_ref, b_ref, *rest, tm, tile0, ntiles, with_router):
    if with_router:
        rw_ref, rb_ref, xo_ref, xb_ref, idx_ref, gate_ref, ybuf, sem = rest
    else:
        xo_ref, ybuf, sem = rest
    i = pl.program_id(0)
    slot = lax.rem(i, 2)

    @pl.when(i == 0)
    def _():
        def body(r, carry):
            _row_gather_start(y_hbm, pos_ref[2 * tile0 * tm + 2 * r], ybuf.at[0, 0], r, sem.at[0])
            _row_gather_start(y_hbm, pos_ref[2 * tile0 * tm + 2 * r + 1], ybuf.at[0, 1], r, sem.at[0])
            return carry
        lax.fori_loop(0, tm, body, 0)

    _slot_wait(ybuf, slot, sem)
    base = 2 * (tile0 + jnp.minimum(i + 1, ntiles - 1)) * tm
    for r in range(tm):
        _row_gather_start(y_hbm, pos_ref[base + 2 * r], ybuf.at[1 - slot, 0], r, sem.at[1 - slot], priority=0)
        _row_gather_start(y_hbm, pos_ref[base + 2 * r + 1], ybuf.at[1 - slot, 1], r, sem.at[1 - slot], priority=1)
    gsel = gsel_ref[...]
    ffn = gsel[:, 0:1] * ybuf[slot, 0] + gsel[:, 1:2] * ybuf[slot, 1]
    y = _layer_norm(DEEPNORM_ALPHA * x_ref[...] + ffn, g_ref[...], b_ref[...])
    xo_ref[...] = y
    if with_router:
        yb = y.astype(BF16)
        xb_ref[...] = yb
        idx, gate = _route_rows(yb, rw_ref[...], rb_ref[...])
        idx_ref[...] = idx
        gate_ref[...] = gate

    @pl.when(i == ntiles - 1)
    def _():
        _slot_wait(ybuf, 1 - slot, sem)


def combine_ln(pos, x, gate_sel, y_sorted, g, b, *, row0=0, nrows=None, router=None):
    d = x.shape[1]
    tm = LN_TM
    nrows = x.shape[0] if nrows is None else nrows
    tile0 = row0 // tm
    ntiles = nrows // tm
    row_in = pl.BlockSpec((tm, d), lambda i, p: (tile0 + i, 0))
    row_out = pl.BlockSpec((tm, d), lambda i, p: (i, 0))
    vec = pl.BlockSpec((1, d), lambda i, p: (0, 0))
    small = pl.BlockSpec((tm, LANES), lambda i, p: (i, 0))
    in_specs = [row_in, pl.BlockSpec((tm, LANES), lambda i, p: (tile0 + i, 0)),
                pl.BlockSpec(memory_space=pl.ANY), vec, vec]
    args = [pos, x, gate_sel, y_sorted, g, b]
    out_shape = [jax.ShapeDtypeStruct((nrows, d), F32)]
    out_specs = [row_out]
    if router is not None:
        in_specs += [pl.BlockSpec((d, LANES), lambda i, p: (0, 0)), pl.BlockSpec((1, LANES), lambda i, p: (0, 0))]
        args += list(router)
        out_shape += [jax.ShapeDtypeStruct((nrows, d), BF16), jax.ShapeDtypeStruct((nrows, LANES), I32),
                      jax.ShapeDtypeStruct((nrows, LANES), F32)]
        out_specs += [row_out, small, small]
    grid_spec = pltpu.PrefetchScalarGridSpec(
        num_scalar_prefetch=1,
        grid=(ntiles,),
        in_specs=in_specs,
        out_specs=tuple(out_specs),
        scratch_shapes=[pltpu.VMEM((2, 2, tm, d), F32), pltpu.SemaphoreType.DMA((2,))],
    )
    kern = functools.partial(_combine_ln_kernel, tm=tm, tile0=tile0, ntiles=ntiles,
                             with_router=router is not None)
    return pl.pallas_call(
        kern,
        out_shape=tuple(out_shape),
        grid_spec=grid_spec,
        compiler_params=_cparams(("arbitrary",)),
        name="combine_ln",
    )(*args)


def _route_tables(idx):
    t = idx.shape[0]
    e_flat = idx[:, :2].reshape(-1)
    onehot = (e_flat[:, None] == jnp.arange(N_EXPERTS, dtype=I32)[None, :]).astype(I32)
    csum = jnp.cumsum(onehot, axis=0)
    rank = jnp.take_along_axis(csum, e_flat[:, None], axis=1)[:, 0] - 1
    counts = csum[-1]
    padded = ((counts + MOE_TM - 1) // MOE_TM) * MOE_TM
    ends = jnp.cumsum(padded)
    starts = ends - padded
    pos = (starts[e_flat] + rank).astype(I32)
    src = jnp.zeros((MOE_ROWS,), I32).at[pos].set(jnp.arange(2 * t, dtype=I32) // 2)
    nvalid = (ends[-1] // MOE_TM).astype(I32)
    tile_start = jnp.arange(MOE_TILES, dtype=I32) * MOE_TM
    te = jnp.sum((tile_start[:, None] >= ends[None, :]).astype(I32), axis=1)
    te = jnp.minimum(te, te[nvalid - 1]).astype(I32)
    return pos, src, te, nvalid.reshape(1)


def moe_layer(x_f32, idx, wg, wu, wd, layer):
    pos, src, te, nvalid = _route_tables(idx)
    h = moe_up(te, src, nvalid, x_f32, wg, wu, layer)
    y_sorted = moe_down(te, nvalid, h, wd, layer)
    return pos, y_sorted


def _conv_silu(x, prev8, w, b):
    row8 = lax.broadcasted_iota(I32, prev8.shape, 0)
    acc = jnp.broadcast_to(b, x.shape)
    for k in range(SSD_CONV_W):
        s = SSD_CONV_W - 1 - k
        if s == 0:
            xs = x
        else:
            xs = pltpu.roll(x, s, axis=0)
            top = jnp.where(row8 < s, pltpu.roll(prev8, s, axis=0), xs[:SUBLANES])
            xs = jnp.concatenate([top, xs[SUBLANES:]], axis=0)
        acc = acc + xs * w[k:k + 1, :]
    return _silu(acc)


def _dt_kernel(dt_ref, bias_ref, alog_ref, dt_o, dta_o):
    v = dt_ref[...] + bias_ref[...]
    sp = jnp.maximum(v, 0.0) + jnp.log1p(jnp.exp(-jnp.abs(v)))
    dt_o[...] = sp
    dta_o[...] = sp * (-jnp.exp(alog_ref[...]))


def dt_prep(dt_raw, bias, a_log):
    t, h = dt_raw.shape
    tr = t // 8
    row = pl.BlockSpec((tr, h), lambda i: (i, 0))
    vec = pl.BlockSpec((1, h), lambda i: (0, 0))
    return pl.pallas_call(
        _dt_kernel,
        out_shape=(jax.ShapeDtypeStruct((t, h), F32), jax.ShapeDtypeStruct((t, h), F32)),
        grid=(8,),
        in_specs=[row, vec, vec],
        out_specs=(row, row),
        compiler_params=_cparams(("parallel",)),
        name="dt_prep",
    )(dt_raw, bias, a_log)


def _ssd_kernel(x_ref, b_ref, c_ref, px_ref, pb_ref, pc_ref, wx_ref, wb_ref, wc_ref, bx_ref, bb_ref, bc_ref,
                z_ref, dt_ref, dta_ref, dtt_ref, dtat_ref, h0_ref, dskip_ref, nw_ref,
                y_ref, hout_ref, state, halo_x, halo_b, halo_c, *, q, nchunks):
    c = pl.program_id(2)
    gc = SSD_GROUP_CH
    n = SSD_D_STATE
    streams = ((x_ref, px_ref, wx_ref, bx_ref, halo_x), (b_ref, pb_ref, wb_ref, bb_ref, halo_b),
               (c_ref, pc_ref, wc_ref, bc_ref, halo_c))

    @pl.when(c == 0)
    def _():
        for k in range(SSD_GROUPS_PER_STEP):
            state[k] = h0_ref[0, k].T
        for _, prev_ref, _, _, halo in streams:
            halo[...] = prev_ref[0]

    acts = []
    for raw_ref, _, w_ref, bias_ref, halo in streams:
        raw = raw_ref[...]
        acts.append(_conv_silu(raw, halo[...], w_ref[...], bias_ref[...]))
        halo[...] = raw[q - SUBLANES:, :]
    xa, ba, ca = acts

    new_states = []
    for k in range(SSD_GROUPS_PER_STEP):
        cols = slice(k * gc, (k + 1) * gc)
        y, s_new = _ssd_group(
            xa[:, cols], ba[:, k * n:(k + 1) * n], ca[:, k * n:(k + 1) * n], z_ref[:, cols],
            dt_ref[k, 0], dta_ref[k, 0], dtt_ref[k, 0], dtat_ref[k, 0], state[k],
            dskip_ref[:, cols], nw_ref[:, cols], q=q)
        y_ref[:, cols] = y
        state[k] = s_new
        new_states.append(s_new)

    @pl.when(c == nchunks - 1)
    def _():
        for k in range(SSD_GROUPS_PER_STEP):
            hout_ref[0, k] = new_states[k].T


def _ssd_group(x, bm, cm, z, dt, dta, dtt, dtat, s_old, dskip, nw, *, q):
    hg = SSD_HEADS_PER_GROUP
    p = SSD_HEAD_DIM
    xb = x.astype(BF16)
    bm = bm.astype(BF16)
    cm = cm.astype(BF16)

    ri = lax.broadcasted_iota(I32, (q, q), 0)
    ci = lax.broadcasted_iota(I32, (q, q), 1)
    causal = ri >= ci
    lower = jnp.where(causal, 1.0, 0.0).astype(BF16)
    upper = jnp.where(ri <= ci, 1.0, 0.0).astype(BF16)
    acum = _dot_sel(lower, dta)
    acumt = _dot_sel_r(dtat, upper)
    alast = acum[q - 1:q, :]

    cb = _dot_nt(cm, bm)
    lane = lax.broadcasted_iota(I32, (q, 2 * p), 1)
    y_pairs = []
    for hp in range(hg // 2):
        ms = []
        for h in (2 * hp, 2 * hp + 1):
            seg = acum[:, h:h + 1] - acumt[h:h + 1, :]
            decay = jnp.exp(jnp.where(causal, seg, -jnp.inf))
            ms.append((cb * decay * dtt[h:h + 1, :]).astype(BF16))
        lhs = jnp.concatenate(ms, axis=1)
        xp = xb[:, 2 * p * hp:2 * p * (hp + 1)]
        zero = jnp.zeros_like(xp)
        rhs = jnp.concatenate([jnp.where(lane < p, xp, zero), jnp.where(lane >= p, xp, zero)], axis=0)
        y_pairs.append(_dot(lhs, rhs))
    y = jnp.concatenate(y_pairs, axis=1)

    hi = lax.broadcasted_iota(I32, (hg, hg * p), 0)
    li = lax.broadcasted_iota(I32, (hg, hg * p), 1)
    widen = jnp.where((li >= hi * p) & (li < (hi + 1) * p), 1.0, 0.0).astype(BF16)
    scales = jnp.concatenate([jnp.exp(acum), jnp.exp(alast - acum) * dt,
                              jnp.broadcast_to(jnp.exp(alast), (SUBLANES, hg))], axis=0)
    wide = _dot_sel_r2(scales, widen)
    e_in = wide[:q]
    e_out = wide[q:2 * q]
    e_all = wide[2 * q:2 * q + 1]

    y = y + _dot(cm, s_old.astype(BF16)) * e_in
    xw = (x * e_out).astype(BF16)
    s_new = s_old * e_all + _dot_tn(bm, xw)

    y = y + dskip * x
    y = y * _silu(z)
    y = y * lax.rsqrt(jnp.mean(y * y, axis=-1, keepdims=True) + RMS_EPS)
    return (y * nw).astype(BF16), s_new


def ssd_scan(xbc, conv_prev, conv_w, conv_b, z, dt4, dta4, dtt4, dtat4, h0, dskip, norm_w, *,
             row0, nseq, seqlen, q, out_alias=None):
    t = xbc.shape[0]
    gps = SSD_GROUPS_PER_STEP
    g = SSD_N_GROUPS
    gc = gps * SSD_GROUP_CH
    n = SSD_D_STATE
    nchunks = seqlen // q
    rb0 = row0 // q
    bcol0 = SSD_D_INNER // (gps * n)
    ccol0 = (SSD_D_INNER + SSD_BC_DIM) // (gps * n)

    def rows(b, gg, c):
        return rb0 + b * nchunks + c

    def col_specs(shape_of, index_of):
        return [pl.BlockSpec(shape_of(gc), index_of(0)), pl.BlockSpec(shape_of(gps * n), index_of(bcol0)),
                pl.BlockSpec(shape_of(gps * n), index_of(ccol0))]

    in_specs = (col_specs(lambda w: (q, w), lambda c0: lambda b, gg, c: (rows(b, gg, c), c0 + gg))
                + col_specs(lambda w: (1, SUBLANES, w), lambda c0: lambda b, gg, c: (b, 0, c0 + gg))
                + col_specs(lambda w: (SSD_CONV_W, w), lambda c0: lambda b, gg, c: (0, c0 + gg))
                + col_specs(lambda w: (1, w), lambda c0: lambda b, gg, c: (0, c0 + gg)))
    in_specs += [pl.BlockSpec((q, gc), lambda b, gg, c: (rows(b, gg, c), gg)),
                pl.BlockSpec((gps, 1, q, 16), lambda b, gg, c: (gg, b * nchunks + c, 0, 0)),
                pl.BlockSpec((gps, 1, q, 16), lambda b, gg, c: (gg, b * nchunks + c, 0, 0)),
                pl.BlockSpec((gps, 1, 16, q), lambda b, gg, c: (gg, b * nchunks + c, 0, 0)),
                pl.BlockSpec((gps, 1, 16, q), lambda b, gg, c: (gg, b * nchunks + c, 0, 0)),
                pl.BlockSpec((1, gps, SSD_GROUP_CH, n), lambda b, gg, c: (b, gg, 0, 0)),
                pl.BlockSpec((1, gc), lambda b, gg, c: (0, gg)),
                pl.BlockSpec((1, gc), lambda b, gg, c: (0, gg))]
    args = [xbc] * 3 + [conv_prev] * 3 + [conv_w] * 3 + [conv_b] * 3 + [z, dt4, dta4, dtt4, dtat4, h0, dskip, norm_w]
    nin = len(args)
    kern = functools.partial(_ssd_kernel, q=q, nchunks=nchunks)
    aliases = {}
    if out_alias is not None:
        in_specs.append(pl.BlockSpec(memory_space=pl.ANY))
        args.append(out_alias)
        aliases = {nin: 0}
        base = kern
        kern = lambda *refs: base(*refs[:nin], *refs[nin + 1:])
    return pl.pallas_call(
        kern,
        out_shape=(jax.ShapeDtypeStruct((t, SSD_D_INNER), BF16),
                   jax.ShapeDtypeStruct((nseq, g, SSD_GROUP_CH, n), F32)),
        grid=(nseq, g // gps, nchunks),
        in_specs=in_specs,
        out_specs=(pl.BlockSpec((q, gc), lambda b, gg, c: (rows(b, gg, c), gg)),
                   pl.BlockSpec((1, gps, SSD_GROUP_CH, n), lambda b, gg, c: (b, gg, 0, 0))),
        scratch_shapes=[pltpu.VMEM((gps, n, SSD_GROUP_CH), F32), pltpu.VMEM((SUBLANES, gc), F32),
                        pltpu.VMEM((SUBLANES, gps * n), F32), pltpu.VMEM((SUBLANES, gps * n), F32)],
        input_output_aliases=aliases,
        compiler_params=_cparams(("parallel", "parallel", "arbitrary")),
        name="ssd_scan",
    )(*args)


def _dt_layouts(v, row0, nseq, seqlen, q):
    nb = nseq * seqlen // q
    part = v[row0:row0 + nseq * seqlen].reshape(nb, q, SSD_N_GROUPS, SSD_HEADS_PER_GROUP)
    return part.transpose(2, 0, 1, 3), part.transpose(2, 0, 3, 1)


def _s5_prep_kernel(lr_ref, li_ref, ls_ref, bre_ref, bim_ref, cre_ref, cim_ref, wst_ref, vt_ref, krev_ref, aq_ref):
    lr = lr_ref[0]
    li = li_ref[0]
    dstep = jnp.exp(ls_ref[0])
    mag = jnp.exp(lr * dstep)
    ar = mag * jnp.cos(li * dstep)
    ai = mag * jnp.sin(li * dstep)
    den = lr * lr + li * li
    cr = ((ar - 1.0) * lr + ai * li) / den
    ci = (ai * lr - (ar - 1.0) * li) / den
    bre = bre_ref[0]
    bim = bim_ref[0]
    bbr = cr * bre - ci * bim
    bbi = cr * bim + ci * bre
    bmat = jnp.concatenate([bbr, bbi], axis=1)
    cre = cre_ref[0]
    cim = cim_ref[0]

    def power(m):
        e = jnp.exp((m * lr) * dstep)
        return e * jnp.cos((m * li) * dstep), e * jnp.sin((m * li) * dstep)

    powers = [power(float(m)) for m in range(S5_Q + 1)]

    def out_rows(m):
        pr, pi = powers[m]
        return jnp.concatenate([cre * pr - cim * pi, -(cre * pi + cim * pr)], axis=1)

    rows = [out_rows(m) for m in range(S5_Q + 1)]
    for t in range(S5_Q):
        pr, pi = powers[S5_Q - 1 - t]
        blk = jnp.concatenate([pr * bbr - pi * bbi, pr * bbi + pi * bbr], axis=1)
        wst_ref[0, t * LANES:(t + 1) * LANES, :] = blk.astype(BF16)
        vt_ref[0, t * LANES:(t + 1) * LANES, :] = rows[t + 1].astype(BF16)
    taps = jnp.concatenate([rows[S5_Q - 1 - t] for t in range(S5_Q)], axis=0)
    kk = lax.dot_general(bmat, taps, (((1,), (1,)), ((), ())),
                         precision=lax.Precision.HIGHEST, preferred_element_type=F32)
    for t in range(S5_Q):
        krev_ref[0, t * LANES:(t + 1) * LANES, :] = kk[:, t * LANES:(t + 1) * LANES].astype(BF16)
    pr, pi = powers[S5_Q]
    aq_ref[0] = jnp.concatenate([pr, pi], axis=1)


def s5_prep(lr, li, ls, bre, bim, cre, cim):
    np_ = S5_N_PACKS
    sp = S5_PACK_STATE
    vec = pl.BlockSpec((1, 1, sp), lambda i: (i, 0, 0))
    mat = pl.BlockSpec((1, LANES, sp), lambda i: (i, 0, 0))
    big = pl.BlockSpec((1, S5_Q * LANES, 2 * sp), lambda i: (i, 0, 0))
    return pl.pallas_call(
        _s5_prep_kernel,
        out_shape=(jax.ShapeDtypeStruct((np_, S5_Q * LANES, 2 * sp), BF16),
                   jax.ShapeDtypeStruct((np_, S5_Q * LANES, 2 * sp), BF16),
                   jax.ShapeDtypeStruct((np_, S5_Q * LANES, LANES), BF16),
                   jax.ShapeDtypeStruct((np_, 1, 2 * sp), F32)),
        grid=(np_,),
        in_specs=[vec, vec, vec, mat, mat, mat, mat],
        out_specs=(big, big, pl.BlockSpec((1, S5_Q * LANES, LANES), lambda i: (i, 0, 0)),
                   pl.BlockSpec((1, 1, 2 * sp), lambda i: (i, 0, 0))),
        compiler_params=_cparams(("parallel",)),
        name="s5_prep",
    )(lr, li, ls, bre, bim, cre, cim)


def _s5_kernel(x_ref, wst_ref, vt_ref, krev_ref, aq_ref, d_ref, h0_ref, g_ref, hout_ref, *scratch,
               nblk, nseq, chain):
    q = S5_Q
    sp = S5_PACK_STATE
    us = [x_ref[pl.ds(t, nblk, stride=q), :] for t in range(q)]
    ucat = jnp.concatenate([u.astype(BF16) for u in us], axis=1)
    s_in = _dot(ucat, wst_ref[0])
    aq = aq_ref[0]
    aqr, aqi = aq[:, :sp], aq[:, sp:]

    def advance(h, s):
        hr, hi = h[:, :sp], h[:, sp:]
        return jnp.concatenate([aqr * hr - aqi * hi + s[:, :sp], aqr * hi + aqi * hr + s[:, sp:]], axis=1)

    tmat = scratch[-1]

    @pl.when(pl.program_id(1) == 0)
    def _():
        for s in range(q):
            for t in range(s, q):
                tmat[s * LANES:(s + 1) * LANES, t * LANES:(t + 1) * LANES] = (
                    krev_ref[0, (q - 1 - (t - s)) * LANES:(q - (t - s)) * LANES, :])
        for p in range(q // 2):
            tmat[(2 * p + 1) * LANES:(2 * p + 2) * LANES, 2 * p * LANES:(2 * p + 1) * LANES] = (
                jnp.zeros((LANES, LANES), BF16))

    if chain:
        hin_s, s_s, y_s = scratch[:3]
        per_seq = nblk // nseq
        s_s[...] = s_in

        def body(c, hs):
            out = []
            for s in range(nseq):
                row = s * per_seq + c
                hin_s[pl.ds(row, 1), :] = hs[s]
                out.append(advance(hs[s], s_s[pl.ds(row, 1), :]))
            return tuple(out)

        h0 = h0_ref[0, 0]
        hs = lax.fori_loop(0, per_seq, body, tuple(h0[s:s + 1] for s in range(nseq)), unroll=2)
        hout_ref[0, 0] = jnp.concatenate(hs, axis=0)
        hin = hin_s[...]
    else:
        y_s = scratch[0]
        hin = h0_ref[0, 0]
        hout_ref[0, 0] = advance(hin, s_in)
    hinb = hin.astype(BF16)
    d = d_ref[...]
    for p in range(q // 2):
        c0, c1 = 2 * p * LANES, (2 * p + 2) * LANES
        y2 = _dot_nt(hinb, vt_ref[0, c0:c1, :])
        y2 = y2 + _dot(ucat[:, :c1], tmat[:c1, c0:c1])
        for j in range(2):
            t = 2 * p + j
            y = y2[:, j * LANES:(j + 1) * LANES] + d * us[t]
            y_s[pl.ds(t, nblk, stride=q), :] = _gelu_exact(y)
    g_ref[...] = y_s[...].astype(BF16)


def s5_scan(x, wst, vt, krev, aq, dskip, h0, *, row0, nseq, seqlen, chain, out_alias=None):
    t, d = x.shape
    q = S5_Q
    sp2 = 2 * S5_PACK_STATE
    spb = S5_CHAIN_SEQS if chain else nseq
    nstep = nseq // spb
    rows = spb * seqlen
    nblk = rows // q
    assert chain or seqlen == q
    grid = (S5_N_PACKS, nstep)
    rb0 = row0 // rows
    xmap = lambda p, b: (rb0 + b, p)
    wmap = lambda p, b: (p, 0, 0)
    dmap = lambda p, b: (0, p)
    hspec = pl.BlockSpec((1, 1, spb, sp2), lambda p, b: (p, b, 0, 0))
    hshape = (S5_N_PACKS, nstep, spb, sp2)
    scratch = [pltpu.VMEM((rows, LANES), F32), pltpu.VMEM((q * LANES, q * LANES), BF16)]
    if chain:
        scratch = [pltpu.VMEM((nblk, sp2), F32), pltpu.VMEM((nblk, sp2), F32)] + scratch
    sem = ("parallel", "arbitrary")
    in_specs = [pl.BlockSpec((rows, LANES), xmap),
                pl.BlockSpec((1, q * LANES, sp2), wmap),
                pl.BlockSpec((1, q * LANES, sp2), wmap),
                pl.BlockSpec((1, q * LANES, LANES), wmap),
                pl.BlockSpec((1, 1, sp2), wmap),
                pl.BlockSpec((1, LANES), dmap),
                hspec]
    args = [x, wst, vt, krev, aq, dskip, h0]
    kern = functools.partial(_s5_kernel, nblk=nblk, nseq=spb, chain=chain)
    aliases = {}
    if out_alias is not None:
        in_specs.append(pl.BlockSpec(memory_space=pl.ANY))
        args.append(out_alias)
        aliases = {len(args) - 1: 0}
        base = kern
        kern = lambda *refs: base(*refs[:7], *refs[8:])
    return pl.pallas_call(
        kern,
        out_shape=(jax.ShapeDtypeStruct((t, d), BF16), jax.ShapeDtypeStruct(hshape, F32)),
        grid=grid,
        in_specs=in_specs,
        out_specs=(pl.BlockSpec((rows, LANES), xmap), hspec),
        scratch_shapes=scratch,
        input_output_aliases=aliases,
        compiler_params=_cparams(sem),
        name="s5_scan",
    )(*args)


def _s5_block_diag(m):
    m = m.reshape(S5_N_PACKS, S5_PACK_GROUPS, S5_GROUP_CH, S5_STATE)
    eye = jnp.eye(S5_PACK_GROUPS, dtype=m.dtype)
    out = m[:, :, :, None, :] * eye[None, :, None, :, None]
    return out.reshape(S5_N_PACKS, S5_PACK_GROUPS * S5_GROUP_CH, S5_PACK_STATE)


def _s5_pack_state(re, im):
    b = re.shape[0]
    return jnp.concatenate([re.reshape(b, S5_N_PACKS, S5_PACK_STATE), im.reshape(b, S5_N_PACKS, S5_PACK_STATE)], axis=-1)


def _s5_unpack_state(h):
    b = h.shape[0]
    return (h[..., :S5_PACK_STATE].reshape(b, S5_N_GROUPS, S5_STATE),
            h[..., S5_PACK_STATE:].reshape(b, S5_N_GROUPS, S5_STATE))


def kernel(x_prompt, x_sample, cache_ssd_conv, state_ssd, state_s5_re, state_s5_im, ssd_w_in, ssd_conv_w, ssd_conv_b, ssd_dt_bias, ssd_a_log, ssd_d, ssd_norm_w, ssd_w_out, s5_lambda_re, s5_lambda_im, s5_log_step, s5_b_re, s5_b_im, s5_c_re, s5_c_im, s5_d, s5_w_glu, ln_mix_g, ln_mix_b, ln_ffn_g, ln_ffn_b, router_w, router_b, moe_w_gate, moe_w_up, moe_w_down):
    d = D_MODEL
    x0_parts = [x_prompt.reshape(T_PROMPT, d), x_sample.reshape(T_SAMPLE, d)]
    x0b = cast_rows(x0_parts, BF16, tm=T_SAMPLE)
    rw = jnp.zeros((d, LANES), BF16).at[:, :N_EXPERTS].set(router_w.astype(BF16))
    rb = jnp.zeros((1, LANES), F32).at[0, :N_EXPERTS].set(router_b)

    w_in = ssd_w_in[0].astype(BF16)
    z = matmul(x0b, w_in, tm=MM_TM, tn=1024, col0=0, ncols=SSD_D_INNER)
    xbc = matmul(x0b, w_in, tm=MM_TM, tn=1024, col0=SSD_D_INNER, ncols=SSD_CONV_DIM)
    dt_raw = matmul(x0b, w_in, tm=MM_TM, tn=LANES, col0=SSD_D_INNER + SSD_CONV_DIM, ncols=SSD_N_HEADS)

    conv_w = ssd_conv_w[0]
    conv_b = ssd_conv_b[0].reshape(1, SSD_CONV_DIM)
    prev_p = jnp.zeros((BATCH, SUBLANES, SSD_CONV_DIM), F32)
    prev_s = jnp.concatenate([jnp.zeros((DEC_BATCH, SUBLANES - (SSD_CONV_W - 1), SSD_CONV_DIM), F32),
                              cache_ssd_conv[0]], axis=1)

    dt_sp, dta = dt_prep(dt_raw, ssd_dt_bias[0].reshape(1, SSD_N_HEADS), ssd_a_log[0].reshape(1, SSD_N_HEADS))
    dskip = jnp.repeat(ssd_d[0], SSD_HEAD_DIM).reshape(1, SSD_D_INNER)
    norm_w = ssd_norm_w[0].reshape(1, SSD_D_INNER)
    h0_p = jnp.zeros((BATCH, SSD_N_GROUPS, SSD_GROUP_CH, SSD_D_STATE), F32)
    h0_s = state_ssd[0].reshape(DEC_BATCH, SSD_N_GROUPS, SSD_GROUP_CH, SSD_D_STATE)
    lay_p = _dt_layouts(dt_sp, 0, BATCH, SEQ, SSD_Q_PROMPT) + _dt_layouts(dta, 0, BATCH, SEQ, SSD_Q_PROMPT)
    lay_s = (_dt_layouts(dt_sp, T_PROMPT, DEC_BATCH, DEC_SEQ, DEC_SEQ)
             + _dt_layouts(dta, T_PROMPT, DEC_BATCH, DEC_SEQ, DEC_SEQ))
    ymix, ssd_p = ssd_scan(xbc, prev_p, conv_w, conv_b, z, lay_p[0], lay_p[2], lay_p[1], lay_p[3], h0_p, dskip,
                           norm_w, row0=0, nseq=BATCH, seqlen=SEQ, q=SSD_Q_PROMPT)
    ymix, ssd_s = ssd_scan(xbc, prev_s, conv_w, conv_b, z, lay_s[0], lay_s[2], lay_s[1], lay_s[3], h0_s, dskip,
                           norm_w, row0=T_PROMPT, nseq=DEC_BATCH, seqlen=DEC_SEQ, q=DEC_SEQ, out_alias=ymix)
    mix = matmul(ymix, ssd_w_out[0].astype(BF16), tm=MM_TM, tn=512)

    wg = cast_weights(moe_w_gate, BF16, rows_per_block=D_MODEL // 2)
    wu = cast_weights(moe_w_up, BF16, rows_per_block=D_MODEL // 2)
    wd = moe_w_down
    x1, idx, gate = ln_router(x0_parts, mix, ln_mix_g[0].reshape(1, d), ln_mix_b[0].reshape(1, d), rw, rb)
    pos, y_sorted = moe_layer(x1, idx, wg, wu, wd, 0)
    (x2,) = combine_ln(pos, x1, gate, y_sorted, ln_ffn_g[0].reshape(1, d), ln_ffn_b[0].reshape(1, d))

    lr = s5_lambda_re[0].reshape(S5_N_PACKS, 1, S5_PACK_STATE)
    li = s5_lambda_im[0].reshape(S5_N_PACKS, 1, S5_PACK_STATE)
    ls = jnp.repeat(s5_log_step[0], S5_STATE).reshape(S5_N_PACKS, 1, S5_PACK_STATE)
    wst, vt, krev, aq = s5_prep(lr, li, ls,
                                _s5_block_diag(s5_b_re[0].transpose(0, 2, 1)),
                                _s5_block_diag(s5_b_im[0].transpose(0, 2, 1)),
                                _s5_block_diag(s5_c_re[0]), _s5_block_diag(s5_c_im[0]))
    s5_dskip = s5_d[0].reshape(1, d)
    sp2 = 2 * S5_PACK_STATE
    hs_p = jnp.zeros((S5_N_PACKS, BATCH // S5_CHAIN_SEQS, S5_CHAIN_SEQS, sp2), F32)
    hs_s = _s5_pack_state(state_s5_re[0], state_s5_im[0]).transpose(1, 0, 2)[:, None]
    gact, s5_p = s5_scan(x2, wst, vt, krev, aq, s5_dskip, hs_p, row0=0, nseq=BATCH, seqlen=SEQ, chain=True)
    gact, s5_s = s5_scan(x2, wst, vt, krev, aq, s5_dskip, hs_s, row0=T_PROMPT, nseq=DEC_BATCH, seqlen=DEC_SEQ,
                         chain=False, out_alias=gact)
    mix = glu_matmul(gact, s5_w_glu[0].astype(BF16), tm=MM_TM, tn=512)

    x3, idx, gate = ln_router([x2], mix, ln_mix_g[1].reshape(1, d), ln_mix_b[1].reshape(1, d), rw, rb)
    pos, y_sorted = moe_layer(x3, idx, wg, wu, wd, 1)
    g1 = ln_ffn_g[1].reshape(1, d)
    b1 = ln_ffn_b[1].reshape(1, d)
    (y_p,) = combine_ln(pos, x3, gate, y_sorted, g1, b1, row0=0, nrows=T_PROMPT)
    (y_s,) = combine_ln(pos, x3, gate, y_sorted, g1, b1, row0=T_PROMPT, nrows=T_SAMPLE)

    keep = SSD_CONV_W - 1
    conv_p = jnp.stack([xbc[(b + 1) * SEQ - keep:(b + 1) * SEQ] for b in range(BATCH)])[None]
    conv_s = xbc[T_PROMPT:].reshape(DEC_BATCH, DEC_SEQ, SSD_CONV_DIM)[:, DEC_SEQ - keep:][None]
    ssd_state_p = ssd_p.reshape(1, BATCH, SSD_N_HEADS, SSD_HEAD_DIM, SSD_D_STATE)
    ssd_state_s = ssd_s.reshape(1, DEC_BATCH, SSD_N_HEADS, SSD_HEAD_DIM, SSD_D_STATE)
    re_p, im_p = _s5_unpack_state(s5_p.reshape(S5_N_PACKS, BATCH, sp2).transpose(1, 0, 2))
    re_s, im_s = _s5_unpack_state(s5_s[:, 0].transpose(1, 0, 2))
    return (y_p.reshape(BATCH, SEQ, d), y_s.reshape(DEC_BATCH, DEC_SEQ, d),
            conv_p, ssd_state_p, re_p[None], im_p[None],
            conv_s, ssd_state_s, re_s[None], im_s[None])
```

```python
import functools
import math

import jax
import jax.numpy as jnp
from jax import lax
from jax.experimental import pallas as pl
from jax.experimental.pallas import tpu as pltpu

F32 = jnp.float32
BF16 = jnp.bfloat16
I32 = jnp.int32

D_MODEL = 4096
BATCH, SEQ = 4, 4096
DEC_BATCH, DEC_SEQ = 32, 16
T_PROMPT = BATCH * SEQ
T_SAMPLE = DEC_BATCH * DEC_SEQ
T_ALL = T_PROMPT + T_SAMPLE
SSD_D_INNER = 8192
SSD_HEAD_DIM = 64
SSD_N_HEADS = 128
SSD_N_GROUPS = 8
SSD_HEADS_PER_GROUP = 16
SSD_D_STATE = 128
SSD_GROUP_CH = SSD_D_INNER // SSD_N_GROUPS
SSD_BC_DIM = SSD_N_GROUPS * SSD_D_STATE
SSD_CONV_DIM = SSD_D_INNER + 2 * SSD_BC_DIM
SSD_CONV_W = 4
S5_GROUP_CH = 16
S5_N_GROUPS = 256
S5_STATE = 64
S5_PACK_GROUPS = 8
S5_N_PACKS = S5_N_GROUPS // S5_PACK_GROUPS
S5_PACK_STATE = S5_PACK_GROUPS * S5_STATE
S5_Q = 16
S5_CHAIN_SEQS = 2
N_EXPERTS = 16
EXPERTS_PER_GROUP = 4
D_EXPERT = 1024
DEPTH = 2
DEEPNORM_ALPHA = (2 * DEPTH) ** 0.25
LN_EPS = 1e-5
RMS_EPS = 1e-5

LANES = 128
SUBLANES = 8
VMEM_LIMIT_BYTES = 56 * 1024 * 1024

MM_TM = 768
MOE_TM = 256
MOE_ROWS = 2 * T_ALL + N_EXPERTS * MOE_TM
MOE_TILES = MOE_ROWS // MOE_TM
LN_TM = 128
SSD_Q_PROMPT = 128
SSD_GROUPS_PER_STEP = 4


def _cparams(semantics):
    return pltpu.CompilerParams(dimension_semantics=semantics, vmem_limit_bytes=VMEM_LIMIT_BYTES)


def _dot(a, b):
    return jnp.dot(a, b, preferred_element_type=F32)


def _dot_nt(a, b):
    return lax.dot_general(a, b, (((1,), (1,)), ((), ())), preferred_element_type=F32)


def _dot_tn(a, b):
    return lax.dot_general(a, b, (((0,), (0,)), ((), ())), preferred_element_type=F32)


def _split3(v):
    hi = v.astype(BF16)
    r = v - hi.astype(F32)
    mid = r.astype(BF16)
    lo = (r - mid.astype(F32)).astype(BF16)
    return hi, mid, lo


def _dot_sel(sel, v):
    hi, mid, lo = _split3(v)
    return _dot(sel, hi) + _dot(sel, mid) + _dot(sel, lo)


def _dot_sel_r(v, sel):
    hi, mid, lo = _split3(v)
    return _dot(hi, sel) + _dot(mid, sel) + _dot(lo, sel)


def _dot_sel_r2(v, sel):
    hi = v.astype(BF16)
    lo = (v - hi.astype(F32)).astype(BF16)
    return _dot(hi, sel) + _dot(lo, sel)


def _silu(x):
    return x * jax.nn.sigmoid(x)


def _gelu_exact(x):
    return 0.5 * x * (1.0 + lax.erf(x * (1.0 / math.sqrt(2.0))))


def _mm_kernel(x_ref, w_ref, o_ref):
    o_ref[...] = _dot(x_ref[...], w_ref[...]).astype(o_ref.dtype)


def matmul(x, w, *, tm, tn, col0=0, ncols=None, out_dtype=F32):
    m, k = x.shape
    n = w.shape[1] if ncols is None else ncols
    c0 = col0 // tn
    return pl.pallas_call(
        _mm_kernel,
        out_shape=jax.ShapeDtypeStruct((m, n), out_dtype),
        grid=(n // tn, m // tm),
        in_specs=[pl.BlockSpec((tm, k), lambda j, i: (i, 0)),
                  pl.BlockSpec((k, tn), lambda j, i: (0, c0 + j))],
        out_specs=pl.BlockSpec((tm, tn), lambda j, i: (i, j)),
        compiler_params=_cparams(("parallel", "parallel")),
        name="matmul",
    )(x, w)


def _glu_kernel(x_ref, w1_ref, w2_ref, o_ref):
    x = x_ref[...]
    z1 = _dot(x, w1_ref[...])
    z2 = _dot(x, w2_ref[...])
    o_ref[...] = z1 * jax.nn.sigmoid(z2)


def glu_matmul(x, w, *, tm, tn):
    m, k = x.shape
    n = w.shape[1] // 2
    half = n // tn
    return pl.pallas_call(
        _glu_kernel,
        out_shape=jax.ShapeDtypeStruct((m, n), F32),
        grid=(n // tn, m // tm),
        in_specs=[pl.BlockSpec((tm, k), lambda j, i: (i, 0)),
                  pl.BlockSpec((k, tn), lambda j, i: (0, j)),
                  pl.BlockSpec((k, tn), lambda j, i: (0, half + j))],
        out_specs=pl.BlockSpec((tm, tn), lambda j, i: (i, j)),
        compiler_params=_cparams(("parallel", "parallel")),
        name="glu_matmul",
    )(x, w, w)


def _layer_norm(v, g, b):
    mu = jnp.mean(v, axis=-1, keepdims=True)
    d = v - mu
    var = jnp.mean(d * d, axis=-1, keepdims=True)
    return d * lax.rsqrt(var + LN_EPS) * g + b


def _route_rows(xb, rw, rb):
    tm = xb.shape[0]
    lane = lax.broadcasted_iota(I32, (tm, LANES), 1)
    live = lane < N_EXPERTS
    logits = _dot(xb, rw) + rb
    logits = jnp.where(live, logits, -jnp.inf)
    m = jnp.max(logits, axis=-1, keepdims=True)
    e = jnp.exp(logits - m)
    probs = e / jnp.sum(e, axis=-1, keepdims=True)
    best = None
    for g in range(N_EXPERTS // EXPERTS_PER_GROUP):
        in_g = (lane >= g * EXPERTS_PER_GROUP) & (lane < (g + 1) * EXPERTS_PER_GROUP)
        score = jnp.max(jnp.where(in_g, probs, -1.0), axis=-1, keepdims=True)
        if best is None:
            best, best_g = score, jnp.zeros((tm, 1), I32)
        else:
            upd = score > best
            best = jnp.where(upd, score, best)
            best_g = jnp.where(upd, g, best_g)
    lo = best_g * EXPERTS_PER_GROUP
    in_best = (lane >= lo) & (lane < lo + EXPERTS_PER_GROUP)
    cand = jnp.where(in_best, probs, -1.0)
    p1 = jnp.max(cand, axis=-1, keepdims=True)
    i1 = jnp.min(jnp.where(cand == p1, lane, LANES), axis=-1, keepdims=True)
    cand2 = jnp.where(lane == i1, -2.0, cand)
    p2 = jnp.max(cand2, axis=-1, keepdims=True)
    i2 = jnp.min(jnp.where(cand2 == p2, lane, LANES), axis=-1, keepdims=True)
    tot = p1 + p2
    idx = jnp.where(lane == 0, i1, jnp.where(lane == 1, i2, 0))
    gate = jnp.where(lane == 0, p1 / tot, jnp.where(lane == 1, p2 / tot, 0.0))
    return idx, gate


def _row_part_specs(parts, tm, d):
    specs, firsts, t0 = [], [], 0
    for p in parts:
        nt = p.shape[0] // tm
        specs.append(pl.BlockSpec((tm, d), lambda i, *_, t0=t0, nt=nt: (jnp.clip(i - t0, 0, nt - 1), 0)))
        firsts.append(t0)
        t0 += nt
    return specs, firsts


def _select_part(i, refs, firsts, store):
    for k, ref in enumerate(refs):
        hi = firsts[k + 1] if k + 1 < len(refs) else None
        cond = i >= firsts[k] if hi is None else (i >= firsts[k]) & (i < hi)

        @pl.when(cond)
        def _(ref=ref):
            store(ref[...])


def _ln_router_kernel(*refs, firsts):
    n = len(firsts)
    x_refs = refs[:n]
    mix_ref, g_ref, b_ref, rw_ref, rb_ref, xo_ref, idx_ref, gate_ref = refs[n:]

    def finish(x):
        y = _layer_norm(DEEPNORM_ALPHA * x + mix_ref[...], g_ref[...], b_ref[...])
        xo_ref[...] = y
        idx, gate = _route_rows(y.astype(BF16), rw_ref[...], rb_ref[...])
        idx_ref[...] = idx
        gate_ref[...] = gate

    _select_part(pl.program_id(0), x_refs, firsts, finish)


def ln_router(x_parts, mix, g, b, rw, rb):
    t, d = mix.shape
    tm = LN_TM
    x_specs, firsts = _row_part_specs(x_parts, tm, d)
    row = pl.BlockSpec((tm, d), lambda i: (i, 0))
    vec = pl.BlockSpec((1, d), lambda i: (0, 0))
    small = pl.BlockSpec((tm, LANES), lambda i: (i, 0))
    return pl.pallas_call(
        functools.partial(_ln_router_kernel, firsts=tuple(firsts)),
        out_shape=(jax.ShapeDtypeStruct((t, d), F32),
                   jax.ShapeDtypeStruct((t, LANES), I32), jax.ShapeDtypeStruct((t, LANES), F32)),
        grid=(t // tm,),
        in_specs=x_specs + [row, vec, vec, pl.BlockSpec((d, LANES), lambda i: (0, 0)),
                            pl.BlockSpec((1, LANES), lambda i: (0, 0))],
        out_specs=(row, small, small),
        compiler_params=_cparams(("parallel",)),
        name="ln_router",
    )(*x_parts, mix, g, b, rw, rb)


def _cast_rows_kernel(*refs, firsts):
    o_ref = refs[-1]

    def store(v):
        o_ref[...] = v.astype(o_ref.dtype)

    _select_part(pl.program_id(0), refs[:-1], firsts, store)


def cast_rows(x_parts, dtype, tm):
    d = x_parts[0].shape[1]
    t = sum(p.shape[0] for p in x_parts)
    x_specs, firsts = _row_part_specs(x_parts, tm, d)
    return pl.pallas_call(
        functools.partial(_cast_rows_kernel, firsts=tuple(firsts)),
        out_shape=jax.ShapeDtypeStruct((t, d), dtype),
        grid=(t // tm,),
        in_specs=x_specs,
        out_specs=pl.BlockSpec((tm, d), lambda i: (i, 0)),
        compiler_params=_cparams(("parallel",)),
        name="cast_rows",
    )(*x_parts)


def _cast_kernel(x_ref, o_ref):
    o_ref[...] = x_ref[...].astype(o_ref.dtype)


def cast_weights(w, dtype, rows_per_block):
    nl, ne, r, c = w.shape
    spec = pl.BlockSpec((1, 1, rows_per_block, c), lambda l, e, j: (l, e, j, 0))
    return pl.pallas_call(
        _cast_kernel,
        out_shape=jax.ShapeDtypeStruct(w.shape, dtype),
        grid=(nl, ne, r // rows_per_block),
        in_specs=[spec],
        out_specs=spec,
        compiler_params=_cparams(("parallel", "parallel", "parallel")),
        name="cast_weights",
    )(w)


def _row_gather_start(src_hbm, row, dst, dst_row, sem, priority=0):
    pltpu.make_async_copy(src_hbm.at[pl.ds(row, 1)], dst.at[pl.ds(dst_row, 1)], sem).start(priority=priority)


def _slot_wait(buf, slot, sem):
    pltpu.make_async_copy(buf.at[slot], buf.at[slot], sem.at[slot]).wait()


def _moe_up_kernel(te_ref, src_ref, nv_ref, x_hbm, wg_ref, wu_ref, h_ref, xbuf, sem):
    del te_ref
    i = pl.program_id(0)
    slot = lax.rem(i, 2)
    nvalid = nv_ref[0]

    @pl.when(i == 0)
    def _():
        def body(r, carry):
            _row_gather_start(x_hbm, src_ref[r], xbuf.at[0], r, sem.at[0])
            return carry
        lax.fori_loop(0, MOE_TM, body, 0)

    @pl.when(i <= nvalid)
    def _():
        _slot_wait(xbuf, slot, sem)

    @pl.when(i < nvalid)
    def _():
        base = (i + 1) * MOE_TM
        for r in range(MOE_TM):
            _row_gather_start(x_hbm, src_ref[base + r], xbuf.at[1 - slot], r, sem.at[1 - slot], priority=r % 2)
        xb = xbuf[slot].astype(BF16)
        gate = _dot(xb, wg_ref[0, 0])
        up = _dot(xb, wu_ref[0, 0])
        h_ref[...] = (_silu(gate) * up).astype(BF16)

    @pl.when(i >= nvalid)
    def _():
        h_ref[...] = jnp.zeros_like(h_ref)


def moe_up(tile_expert, src_rows, nvalid, x, wg, wu, layer):
    d = x.shape[1]
    assert (2 * x.shape[0] + N_EXPERTS * (MOE_TM - 1)) // MOE_TM < MOE_TILES
    wspec = pl.BlockSpec((1, 1, d, D_EXPERT), lambda i, te, src, nv: (layer, te[i], 0, 0))
    grid_spec = pltpu.PrefetchScalarGridSpec(
        num_scalar_prefetch=3,
        grid=(MOE_TILES,),
        in_specs=[pl.BlockSpec(memory_space=pl.ANY), wspec, wspec],
        out_specs=pl.BlockSpec((MOE_TM, D_EXPERT), lambda i, te, src, nv: (i, 0)),
        scratch_shapes=[pltpu.VMEM((2, MOE_TM, d), F32), pltpu.SemaphoreType.DMA((2,))],
    )
    return pl.pallas_call(
        _moe_up_kernel,
        out_shape=jax.ShapeDtypeStruct((MOE_ROWS, D_EXPERT), BF16),
        grid_spec=grid_spec,
        compiler_params=_cparams(("arbitrary",)),
        name="moe_up",
    )(tile_expert, src_rows, nvalid, x, wg, wu)


def _moe_down_kernel(te_ref, nv_ref, h_ref, wd_ref, y_ref, wbf):
    i = pl.program_id(0)
    new_expert = (i == 0) | (te_ref[i] != te_ref[jnp.maximum(i - 1, 0)])

    @pl.when(new_expert & (i < nv_ref[0]))
    def _():
        wbf[...] = wd_ref[0, 0].astype(BF16)

    @pl.when(i < nv_ref[0])
    def _():
        y_ref[...] = _dot(h_ref[...], wbf[...])

    @pl.when(i >= nv_ref[0])
    def _():
        y_ref[...] = jnp.zeros_like(y_ref)


def moe_down(tile_expert, nvalid, h, wd, layer):
    d = wd.shape[3]
    grid_spec = pltpu.PrefetchScalarGridSpec(
        num_scalar_prefetch=2,
        grid=(MOE_TILES,),
        in_specs=[pl.BlockSpec((MOE_TM, D_EXPERT), lambda i, te, nv: (i, 0)),
                  pl.BlockSpec((1, 1, D_EXPERT, d), lambda i, te, nv: (layer, te[i], 0, 0))],
        out_specs=pl.BlockSpec((MOE_TM, d), lambda i, te, nv: (i, 0)),
        scratch_shapes=[pltpu.VMEM((D_EXPERT, d), BF16)],
    )
    return pl.pallas_call(
        _moe_down_kernel,
        out_shape=jax.ShapeDtypeStruct((MOE_ROWS, d), F32),
        grid_spec=grid_spec,
        compiler_params=_cparams(("arbitrary",)),
        name="moe_down",
    )(tile_expert, nvalid, h, wd)


def _combine_ln_kernel(pos_ref, x_ref, gsel_ref, y_hbm, g_ref, b_ref, *rest, tm, tile0, ntiles, with_router):
    if with_router:
        rw_ref, rb_ref, xo_ref, xb_ref, idx_ref, gate_ref, ybuf, sem = rest
    else:
        xo_ref, ybuf, sem = rest
    i = pl.program_id(0)
    slot = lax.rem(i, 2)

    @pl.when(i == 0)
    def _():
        def body(r, carry):
            _row_gather_start(y_hbm, pos_ref[2 * tile0 * tm + 2 * r], ybuf.at[0, 0], r, sem.at[0])
            _row_gather_start(y_hbm, pos_ref[2 * tile0 * tm + 2 * r + 1], ybuf.at[0, 1], r, sem.at[0])
            return carry
        lax.fori_loop(0, tm, body, 0)

    _slot_wait(ybuf, slot, sem)
    base = 2 * (tile0 + jnp.minimum(i + 1, ntiles - 1)) * tm
    for r in range(tm):
        _row_gather_start(y_hbm, pos_ref[base + 2 * r], ybuf.at[1 - slot, 0], r, sem.at[1 - slot], priority=0)
        _row_gather_start(y_hbm, pos_ref[base + 2 * r + 1], ybuf.at[1 - slot, 1], r, sem.at[1 - slot], priority=1)
    gsel = gsel_ref[...]
    ffn = gsel[:, 0:1] * ybuf[slot, 0] + gsel[:, 1:2] * ybuf[slot, 1]
    y = _layer_norm(DEEPNORM_ALPHA * x_ref[...] + ffn, g_ref[...], b_ref[...])
    xo_ref[...] = y
    if with_router:
        yb = y.astype(BF16)
        xb_ref[...] = yb
        idx, gate = _route_rows(yb, rw_ref[...], rb_ref[...])
        idx_ref[...] = idx
        gate_ref[...] = gate

    @pl.when(i == ntiles - 1)
    def _():
        _slot_wait(ybuf, 1 - slot, sem)


def combine_ln(pos, x, gate_sel, y_sorted, g, b, *, row0=0, nrows=None, router=None):
    d = x.shape[1]
    tm = LN_TM
    nrows = x.shape[0] if nrows is None else nrows
    tile0 = row0 // tm
    ntiles = nrows // tm
    row_in = pl.BlockSpec((tm, d), lambda i, p: (tile0 + i, 0))
    row_out = pl.BlockSpec((tm, d), lambda i, p: (i, 0))
    vec = pl.BlockSpec((1, d), lambda i, p: (0, 0))
    small = pl.BlockSpec((tm, LANES), lambda i, p: (i, 0))
    in_specs = [row_in, pl.BlockSpec((tm, LANES), lambda i, p: (tile0 + i, 0)),
                pl.BlockSpec(memory_space=pl.ANY), vec, vec]
    args = [pos, x, gate_sel, y_sorted, g, b]
    out_shape = [jax.ShapeDtypeStruct((nrows, d), F32)]
    out_specs = [row_out]
    if router is not None:
        in_specs += [pl.BlockSpec((d, LANES), lambda i, p: (0, 0)), pl.BlockSpec((1, LANES), lambda i, p: (0, 0))]
        args += list(router)
        out_shape += [jax.ShapeDtypeStruct((nrows, d), BF16), jax.ShapeDtypeStruct((nrows, LANES), I32),
                      jax.ShapeDtypeStruct((nrows, LANES), F32)]
        out_specs += [row_out, small, small]
    grid_spec = pltpu.PrefetchScalarGridSpec(
        num_scalar_prefetch=1,
        grid=(ntiles,),
        in_specs=in_specs,
        out_specs=tuple(out_specs),
        scratch_shapes=[pltpu.VMEM((2, 2, tm, d), F32), pltpu.SemaphoreType.DMA((2,))],
    )
    kern = functools.partial(_combine_ln_kernel, tm=tm, tile0=tile0, ntiles=ntiles,
                             with_router=router is not None)
    return pl.pallas_call(
        kern,
        out_shape=tuple(out_shape),
        grid_spec=grid_spec,
        compiler_params=_cparams(("arbitrary",)),
        name="combine_ln",
    )(*args)


def _route_tables(idx):
    t = idx.shape[0]
    e_flat = idx[:, :2].reshape(-1)
    onehot = (e_flat[:, None] == jnp.arange(N_EXPERTS, dtype=I32)[None, :]).astype(I32)
    csum = jnp.cumsum(onehot, axis=0)
    rank = jnp.take_along_axis(csum, e_flat[:, None], axis=1)[:, 0] - 1
    counts = csum[-1]
    padded = ((counts + MOE_TM - 1) // MOE_TM) * MOE_TM
    ends = jnp.cumsum(padded)
    starts = ends - padded
    pos = (starts[e_flat] + rank).astype(I32)
    src = jnp.zeros((MOE_ROWS,), I32).at[pos].set(jnp.arange(2 * t, dtype=I32) // 2)
    nvalid = (ends[-1] // MOE_TM).astype(I32)
    tile_start = jnp.arange(MOE_TILES, dtype=I32) * MOE_TM
    te = jnp.sum((tile_start[:, None] >= ends[None, :]).astype(I32), axis=1)
    te = jnp.minimum(te, te[nvalid - 1]).astype(I32)
    return pos, src, te, nvalid.reshape(1)


def moe_layer(x_f32, idx, wg, wu, wd, layer):
    pos, src, te, nvalid = _route_tables(idx)
    h = moe_up(te, src, nvalid, x_f32, wg, wu, layer)
    y_sorted = moe_down(te, nvalid, h, wd, layer)
    return pos, y_sorted


def _conv_silu(x, prev8, w, b):
    row8 = lax.broadcasted_iota(I32, prev8.shape, 0)
    acc = jnp.broadcast_to(b, x.shape)
    for k in range(SSD_CONV_W):
        s = SSD_CONV_W - 1 - k
        if s == 0:
            xs = x
        else:
            xs = pltpu.roll(x, s, axis=0)
            top = jnp.where(row8 < s, pltpu.roll(prev8, s, axis=0), xs[:SUBLANES])
            xs = jnp.concatenate([top, xs[SUBLANES:]], axis=0)
        acc = acc + xs * w[k:k + 1, :]
    return _silu(acc)


def _dt_kernel(dt_ref, bias_ref, alog_ref, dt_o, dta_o):
    v = dt_ref[...] + bias_ref[...]
    sp = jnp.maximum(v, 0.0) + jnp.log1p(jnp.exp(-jnp.abs(v)))
    dt_o[...] = sp
    dta_o[...] = sp * (-jnp.exp(alog_ref[...]))


def dt_prep(dt_raw, bias, a_log):
    t, h = dt_raw.shape
    tr = t // 8
    row = pl.BlockSpec((tr, h), lambda i: (i, 0))
    vec = pl.BlockSpec((1, h), lambda i: (0, 0))
    return pl.pallas_call(
        _dt_kernel,
        out_shape=(jax.ShapeDtypeStruct((t, h), F32), jax.ShapeDtypeStruct((t, h), F32)),
        grid=(8,),
        in_specs=[row, vec, vec],
        out_specs=(row, row),
        compiler_params=_cparams(("parallel",)),
        name="dt_prep",
    )(dt_raw, bias, a_log)


def _ssd_kernel(x_ref, b_ref, c_ref, px_ref, pb_ref, pc_ref, wx_ref, wb_ref, wc_ref, bx_ref, bb_ref, bc_ref,
                z_ref, dt_ref, dta_ref, dtt_ref, dtat_ref, h0_ref, dskip_ref, nw_ref,
                y_ref, hout_ref, state, halo_x, halo_b, halo_c, *, q, nchunks):
    c = pl.program_id(2)
    gc = SSD_GROUP_CH
    n = SSD_D_STATE
    streams = ((x_ref, px_ref, wx_ref, bx_ref, halo_x), (b_ref, pb_ref, wb_ref, bb_ref, halo_b),
               (c_ref, pc_ref, wc_ref, bc_ref, halo_c))

    @pl.when(c == 0)
    def _():
        for k in range(SSD_GROUPS_PER_STEP):
            state[k] = h0_ref[0, k].T
        for _, prev_ref, _, _, halo in streams:
            halo[...] = prev_ref[0]

    acts = []
    for raw_ref, _, w_ref, bias_ref, halo in streams:
        raw = raw_ref[...]
        acts.append(_conv_silu(raw, halo[...], w_ref[...], bias_ref[...]))
        halo[...] = raw[q - SUBLANES:, :]
    xa, ba, ca = acts

    new_states = []
    for k in range(SSD_GROUPS_PER_STEP):
        cols = slice(k * gc, (k + 1) * gc)
        y, s_new = _ssd_group(
            xa[:, cols], ba[:, k * n:(k + 1) * n], ca[:, k * n:(k + 1) * n], z_ref[:, cols],
            dt_ref[k, 0], dta_ref[k, 0], dtt_ref[k, 0], dtat_ref[k, 0], state[k],
            dskip_ref[:, cols], nw_ref[:, cols], q=q)
        y_ref[:, cols] = y
        state[k] = s_new
        new_states.append(s_new)

    @pl.when(c == nchunks - 1)
    def _():
        for k in range(SSD_GROUPS_PER_STEP):
            hout_ref[0, k] = new_states[k].T


def _ssd_group(x, bm, cm, z, dt, dta, dtt, dtat, s_old, dskip, nw, *, q):
    hg = SSD_HEADS_PER_GROUP
    p = SSD_HEAD_DIM
    xb = x.astype(BF16)
    bm = bm.astype(BF16)
    cm = cm.astype(BF16)

    ri = lax.broadcasted_iota(I32, (q, q), 0)
    ci = lax.broadcasted_iota(I32, (q, q), 1)
    causal = ri >= ci
    lower = jnp.where(causal, 1.0, 0.0).astype(BF16)
    upper = jnp.where(ri <= ci, 1.0, 0.0).astype(BF16)
    acum = _dot_sel(lower, dta)
    acumt = _dot_sel_r(dtat, upper)
    alast = acum[q - 1:q, :]

    cb = _dot_nt(cm, bm)
    lane = lax.broadcasted_iota(I32, (q, 2 * p), 1)
    y_pairs = []
    for hp in range(hg // 2):
        ms = []
        for h in (2 * hp, 2 * hp + 1):
            seg = acum[:, h:h + 1] - acumt[h:h + 1, :]
            decay = jnp.exp(jnp.where(causal, seg, -jnp.inf))
            ms.append((cb * decay * dtt[h:h + 1, :]).astype(BF16))
        lhs = jnp.concatenate(ms, axis=1)
        xp = xb[:, 2 * p * hp:2 * p * (hp + 1)]
        zero = jnp.zeros_like(xp)
        rhs = jnp.concatenate([jnp.where(lane < p, xp, zero), jnp.where(lane >= p, xp, zero)], axis=0)
        y_pairs.append(_dot(lhs, rhs))
    y = jnp.concatenate(y_pairs, axis=1)

    hi = lax.broadcasted_iota(I32, (hg, hg * p), 0)
    li = lax.broadcasted_iota(I32, (hg, hg * p), 1)
    widen = jnp.where((li >= hi * p) & (li < (hi + 1) * p), 1.0, 0.0).astype(BF16)
    scales = jnp.concatenate([jnp.exp(acum), jnp.exp(alast - acum) * dt,
                              jnp.broadcast_to(jnp.exp(alast), (SUBLANES, hg))], axis=0)
    wide = _dot_sel_r2(scales, widen)
    e_in = wide[:q]
    e_out = wide[q:2 * q]
    e_all = wide[2 * q:2 * q + 1]

    y = y + _dot(cm, s_old.astype(BF16)) * e_in
    xw = (x * e_out).astype(BF16)
    s_new = s_old * e_all + _dot_tn(bm, xw)

    y = y + dskip * x
    y = y * _silu(z)
    y = y * lax.rsqrt(jnp.mean(y * y, axis=-1, keepdims=True) + RMS_EPS)
    return (y * nw).astype(BF16), s_new


def ssd_scan(xbc, conv_prev, conv_w, conv_b, z, dt4, dta4, dtt4, dtat4, h0, dskip, norm_w, *,
             row0, nseq, seqlen, q, out_alias=None):
    t = xbc.shape[0]
    gps = SSD_GROUPS_PER_STEP
    g = SSD_N_GROUPS
    gc = gps * SSD_GROUP_CH
    n = SSD_D_STATE
    nchunks = seqlen // q
    rb0 = row0 // q
    bcol0 = SSD_D_INNER // (gps * n)
    ccol0 = (SSD_D_INNER + SSD_BC_DIM) // (gps * n)

    def rows(b, gg, c):
        return rb0 + b * nchunks + c

    def col_specs(shape_of, index_of):
        return [pl.BlockSpec(shape_of(gc), index_of(0)), pl.BlockSpec(shape_of(gps * n), index_of(bcol0)),
                pl.BlockSpec(shape_of(gps * n), index_of(ccol0))]

    in_specs = (col_specs(lambda w: (q, w), lambda c0: lambda b, gg, c: (rows(b, gg, c), c0 + gg))
                + col_specs(lambda w: (1, SUBLANES, w), lambda c0: lambda b, gg, c: (b, 0, c0 + gg))
                + col_specs(lambda w: (SSD_CONV_W, w), lambda c0: lambda b, gg, c: (0, c0 + gg))
                + col_specs(lambda w: (1, w), lambda c0: lambda b, gg, c: (0, c0 + gg)))
    in_specs += [pl.BlockSpec((q, gc), lambda b, gg, c: (rows(b, gg, c), gg)),
                pl.BlockSpec((gps, 1, q, 16), lambda b, gg, c: (gg, b * nchunks + c, 0, 0)),
                pl.BlockSpec((gps, 1, q, 16), lambda b, gg, c: (gg, b * nchunks + c, 0, 0)),
                pl.BlockSpec((gps, 1, 16, q), lambda b, gg, c: (gg, b * nchunks + c, 0, 0)),
                pl.BlockSpec((gps, 1, 16, q), lambda b, gg, c: (gg, b * nchunks + c, 0, 0)),
                pl.BlockSpec((1, gps, SSD_GROUP_CH, n), lambda b, gg, c: (b, gg, 0, 0)),
                pl.BlockSpec((1, gc), lambda b, gg, c: (0, gg)),
                pl.BlockSpec((1, gc), lambda b, gg, c: (0, gg))]
    args = [xbc] * 3 + [conv_prev] * 3 + [conv_w] * 3 + [conv_b] * 3 + [z, dt4, dta4, dtt4, dtat4, h0, dskip, norm_w]
    nin = len(args)
    kern = functools.partial(_ssd_kernel, q=q, nchunks=nchunks)
    aliases = {}
    if out_alias is not None:
        in_specs.append(pl.BlockSpec(memory_space=pl.ANY))
        args.append(out_alias)
        aliases = {nin: 0}
        base = kern
        kern = lambda *refs: base(*refs[:nin], *refs[nin + 1:])
    return pl.pallas_call(
        kern,
        out_shape=(jax.ShapeDtypeStruct((t, SSD_D_INNER), BF16),
                   jax.ShapeDtypeStruct((nseq, g, SSD_GROUP_CH, n), F32)),
        grid=(nseq, g // gps, nchunks),
        in_specs=in_specs,
        out_specs=(pl.BlockSpec((q, gc), lambda b, gg, c: (rows(b, gg, c), gg)),
                   pl.BlockSpec((1, gps, SSD_GROUP_CH, n), lambda b, gg, c: (b, gg, 0, 0))),
        scratch_shapes=[pltpu.VMEM((gps, n, SSD_GROUP_CH), F32), pltpu.VMEM((SUBLANES, gc), F32),
                        pltpu.VMEM((SUBLANES, gps * n), F32), pltpu.VMEM((SUBLANES, gps * n), F32)],
        input_output_aliases=aliases,
        compiler_params=_cparams(("parallel", "parallel", "arbitrary")),
        name="ssd_scan",
    )(*args)


def _dt_layouts(v, row0, nseq, seqlen, q):
    nb = nseq * seqlen // q
    part = v[row0:row0 + nseq * seqlen].reshape(nb, q, SSD_N_GROUPS, SSD_HEADS_PER_GROUP)
    return part.transpose(2, 0, 1, 3), part.transpose(2, 0, 3, 1)


def _s5_prep_kernel(lam_ref, row_ref, bcat_ref, bswap_ref, ccat_ref, cswap_ref, wst_ref, vt_ref, krev_ref, aq_ref):
    n = S5_STATE
    sp = S5_PACK_STATE
    lr, li, ls = lam_ref[0, 0], lam_ref[0, 1], lam_ref[0, 2]
    lane = lax.broadcasted_iota(I32, (LANES, LANES), 1)
    first = lane < n
    dstep = jnp.exp(ls)
    mag = jnp.exp(lr * dstep)
    ar = mag * jnp.cos(li * dstep)
    ai = mag * jnp.sin(li * dstep)
    den = lr * lr + li * li
    cr = ((ar - 1.0) * lr + ai * li) / den
    ci = (ai * lr - (ar - 1.0) * li) / den
    ci_s = jnp.where(first, -ci, ci)
    bcat, bswap = bcat_ref[0], bswap_ref[0]
    bb = cr * bcat + ci_s * bswap
    bb_swap = cr * bswap - ci_s * bcat
    ccat, cswap = ccat_ref[0], cswap_ref[0]

    row_e = lax.broadcasted_iota(I32, (LANES, 2 * sp), 0)
    col_e = lax.broadcasted_iota(I32, (LANES, 2 * sp), 1)
    target = jnp.where(col_e >= sp, n, 0) + (col_e & (n - 1))
    place = jnp.where(row_e == target, 1.0, 0.0).astype(BF16)
    own = lax.shift_right_logical(row_e, 4) == lax.shift_right_logical(col_e & (sp - 1), 6)
    row_k = lax.broadcasted_iota(I32, (LANES, LANES), 0)
    own_k = lax.shift_right_logical(row_k, 4) == lax.shift_right_logical(lane, 4)

    def expand(blk):
        return jnp.where(own, _dot(blk.astype(BF16), place), 0.0).astype(BF16)

    pr = jnp.ones((LANES, LANES), F32)
    pi = jnp.zeros((LANES, LANES), F32)
    outs = []
    for m in range(S5_Q + 1):
        pr_m = jnp.where(first, pr, -pr)
        outs.append(pr_m * ccat - pi * cswap)
        if m < S5_Q:
            t = S5_Q - 1 - m
            pi_s = jnp.where(first, -pi, pi)
            wst_ref[0, t * LANES:(t + 1) * LANES, :] = expand(pr * bb + pi_s * bb_swap)
        if m >= 1:
            vt_ref[0, (m - 1) * LANES:m * LANES, :] = expand(outs[m])
        pr, pi = pr * ar - pi * ai, pr * ai + pi * ar

    taps = jnp.concatenate([outs[S5_Q - 1 - t] for t in range(S5_Q)], axis=0)
    kk = lax.dot_general(bb, taps, (((1,), (1,)), ((), ())),
                         precision=lax.Precision.HIGHEST, preferred_element_type=F32)
    for t in range(S5_Q):
        blk = kk[:, t * LANES:(t + 1) * LANES]
        krev_ref[0, t * LANES:(t + 1) * LANES, :] = jnp.where(own_k, blk, 0.0).astype(BF16)

    rlr, rli, rls = row_ref[0, 0:1, :], row_ref[0, 1:2, :], row_ref[0, 2:3, :]
    rstep = jnp.exp(rls)
    e16 = jnp.exp((float(S5_Q) * rlr) * rstep)
    ang = (float(S5_Q) * rli) * rstep
    aq_ref[0] = jnp.concatenate([e16 * jnp.cos(ang), e16 * jnp.sin(ang)], axis=1)


def s5_prep(lam_re, lam_im, log_step, b_re, b_im, c_re, c_im):
    np_ = S5_N_PACKS
    sp = S5_PACK_STATE

    def compact(m):
        return m.reshape(np_, LANES, S5_STATE)

    def per_row(v):
        return compact(jnp.broadcast_to(v[:, None, :], (S5_N_GROUPS, S5_GROUP_CH, S5_STATE)))

    def two(a, b):
        return jnp.concatenate([a, b], axis=-1)

    step2d = jnp.broadcast_to(log_step[:, None], (S5_N_GROUPS, S5_STATE))
    lam = jnp.stack([two(per_row(v), per_row(v)) for v in (lam_re, lam_im, step2d)], axis=1)
    rows = jnp.stack([v.reshape(np_, sp) for v in (lam_re, lam_im, step2d)], axis=1)
    bre, bim = compact(b_re.transpose(0, 2, 1)), compact(b_im.transpose(0, 2, 1))
    cre, cim = compact(c_re), compact(c_im)
    mat = pl.BlockSpec((1, LANES, LANES), lambda i: (i, 0, 0))
    big = pl.BlockSpec((1, S5_Q * LANES, 2 * sp), lambda i: (i, 0, 0))
    return pl.pallas_call(
        _s5_prep_kernel,
        out_shape=(jax.ShapeDtypeStruct((np_, S5_Q * LANES, 2 * sp), BF16),
                   jax.ShapeDtypeStruct((np_, S5_Q * LANES, 2 * sp), BF16),
                   jax.ShapeDtypeStruct((np_, S5_Q * LANES, LANES), BF16),
                   jax.ShapeDtypeStruct((np_, 1, 2 * sp), F32)),
        grid=(np_,),
        in_specs=[pl.BlockSpec((1, 3, LANES, LANES), lambda i: (i, 0, 0, 0)),
                  pl.BlockSpec((1, 3, sp), lambda i: (i, 0, 0)), mat, mat, mat, mat],
        out_specs=(big, big, pl.BlockSpec((1, S5_Q * LANES, LANES), lambda i: (i, 0, 0)),
                   pl.BlockSpec((1, 1, 2 * sp), lambda i: (i, 0, 0))),
        compiler_params=_cparams(("parallel",)),
        name="s5_prep",
    )(lam, rows, two(bre, bim), two(bim, bre), two(cre, cim), two(cim, cre))


def _s5_kernel(x_ref, wst_ref, vt_ref, krev_ref, aq_ref, d_ref, h0_ref, g_ref, hout_ref, *scratch,
               nblk, nseq, chain):
    q = S5_Q
    sp = S5_PACK_STATE
    us = [x_ref[pl.ds(t, nblk, stride=q), :] for t in range(q)]
    ucat = jnp.concatenate([u.astype(BF16) for u in us], axis=1)
    s_in = _dot(ucat, wst_ref[0])
    aq = aq_ref[0]
    aqr, aqi = aq[:, :sp], aq[:, sp:]

    def advance(h, s):
        hr, hi = h[:, :sp], h[:, sp:]
        return jnp.concatenate([aqr * hr - aqi * hi + s[:, :sp], aqr * hi + aqi * hr + s[:, sp:]], axis=1)

    tmat = scratch[-1]

    @pl.when(pl.program_id(1) == 0)
    def _():
        for s in range(q):
            for t in range(s, q):
                tmat[s * LANES:(s + 1) * LANES, t * LANES:(t + 1) * LANES] = (
                    krev_ref[0, (q - 1 - (t - s)) * LANES:(q - (t - s)) * LANES, :])
        for p in range(q // 2):
            tmat[(2 * p + 1) * LANES:(2 * p + 2) * LANES, 2 * p * LANES:(2 * p + 1) * LANES] = (
                jnp.zeros((LANES, LANES), BF16))

    if chain:
        hin_s, s_s, y_s = scratch[:3]
        per_seq = nblk // nseq
        s_s[...] = s_in

        def body(c, hs):
            out = []
            for s in range(nseq):
                row = s * per_seq + c
                hin_s[pl.ds(row, 1), :] = hs[s]
                out.append(advance(hs[s], s_s[pl.ds(row, 1), :]))
            return tuple(out)

        h0 = h0_ref[0, 0]
        hs = lax.fori_loop(0, per_seq, body, tuple(h0[s:s + 1] for s in range(nseq)), unroll=2)
        hout_ref[0, 0] = jnp.concatenate(hs, axis=0)
        hin = hin_s[...]
    else:
        y_s = scratch[0]
        hin = h0_ref[0, 0]
        hout_ref[0, 0] = advance(hin, s_in)
    hinb = hin.astype(BF16)
    d = d_ref[...]
    for p in range(q // 2):
        c0, c1 = 2 * p * LANES, (2 * p + 2) * LANES
        y2 = _dot_nt(hinb, vt_ref[0, c0:c1, :])
        y2 = y2 + _dot(ucat[:, :c1], tmat[:c1, c0:c1])
        for j in range(2):
            t = 2 * p + j
            y = y2[:, j * LANES:(j + 1) * LANES] + d * us[t]
            y_s[pl.ds(t, nblk, stride=q), :] = _gelu_exact(y)
    g_ref[...] = y_s[...].astype(BF16)


def s5_scan(x, wst, vt, krev, aq, dskip, h0, *, row0, nseq, seqlen, chain, out_alias=None):
    t, d = x.shape
    q = S5_Q
    sp2 = 2 * S5_PACK_STATE
    spb = S5_CHAIN_SEQS if chain else nseq
    nstep = nseq // spb
    rows = spb * seqlen
    nblk = rows // q
    assert chain or seqlen == q
    grid = (S5_N_PACKS, nstep)
    rb0 = row0 // rows
    xmap = lambda p, b: (rb0 + b, p)
    wmap = lambda p, b: (p, 0, 0)
    dmap = lambda p, b: (0, p)
    hspec = pl.BlockSpec((1, 1, spb, sp2), lambda p, b: (p, b, 0, 0))
    hshape = (S5_N_PACKS, nstep, spb, sp2)
    scratch = [pltpu.VMEM((rows, LANES), F32), pltpu.VMEM((q * LANES, q * LANES), BF16)]
    if chain:
        scratch = [pltpu.VMEM((nblk, sp2), F32), pltpu.VMEM((nblk, sp2), F32)] + scratch
    sem = ("parallel", "arbitrary")
    in_specs = [pl.BlockSpec((rows, LANES), xmap),
                pl.BlockSpec((1, q * LANES, sp2), wmap),
                pl.BlockSpec((1, q * LANES, sp2), wmap),
                pl.BlockSpec((1, q * LANES, LANES), wmap),
                pl.BlockSpec((1, 1, sp2), wmap),
                pl.BlockSpec((1, LANES), dmap),
                hspec]
    args = [x, wst, vt, krev, aq, dskip, h0]
    kern = functools.partial(_s5_kernel, nblk=nblk, nseq=spb, chain=chain)
    aliases = {}
    if out_alias is not None:
        in_specs.append(pl.BlockSpec(memory_space=pl.ANY))
        args.append(out_alias)
        aliases = {len(args) - 1: 0}
        base = kern
        kern = lambda *refs: base(*refs[:7], *refs[8:])
    return pl.pallas_call(
        kern,
        out_shape=(jax.ShapeDtypeStruct((t, d), BF16), jax.ShapeDtypeStruct(hshape, F32)),
        grid=grid,
        in_specs=in_specs,
        out_specs=(pl.BlockSpec((rows, LANES), xmap), hspec),
        scratch_shapes=scratch,
        input_output_aliases=aliases,
        compiler_params=_cparams(sem),
        name="s5_scan",
    )(*args)


def _s5_pack_state(re, im):
    b = re.shape[0]
    return jnp.concatenate([re.reshape(b, S5_N_PACKS, S5_PACK_STATE), im.reshape(b, S5_N_PACKS, S5_PACK_STATE)], axis=-1)


def _s5_unpack_state(h):
    b = h.shape[0]
    return (h[..., :S5_PACK_STATE].reshape(b, S5_N_GROUPS, S5_STATE),
            h[..., S5_PACK_STATE:].reshape(b, S5_N_GROUPS, S5_STATE))


def kernel(x_prompt, x_sample, cache_ssd_conv, state_ssd, state_s5_re, state_s5_im, ssd_w_in, ssd_conv_w, ssd_conv_b, ssd_dt_bias, ssd_a_log, ssd_d, ssd_norm_w, ssd_w_out, s5_lambda_re, s5_lambda_im, s5_log_step, s5_b_re, s5_b_im, s5_c_re, s5_c_im, s5_d, s5_w_glu, ln_mix_g, ln_mix_b, ln_ffn_g, ln_ffn_b, router_w, router_b, moe_w_gate, moe_w_up, moe_w_down):
    d = D_MODEL
    x0_parts = [x_prompt.reshape(T_PROMPT, d), x_sample.reshape(T_SAMPLE, d)]
    x0b = cast_rows(x0_parts, BF16, tm=T_SAMPLE)
    rw = jnp.zeros((d, LANES), BF16).at[:, :N_EXPERTS].set(router_w.astype(BF16))
    rb = jnp.zeros((1, LANES), F32).at[0, :N_EXPERTS].set(router_b)

    w_in = ssd_w_in[0].astype(BF16)
    z = matmul(x0b, w_in, tm=MM_TM, tn=1024, col0=0, ncols=SSD_D_INNER)
    xbc = matmul(x0b, w_in, tm=MM_TM, tn=1024, col0=SSD_D_INNER, ncols=SSD_CONV_DIM)
    dt_raw = matmul(x0b, w_in, tm=MM_TM, tn=LANES, col0=SSD_D_INNER + SSD_CONV_DIM, ncols=SSD_N_HEADS)

    conv_w = ssd_conv_w[0]
    conv_b = ssd_conv_b[0].reshape(1, SSD_CONV_DIM)
    prev_p = jnp.zeros((BATCH, SUBLANES, SSD_CONV_DIM), F32)
    prev_s = jnp.concatenate([jnp.zeros((DEC_BATCH, SUBLANES - (SSD_CONV_W - 1), SSD_CONV_DIM), F32),
                              cache_ssd_conv[0]], axis=1)

    dt_sp, dta = dt_prep(dt_raw, ssd_dt_bias[0].reshape(1, SSD_N_HEADS), ssd_a_log[0].reshape(1, SSD_N_HEADS))
    dskip = jnp.repeat(ssd_d[0], SSD_HEAD_DIM).reshape(1, SSD_D_INNER)
    norm_w = ssd_norm_w[0].reshape(1, SSD_D_INNER)
    h0_p = jnp.zeros((BATCH, SSD_N_GROUPS, SSD_GROUP_CH, SSD_D_STATE), F32)
    h0_s = state_ssd[0].reshape(DEC_BATCH, SSD_N_GROUPS, SSD_GROUP_CH, SSD_D_STATE)
    lay_p = _dt_layouts(dt_sp, 0, BATCH, SEQ, SSD_Q_PROMPT) + _dt_layouts(dta, 0, BATCH, SEQ, SSD_Q_PROMPT)
    lay_s = (_dt_layouts(dt_sp, T_PROMPT, DEC_BATCH, DEC_SEQ, DEC_SEQ)
             + _dt_layouts(dta, T_PROMPT, DEC_BATCH, DEC_SEQ, DEC_SEQ))
    ymix, ssd_p = ssd_scan(xbc, prev_p, conv_w, conv_b, z, lay_p[0], lay_p[2], lay_p[1], lay_p[3], h0_p, dskip,
                           norm_w, row0=0, nseq=BATCH, seqlen=SEQ, q=SSD_Q_PROMPT)
    ymix, ssd_s = ssd_scan(xbc, prev_s, conv_w, conv_b, z, lay_s[0], lay_s[2], lay_s[1], lay_s[3], h0_s, dskip,
                           norm_w, row0=T_PROMPT, nseq=DEC_BATCH, seqlen=DEC_SEQ, q=DEC_SEQ, out_alias=ymix)
    mix = matmul(ymix, ssd_w_out[0].astype(BF16), tm=MM_TM, tn=512)

    wg = cast_weights(moe_w_gate, BF16, rows_per_block=D_MODEL // 2)
    wu = cast_weights(moe_w_up, BF16, rows_per_block=D_MODEL // 2)
    wd = moe_w_down
    x1, idx, gate = ln_router(x0_parts, mix, ln_mix_g[0].reshape(1, d), ln_mix_b[0].reshape(1, d), rw, rb)
    pos, y_sorted = moe_layer(x1, idx, wg, wu, wd, 0)
    (x2,) = combine_ln(pos, x1, gate, y_sorted, ln_ffn_g[0].reshape(1, d), ln_ffn_b[0].reshape(1, d))

    wst, vt, krev, aq = s5_prep(s5_lambda_re[0], s5_lambda_im[0], s5_log_step[0], s5_b_re[0], s5_b_im[0],
                                s5_c_re[0], s5_c_im[0])
    s5_dskip = s5_d[0].reshape(1, d)
    sp2 = 2 * S5_PACK_STATE
    hs_p = jnp.zeros((S5_N_PACKS, BATCH // S5_CHAIN_SEQS, S5_CHAIN_SEQS, sp2), F32)
    hs_s = _s5_pack_state(state_s5_re[0], state_s5_im[0]).transpose(1, 0, 2)[:, None]
    gact, s5_p = s5_scan(x2, wst, vt, krev, aq, s5_dskip, hs_p, row0=0, nseq=BATCH, seqlen=SEQ, chain=True)
    gact, s5_s = s5_scan(x2, wst, vt, krev, aq, s5_dskip, hs_s, row0=T_PROMPT, nseq=DEC_BATCH, seqlen=DEC_SEQ,
                         chain=False, out_alias=gact)
    mix = glu_matmul(gact, s5_w_glu[0].astype(BF16), tm=MM_TM, tn=512)

    x3, idx, gate = ln_router([x2], mix, ln_mix_g[1].reshape(1, d), ln_mix_b[1].reshape(1, d), rw, rb)
    pos, y_sorted = moe_layer(x3, idx, wg, wu, wd, 1)
    g1 = ln_ffn_g[1].reshape(1, d)
    b1 = ln_ffn_b[1].reshape(1, d)
    (y_p,) = combine_ln(pos, x3, gate, y_sorted, g1, b1, row0=0, nrows=T_PROMPT)
    (y_s,) = combine_ln(pos, x3, gate, y_sorted, g1, b1, row0=T_PROMPT, nrows=T_SAMPLE)

    keep = SSD_CONV_W - 1
    conv_p = jnp.stack([xbc[(b + 1) * SEQ - keep:(b + 1) * SEQ] for b in range(BATCH)])[None]
    conv_s = xbc[T_PROMPT:].reshape(DEC_BATCH, DEC_SEQ, SSD_CONV_DIM)[:, DEC_SEQ - keep:][None]
    ssd_state_p = ssd_p.reshape(1, BATCH, SSD_N_HEADS, SSD_HEAD_DIM, SSD_D_STATE)
    ssd_state_s = ssd_s.reshape(1, DEC_BATCH, SSD_N_HEADS, SSD_HEAD_DIM, SSD_D_STATE)
    re_p, im_p = _s5_unpack_state(s5_p.reshape(S5_N_PACKS, BATCH, sp2).transpose(1, 0, 2))
    re_s, im_s = _s5_unpack_state(s5_s[:, 0].transpose(1, 0, 2))
    return (y_p.reshape(BATCH, SEQ, d), y_s.reshape(DEC_BATCH, DEC_SEQ, d),
            conv_p, ssd_state_p, re_p[None], im_p[None],
            conv_s, ssd_state_s, re_s[None], im_s[None])
```

```python
import functools
import math

import jax
import jax.numpy as jnp
from jax import lax
from jax.experimental import pallas as pl
from jax.experimental.pallas import tpu as pltpu

F32 = jnp.float32
BF16 = jnp.bfloat16
I32 = jnp.int32

D_MODEL = 4096
BATCH, SEQ = 4, 4096
DEC_BATCH, DEC_SEQ = 32, 16
T_PROMPT = BATCH * SEQ
T_SAMPLE = DEC_BATCH * DEC_SEQ
T_ALL = T_PROMPT + T_SAMPLE
SSD_D_INNER = 8192
SSD_HEAD_DIM = 64
SSD_N_HEADS = 128
SSD_N_GROUPS = 8
SSD_HEADS_PER_GROUP = 16
SSD_D_STATE = 128
SSD_GROUP_CH = SSD_D_INNER // SSD_N_GROUPS
SSD_BC_DIM = SSD_N_GROUPS * SSD_D_STATE
SSD_CONV_DIM = SSD_D_INNER + 2 * SSD_BC_DIM
SSD_CONV_W = 4
S5_GROUP_CH = 16
S5_N_GROUPS = 256
S5_STATE = 64
S5_PACK_GROUPS = 8
S5_N_PACKS = S5_N_GROUPS // S5_PACK_GROUPS
S5_PACK_STATE = S5_PACK_GROUPS * S5_STATE
S5_Q = 16
S5_CHAIN_SEQS = 2
N_EXPERTS = 16
EXPERTS_PER_GROUP = 4
D_EXPERT = 1024
DEPTH = 2
DEEPNORM_ALPHA = (2 * DEPTH) ** 0.25
LN_EPS = 1e-5
RMS_EPS = 1e-5

LANES = 128
SUBLANES = 8
VMEM_LIMIT_BYTES = 56 * 1024 * 1024

MM_TM = 768
MOE_TM = 256
MOE_ROWS = 2 * T_ALL + N_EXPERTS * MOE_TM
MOE_TILES = MOE_ROWS // MOE_TM
LN_TM = 128
SSD_Q_PROMPT = 128
SSD_GROUPS_PER_STEP = 4


def _cparams(semantics):
    return pltpu.CompilerParams(dimension_semantics=semantics, vmem_limit_bytes=VMEM_LIMIT_BYTES)


def _dot(a, b):
    return jnp.dot(a, b, preferred_element_type=F32)


def _dot_nt(a, b):
    return lax.dot_general(a, b, (((1,), (1,)), ((), ())), preferred_element_type=F32)


def _dot_tn(a, b):
    return lax.dot_general(a, b, (((0,), (0,)), ((), ())), preferred_element_type=F32)


def _split3(v):
    hi = v.astype(BF16)
    r = v - hi.astype(F32)
    mid = r.astype(BF16)
    lo = (r - mid.astype(F32)).astype(BF16)
    return hi, mid, lo


def _dot_sel(sel, v):
    hi, mid, lo = _split3(v)
    return _dot(sel, hi) + _dot(sel, mid) + _dot(sel, lo)


def _dot_sel_r(v, sel):
    hi, mid, lo = _split3(v)
    return _dot(hi, sel) + _dot(mid, sel) + _dot(lo, sel)


def _dot_sel_r2(v, sel):
    hi = v.astype(BF16)
    lo = (v - hi.astype(F32)).astype(BF16)
    return _dot(hi, sel) + _dot(lo, sel)


def _silu(x):
    return x * jax.nn.sigmoid(x)


def _gelu_exact(x):
    return 0.5 * x * (1.0 + lax.erf(x * (1.0 / math.sqrt(2.0))))


def _mm_kernel(x_ref, w_ref, o_ref):
    o_ref[...] = _dot(x_ref[...], w_ref[...]).astype(o_ref.dtype)


def matmul(x, w, *, tm, tn, col0=0, ncols=None, out_dtype=F32):
    m, k = x.shape
    n = w.shape[1] if ncols is None else ncols
    c0 = col0 // tn
    return pl.pallas_call(
        _mm_kernel,
        out_shape=jax.ShapeDtypeStruct((m, n), out_dtype),
        grid=(n // tn, m // tm),
        in_specs=[pl.BlockSpec((tm, k), lambda j, i: (i, 0)),
                  pl.BlockSpec((k, tn), lambda j, i: (0, c0 + j))],
        out_specs=pl.BlockSpec((tm, tn), lambda j, i: (i, j)),
        compiler_params=_cparams(("parallel", "parallel")),
        name="matmul",
    )(x, w)


def _glu_kernel(x_ref, w1_ref, w2_ref, o_ref):
    x = x_ref[...]
    z1 = _dot(x, w1_ref[...])
    z2 = _dot(x, w2_ref[...])
    o_ref[...] = z1 * jax.nn.sigmoid(z2)


def glu_matmul(x, w, *, tm, tn):
    m, k = x.shape
    n = w.shape[1] // 2
    half = n // tn
    return pl.pallas_call(
        _glu_kernel,
        out_shape=jax.ShapeDtypeStruct((m, n), F32),
        grid=(n // tn, m // tm),
        in_specs=[pl.BlockSpec((tm, k), lambda j, i: (i, 0)),
                  pl.BlockSpec((k, tn), lambda j, i: (0, j)),
                  pl.BlockSpec((k, tn), lambda j, i: (0, half + j))],
        out_specs=pl.BlockSpec((tm, tn), lambda j, i: (i, j)),
        compiler_params=_cparams(("parallel", "parallel")),
        name="glu_matmul",
    )(x, w, w)


def _layer_norm(v, g, b):
    mu = jnp.mean(v, axis=-1, keepdims=True)
    d = v - mu
    var = jnp.mean(d * d, axis=-1, keepdims=True)
    return d * lax.rsqrt(var + LN_EPS) * g + b


def _route_rows(xb, rw, rb):
    tm = xb.shape[0]
    lane = lax.broadcasted_iota(I32, (tm, LANES), 1)
    live = lane < N_EXPERTS
    logits = _dot(xb, rw) + rb
    logits = jnp.where(live, logits, -jnp.inf)
    m = jnp.max(logits, axis=-1, keepdims=True)
    e = jnp.exp(logits - m)
    probs = e / jnp.sum(e, axis=-1, keepdims=True)
    best = None
    for g in range(N_EXPERTS // EXPERTS_PER_GROUP):
        in_g = (lane >= g * EXPERTS_PER_GROUP) & (lane < (g + 1) * EXPERTS_PER_GROUP)
        score = jnp.max(jnp.where(in_g, probs, -1.0), axis=-1, keepdims=True)
        if best is None:
            best, best_g = score, jnp.zeros((tm, 1), I32)
        else:
            upd = score > best
            best = jnp.where(upd, score, best)
            best_g = jnp.where(upd, g, best_g)
    lo = best_g * EXPERTS_PER_GROUP
    in_best = (lane >= lo) & (lane < lo + EXPERTS_PER_GROUP)
    cand = jnp.where(in_best, probs, -1.0)
    p1 = jnp.max(cand, axis=-1, keepdims=True)
    i1 = jnp.min(jnp.where(cand == p1, lane, LANES), axis=-1, keepdims=True)
    cand2 = jnp.where(lane == i1, -2.0, cand)
    p2 = jnp.max(cand2, axis=-1, keepdims=True)
    i2 = jnp.min(jnp.where(cand2 == p2, lane, LANES), axis=-1, keepdims=True)
    tot = p1 + p2
    idx = jnp.where(lane == 0, i1, jnp.where(lane == 1, i2, 0))
    gate = jnp.where(lane == 0, p1 / tot, jnp.where(lane == 1, p2 / tot, 0.0))
    return idx, gate


def _row_part_specs(parts, tm, d):
    specs, firsts, t0 = [], [], 0
    for p in parts:
        nt = p.shape[0] // tm
        specs.append(pl.BlockSpec((tm, d), lambda i, *_, t0=t0, nt=nt: (jnp.clip(i - t0, 0, nt - 1), 0)))
        firsts.append(t0)
        t0 += nt
    return specs, firsts


def _select_part(i, refs, firsts, store):
    for k, ref in enumerate(refs):
        hi = firsts[k + 1] if k + 1 < len(refs) else None
        cond = i >= firsts[k] if hi is None else (i >= firsts[k]) & (i < hi)

        @pl.when(cond)
        def _(ref=ref):
            store(ref[...])


def _ln_router_kernel(*refs, firsts):
    n = len(firsts)
    x_refs = refs[:n]
    mix_ref, g_ref, b_ref, rw_ref, rb_ref, xo_ref, idx_ref, gate_ref = refs[n:]

    def finish(x):
        y = _layer_norm(DEEPNORM_ALPHA * x + mix_ref[...], g_ref[...], b_ref[...])
        xo_ref[...] = y
        idx, gate = _route_rows(y.astype(BF16), rw_ref[...], rb_ref[...])
        idx_ref[...] = idx
        gate_ref[...] = gate

    _select_part(pl.program_id(0), x_refs, firsts, finish)


def ln_router(x_parts, mix, g, b, rw, rb):
    t, d = mix.shape
    tm = LN_TM
    x_specs, firsts = _row_part_specs(x_parts, tm, d)
    row = pl.BlockSpec((tm, d), lambda i: (i, 0))
    vec = pl.BlockSpec((1, d), lambda i: (0, 0))
    small = pl.BlockSpec((tm, LANES), lambda i: (i, 0))
    return pl.pallas_call(
        functools.partial(_ln_router_kernel, firsts=tuple(firsts)),
        out_shape=(jax.ShapeDtypeStruct((t, d), F32),
                   jax.ShapeDtypeStruct((t, LANES), I32), jax.ShapeDtypeStruct((t, LANES), F32)),
        grid=(t // tm,),
        in_specs=x_specs + [row, vec, vec, pl.BlockSpec((d, LANES), lambda i: (0, 0)),
                            pl.BlockSpec((1, LANES), lambda i: (0, 0))],
        out_specs=(row, small, small),
        compiler_params=_cparams(("parallel",)),
        name="ln_router",
    )(*x_parts, mix, g, b, rw, rb)


def _cast_rows_kernel(*refs, firsts):
    o_ref = refs[-1]

    def store(v):
        o_ref[...] = v.astype(o_ref.dtype)

    _select_part(pl.program_id(0), refs[:-1], firsts, store)


def cast_rows(x_parts, dtype, tm):
    d = x_parts[0].shape[1]
    t = sum(p.shape[0] for p in x_parts)
    x_specs, firsts = _row_part_specs(x_parts, tm, d)
    return pl.pallas_call(
        functools.partial(_cast_rows_kernel, firsts=tuple(firsts)),
        out_shape=jax.ShapeDtypeStruct((t, d), dtype),
        grid=(t // tm,),
        in_specs=x_specs,
        out_specs=pl.BlockSpec((tm, d), lambda i: (i, 0)),
        compiler_params=_cparams(("parallel",)),
        name="cast_rows",
    )(*x_parts)


def _cast_kernel(x_ref, o_ref):
    o_ref[...] = x_ref[...].astype(o_ref.dtype)


def cast_weights(w, dtype, rows_per_block):
    nl, ne, r, c = w.shape
    spec = pl.BlockSpec((1, 1, rows_per_block, c), lambda l, e, j: (l, e, j, 0))
    return pl.pallas_call(
        _cast_kernel,
        out_shape=jax.ShapeDtypeStruct(w.shape, dtype),
        grid=(nl, ne, r // rows_per_block),
        in_specs=[spec],
        out_specs=spec,
        compiler_params=_cparams(("parallel", "parallel", "parallel")),
        name="cast_weights",
    )(w)


def _row_gather_start(src_hbm, row, dst, dst_row, sem, priority=0):
    pltpu.make_async_copy(src_hbm.at[pl.ds(row, 1)], dst.at[pl.ds(dst_row, 1)], sem).start(priority=priority)


def _slot_wait(buf, slot, sem):
    pltpu.make_async_copy(buf.at[slot], buf.at[slot], sem.at[slot]).wait()


def _moe_up_kernel(te_ref, pos_ref, nv_ref, pad_ref, x_hbm, wg_ref, wu_ref, h_ref, xbuf, sem, src_ref):
    del te_ref
    i = pl.program_id(0)
    slot = lax.rem(i, 2)
    nvalid = nv_ref[0]

    @pl.when(i == 0)
    def _():
        def clear(r, carry):
            src_ref[r] = 0
            return carry
        for e in range(N_EXPERTS + 1):
            lax.fori_loop(pad_ref[e], pad_ref[N_EXPERTS + 1 + e], clear, 0)

        def put(s, carry):
            src_ref[pos_ref[s]] = lax.shift_right_logical(s, 1)
            return carry
        lax.fori_loop(0, pos_ref.shape[0], put, 0, unroll=8)

        def body(r, carry):
            _row_gather_start(x_hbm, src_ref[r], xbuf.at[0], r, sem.at[0])
            return carry
        lax.fori_loop(0, MOE_TM, body, 0)

    @pl.when(i <= nvalid)
    def _():
        _slot_wait(xbuf, slot, sem)

    @pl.when(i < nvalid)
    def _():
        base = (i + 1) * MOE_TM
        for r in range(MOE_TM):
            _row_gather_start(x_hbm, src_ref[base + r], xbuf.at[1 - slot], r, sem.at[1 - slot], priority=r % 2)
        xb = xbuf[slot].astype(BF16)
        gate = _dot(xb, wg_ref[0, 0])
        up = _dot(xb, wu_ref[0, 0])
        h_ref[...] = (_silu(gate) * up).astype(BF16)

    @pl.when(i >= nvalid)
    def _():
        h_ref[...] = jnp.zeros_like(h_ref)


def moe_up(tile_expert, pos, nvalid, pad_ranges, x, wg, wu, layer):
    d = x.shape[1]
    assert (2 * x.shape[0] + N_EXPERTS * (MOE_TM - 1)) // MOE_TM < MOE_TILES
    wspec = pl.BlockSpec((1, 1, d, D_EXPERT), lambda i, te, pos, nv, pad: (layer, te[i], 0, 0))
    grid_spec = pltpu.PrefetchScalarGridSpec(
        num_scalar_prefetch=4,
        grid=(MOE_TILES,),
        in_specs=[pl.BlockSpec(memory_space=pl.ANY), wspec, wspec],
        out_specs=pl.BlockSpec((MOE_TM, D_EXPERT), lambda i, te, pos, nv, pad: (i, 0)),
        scratch_shapes=[pltpu.VMEM((2, MOE_TM, d), F32), pltpu.SemaphoreType.DMA((2,)),
                        pltpu.SMEM((MOE_ROWS,), I32)],
    )
    return pl.pallas_call(
        _moe_up_kernel,
        out_shape=jax.ShapeDtypeStruct((MOE_ROWS, D_EXPERT), BF16),
        grid_spec=grid_spec,
        compiler_params=_cparams(("arbitrary",)),
        name="moe_up",
    )(tile_expert, pos, nvalid, pad_ranges, x, wg, wu)


def _moe_down_kernel(te_ref, nv_ref, h_ref, wd_ref, y_ref, wbf):
    i = pl.program_id(0)
    new_expert = (i == 0) | (te_ref[i] != te_ref[jnp.maximum(i - 1, 0)])

    @pl.when(new_expert & (i < nv_ref[0]))
    def _():
        wbf[...] = wd_ref[0, 0].astype(BF16)

    @pl.when(i < nv_ref[0])
    def _():
        y_ref[...] = _dot(h_ref[...], wbf[...])

    @pl.when(i >= nv_ref[0])
    def _():
        y_ref[...] = jnp.zeros_like(y_ref)


def moe_down(tile_expert, nvalid, h, wd, layer):
    d = wd.shape[3]
    grid_spec = pltpu.PrefetchScalarGridSpec(
        num_scalar_prefetch=2,
        grid=(MOE_TILES,),
        in_specs=[pl.BlockSpec((MOE_TM, D_EXPERT), lambda i, te, nv: (i, 0)),
                  pl.BlockSpec((1, 1, D_EXPERT, d), lambda i, te, nv: (layer, te[i], 0, 0))],
        out_specs=pl.BlockSpec((MOE_TM, d), lambda i, te, nv: (i, 0)),
        scratch_shapes=[pltpu.VMEM((D_EXPERT, d), BF16)],
    )
    return pl.pallas_call(
        _moe_down_kernel,
        out_shape=jax.ShapeDtypeStruct((MOE_ROWS, d), F32),
        grid_spec=grid_spec,
        compiler_params=_cparams(("arbitrary",)),
        name="moe_down",
    )(tile_expert, nvalid, h, wd)


def _combine_ln_kernel(pos_ref, x_ref, gsel_ref, y_hbm, g_ref, b_ref, *rest, tm, tile0, ntiles, with_router):
    if with_router:
        rw_ref, rb_ref, xo_ref, xb_ref, idx_ref, gate_ref, ybuf, sem = rest
    else:
        xo_ref, ybuf, sem = rest
    i = pl.program_id(0)
    slot = lax.rem(i, 2)

    @pl.when(i == 0)
    def _():
        def body(r, carry):
            _row_gather_start(y_hbm, pos_ref[2 * tile0 * tm + 2 * r], ybuf.at[0, 0], r, sem.at[0])
            _row_gather_start(y_hbm, pos_ref[2 * tile0 * tm + 2 * r + 1], ybuf.at[0, 1], r, sem.at[0])
            return carry
        lax.fori_loop(0, tm, body, 0)

    _slot_wait(ybuf, slot, sem)
    base = 2 * (tile0 + jnp.minimum(i + 1, ntiles - 1)) * tm
    for r in range(tm):
        _row_gather_start(y_hbm, pos_ref[base + 2 * r], ybuf.at[1 - slot, 0], r, sem.at[1 - slot], priority=0)
        _row_gather_start(y_hbm, pos_ref[base + 2 * r + 1], ybuf.at[1 - slot, 1], r, sem.at[1 - slot], priority=1)
    gsel = gsel_ref[...]
    ffn = gsel[:, 0:1] * ybuf[slot, 0] + gsel[:, 1:2] * ybuf[slot, 1]
    y = _layer_norm(DEEPNORM_ALPHA * x_ref[...] + ffn, g_ref[...], b_ref[...])
    xo_ref[...] = y
    if with_router:
        yb = y.astype(BF16)
        xb_ref[...] = yb
        idx, gate = _route_rows(yb, rw_ref[...], rb_ref[...])
        idx_ref[...] = idx
        gate_ref[...] = gate

    @pl.when(i == ntiles - 1)
    def _():
        _slot_wait(ybuf, 1 - slot, sem)


def combine_ln(pos, x, gate_sel, y_sorted, g, b, *, row0=0, nrows=None, router=None):
    d = x.shape[1]
    tm = LN_TM
    nrows = x.shape[0] if nrows is None else nrows
    tile0 = row0 // tm
    ntiles = nrows // tm
    row_in = pl.BlockSpec((tm, d), lambda i, p: (tile0 + i, 0))
    row_out = pl.BlockSpec((tm, d), lambda i, p: (i, 0))
    vec = pl.BlockSpec((1, d), lambda i, p: (0, 0))
    small = pl.BlockSpec((tm, LANES), lambda i, p: (i, 0))
    in_specs = [row_in, pl.BlockSpec((tm, LANES), lambda i, p: (tile0 + i, 0)),
                pl.BlockSpec(memory_space=pl.ANY), vec, vec]
    args = [pos, x, gate_sel, y_sorted, g, b]
    out_shape = [jax.ShapeDtypeStruct((nrows, d), F32)]
    out_specs = [row_out]
    if router is not None:
        in_specs += [pl.BlockSpec((d, LANES), lambda i, p: (0, 0)), pl.BlockSpec((1, LANES), lambda i, p: (0, 0))]
        args += list(router)
        out_shape += [jax.ShapeDtypeStruct((nrows, d), BF16), jax.ShapeDtypeStruct((nrows, LANES), I32),
                      jax.ShapeDtypeStruct((nrows, LANES), F32)]
        out_specs += [row_out, small, small]
    grid_spec = pltpu.PrefetchScalarGridSpec(
        num_scalar_prefetch=1,
        grid=(ntiles,),
        in_specs=in_specs,
        out_specs=tuple(out_specs),
        scratch_shapes=[pltpu.VMEM((2, 2, tm, d), F32), pltpu.SemaphoreType.DMA((2,))],
    )
    kern = functools.partial(_combine_ln_kernel, tm=tm, tile0=tile0, ntiles=ntiles,
                             with_router=router is not None)
    return pl.pallas_call(
        kern,
        out_shape=tuple(out_shape),
        grid_spec=grid_spec,
        compiler_params=_cparams(("arbitrary",)),
        name="combine_ln",
    )(*args)


def _route_tables(idx):
    t = idx.shape[0]
    e_flat = idx[:, :2].reshape(-1)
    onehot = (e_flat[:, None] == jnp.arange(N_EXPERTS, dtype=I32)[None, :]).astype(I32)
    csum = jnp.cumsum(onehot, axis=0)
    rank = jnp.take_along_axis(csum, e_flat[:, None], axis=1)[:, 0] - 1
    counts = csum[-1]
    padded = ((counts + MOE_TM - 1) // MOE_TM) * MOE_TM
    ends = jnp.cumsum(padded)
    starts = ends - padded
    pos = (starts[e_flat] + rank).astype(I32)
    nvalid = (ends[-1] // MOE_TM).astype(I32)
    tile_start = jnp.arange(MOE_TILES, dtype=I32) * MOE_TM
    te = jnp.sum((tile_start[:, None] >= ends[None, :]).astype(I32), axis=1)
    te = jnp.minimum(te, te[nvalid - 1]).astype(I32)
    pad_lo = jnp.concatenate([starts + counts, ends[-1:]])
    pad_hi = jnp.concatenate([ends, ends[-1:] + MOE_TM])
    return pos, te, nvalid.reshape(1), jnp.concatenate([pad_lo, pad_hi]).astype(I32)


def moe_layer(x_f32, idx, wg, wu, wd, layer):
    pos, te, nvalid, pad_ranges = _route_tables(idx)
    h = moe_up(te, pos, nvalid, pad_ranges, x_f32, wg, wu, layer)
    y_sorted = moe_down(te, nvalid, h, wd, layer)
    return pos, y_sorted


def _conv_silu(x, prev8, w, b):
    row8 = lax.broadcasted_iota(I32, prev8.shape, 0)
    acc = jnp.broadcast_to(b, x.shape)
    for k in range(SSD_CONV_W):
        s = SSD_CONV_W - 1 - k
        if s == 0:
            xs = x
        else:
            xs = pltpu.roll(x, s, axis=0)
            top = jnp.where(row8 < s, pltpu.roll(prev8, s, axis=0), xs[:SUBLANES])
            xs = jnp.concatenate([top, xs[SUBLANES:]], axis=0)
        acc = acc + xs * w[k:k + 1, :]
    return _silu(acc)


def _dt_kernel(dt_ref, bias_ref, alog_ref, dt_o, dta_o):
    v = dt_ref[...] + bias_ref[...]
    sp = jnp.maximum(v, 0.0) + jnp.log1p(jnp.exp(-jnp.abs(v)))
    dt_o[...] = sp
    dta_o[...] = sp * (-jnp.exp(alog_ref[...]))


def dt_prep(dt_raw, bias, a_log):
    t, h = dt_raw.shape
    tr = t // 8
    row = pl.BlockSpec((tr, h), lambda i: (i, 0))
    vec = pl.BlockSpec((1, h), lambda i: (0, 0))
    return pl.pallas_call(
        _dt_kernel,
        out_shape=(jax.ShapeDtypeStruct((t, h), F32), jax.ShapeDtypeStruct((t, h), F32)),
        grid=(8,),
        in_specs=[row, vec, vec],
        out_specs=(row, row),
        compiler_params=_cparams(("parallel",)),
        name="dt_prep",
    )(dt_raw, bias, a_log)


def _ssd_kernel(x_ref, b_ref, c_ref, px_ref, pb_ref, pc_ref, wx_ref, wb_ref, wc_ref, bx_ref, bb_ref, bc_ref,
                z_ref, dt_ref, dta_ref, dtt_ref, dtat_ref, h0_ref, dskip_ref, nw_ref,
                y_ref, hout_ref, state, halo_x, halo_b, halo_c, *, q, nchunks):
    c = pl.program_id(2)
    gc = SSD_GROUP_CH
    n = SSD_D_STATE
    streams = ((x_ref, px_ref, wx_ref, bx_ref, halo_x), (b_ref, pb_ref, wb_ref, bb_ref, halo_b),
               (c_ref, pc_ref, wc_ref, bc_ref, halo_c))

    @pl.when(c == 0)
    def _():
        for k in range(SSD_GROUPS_PER_STEP):
            state[k] = h0_ref[0, k].T
        for _, prev_ref, _, _, halo in streams:
            halo[...] = prev_ref[0]

    acts = []
    for raw_ref, _, w_ref, bias_ref, halo in streams:
        raw = raw_ref[...]
        acts.append(_conv_silu(raw, halo[...], w_ref[...], bias_ref[...]))
        halo[...] = raw[q - SUBLANES:, :]
    xa, ba, ca = acts

    new_states = []
    for k in range(SSD_GROUPS_PER_STEP):
        cols = slice(k * gc, (k + 1) * gc)
        y, s_new = _ssd_group(
            xa[:, cols], ba[:, k * n:(k + 1) * n], ca[:, k * n:(k + 1) * n], z_ref[:, cols],
            dt_ref[k, 0], dta_ref[k, 0], dtt_ref[k, 0], dtat_ref[k, 0], state[k],
            dskip_ref[:, cols], nw_ref[:, cols], q=q)
        y_ref[:, cols] = y
        state[k] = s_new
        new_states.append(s_new)

    @pl.when(c == nchunks - 1)
    def _():
        for k in range(SSD_GROUPS_PER_STEP):
            hout_ref[0, k] = new_states[k].T


def _ssd_group(x, bm, cm, z, dt, dta, dtt, dtat, s_old, dskip, nw, *, q):
    hg = SSD_HEADS_PER_GROUP
    p = SSD_HEAD_DIM
    xb = x.astype(BF16)
    bm = bm.astype(BF16)
    cm = cm.astype(BF16)

    ri = lax.broadcasted_iota(I32, (q, q), 0)
    ci = lax.broadcasted_iota(I32, (q, q), 1)
    causal = ri >= ci
    lower = jnp.where(causal, 1.0, 0.0).astype(BF16)
    upper = jnp.where(ri <= ci, 1.0, 0.0).astype(BF16)
    log2e = math.log2(math.e)
    acum = _dot_sel(lower, dta) * log2e
    acumt = _dot_sel_r(dtat, upper) * log2e
    alast = acum[q - 1:q, :]

    cb = _dot_nt(cm, bm)
    lane = lax.broadcasted_iota(I32, (q, 2 * p), 1)
    y_pairs = []
    for hp in range(hg // 2):
        ms = []
        for h in (2 * hp, 2 * hp + 1):
            seg = acum[:, h:h + 1] - acumt[h:h + 1, :]
            decay = jnp.exp2(jnp.where(causal, seg, -jnp.inf))
            ms.append((cb * decay * dtt[h:h + 1, :]).astype(BF16))
        lhs = jnp.concatenate(ms, axis=1)
        xp = xb[:, 2 * p * hp:2 * p * (hp + 1)]
        zero = jnp.zeros_like(xp)
        rhs = jnp.concatenate([jnp.where(lane < p, xp, zero), jnp.where(lane >= p, xp, zero)], axis=0)
        y_pairs.append(_dot(lhs, rhs))
    y = jnp.concatenate(y_pairs, axis=1)

    hi = lax.broadcasted_iota(I32, (hg, hg * p), 0)
    li = lax.broadcasted_iota(I32, (hg, hg * p), 1)
    widen = jnp.where((li >= hi * p) & (li < (hi + 1) * p), 1.0, 0.0).astype(BF16)
    scales = jnp.concatenate([jnp.exp2(acum), jnp.exp2(alast - acum) * dt,
                              jnp.broadcast_to(jnp.exp2(alast), (SUBLANES, hg))], axis=0)
    wide = _dot_sel_r2(scales, widen)
    e_in = wide[:q]
    e_out = wide[q:2 * q]
    e_all = wide[2 * q:2 * q + 1]

    y = y + _dot(cm, s_old.astype(BF16)) * e_in
    xw = (x * e_out).astype(BF16)
    s_new = s_old * e_all + _dot_tn(bm, xw)

    y = y + dskip * x
    y = y * _silu(z)
    y = y * lax.rsqrt(jnp.mean(y * y, axis=-1, keepdims=True) + RMS_EPS)
    return (y * nw).astype(BF16), s_new


def ssd_scan(xbc, conv_prev, conv_w, conv_b, z, dt4, dta4, dtt4, dtat4, h0, dskip, norm_w, *,
             row0, nseq, seqlen, q, out_alias=None):
    t = xbc.shape[0]
    gps = SSD_GROUPS_PER_STEP
    g = SSD_N_GROUPS
    gc = gps * SSD_GROUP_CH
    n = SSD_D_STATE
    nchunks = seqlen // q
    rb0 = row0 // q
    bcol0 = SSD_D_INNER // (gps * n)
    ccol0 = (SSD_D_INNER + SSD_BC_DIM) // (gps * n)

    def rows(b, gg, c):
        return rb0 + b * nchunks + c

    def col_specs(shape_of, index_of):
        return [pl.BlockSpec(shape_of(gc), index_of(0)), pl.BlockSpec(shape_of(gps * n), index_of(bcol0)),
                pl.BlockSpec(shape_of(gps * n), index_of(ccol0))]

    in_specs = (col_specs(lambda w: (q, w), lambda c0: lambda b, gg, c: (rows(b, gg, c), c0 + gg))
                + col_specs(lambda w: (1, SUBLANES, w), lambda c0: lambda b, gg, c: (b, 0, c0 + gg))
                + col_specs(lambda w: (SSD_CONV_W, w), lambda c0: lambda b, gg, c: (0, c0 + gg))
                + col_specs(lambda w: (1, w), lambda c0: lambda b, gg, c: (0, c0 + gg)))
    in_specs += [pl.BlockSpec((q, gc), lambda b, gg, c: (rows(b, gg, c), gg)),
                pl.BlockSpec((gps, 1, q, 16), lambda b, gg, c: (gg, b * nchunks + c, 0, 0)),
                pl.BlockSpec((gps, 1, q, 16), lambda b, gg, c: (gg, b * nchunks + c, 0, 0)),
                pl.BlockSpec((gps, 1, 16, q), lambda b, gg, c: (gg, b * nchunks + c, 0, 0)),
                pl.BlockSpec((gps, 1, 16, q), lambda b, gg, c: (gg, b * nchunks + c, 0, 0)),
                pl.BlockSpec((1, gps, SSD_GROUP_CH, n), lambda b, gg, c: (b, gg, 0, 0)),
                pl.BlockSpec((1, gc), lambda b, gg, c: (0, gg)),
                pl.BlockSpec((1, gc), lambda b, gg, c: (0, gg))]
    args = [xbc] * 3 + [conv_prev] * 3 + [conv_w] * 3 + [conv_b] * 3 + [z, dt4, dta4, dtt4, dtat4, h0, dskip, norm_w]
    nin = len(args)
    kern = functools.partial(_ssd_kernel, q=q, nchunks=nchunks)
    aliases = {}
    if out_alias is not None:
        in_specs.append(pl.BlockSpec(memory_space=pl.ANY))
        args.append(out_alias)
        aliases = {nin: 0}
        base = kern
        kern = lambda *refs: base(*refs[:nin], *refs[nin + 1:])
    return pl.pallas_call(
        kern,
        out_shape=(jax.ShapeDtypeStruct((t, SSD_D_INNER), BF16),
                   jax.ShapeDtypeStruct((nseq, g, SSD_GROUP_CH, n), F32)),
        grid=(nseq, g // gps, nchunks),
        in_specs=in_specs,
        out_specs=(pl.BlockSpec((q, gc), lambda b, gg, c: (rows(b, gg, c), gg)),
                   pl.BlockSpec((1, gps, SSD_GROUP_CH, n), lambda b, gg, c: (b, gg, 0, 0))),
        scratch_shapes=[pltpu.VMEM((gps, n, SSD_GROUP_CH), F32), pltpu.VMEM((SUBLANES, gc), F32),
                        pltpu.VMEM((SUBLANES, gps * n), F32), pltpu.VMEM((SUBLANES, gps * n), F32)],
        input_output_aliases=aliases,
        compiler_params=_cparams(("parallel", "parallel", "arbitrary")),
        name="ssd_scan",
    )(*args)


def _dt_layouts(v, row0, nseq, seqlen, q):
    nb = nseq * seqlen // q
    part = v[row0:row0 + nseq * seqlen].reshape(nb, q, SSD_N_GROUPS, SSD_HEADS_PER_GROUP)
    return part.transpose(2, 0, 1, 3), part.transpose(2, 0, 3, 1)


def _s5_prep_kernel(lam_ref, row_ref, bcat_ref, bswap_ref, ccat_ref, cswap_ref, wst_ref, vt_ref, krev_ref, aq_ref):
    n = S5_STATE
    sp = S5_PACK_STATE
    lr, li, ls = lam_ref[0, 0], lam_ref[0, 1], lam_ref[0, 2]
    lane = lax.broadcasted_iota(I32, (LANES, LANES), 1)
    first = lane < n
    dstep = jnp.exp(ls)
    mag = jnp.exp(lr * dstep)
    ar = mag * jnp.cos(li * dstep)
    ai = mag * jnp.sin(li * dstep)
    den = lr * lr + li * li
    cr = ((ar - 1.0) * lr + ai * li) / den
    ci = (ai * lr - (ar - 1.0) * li) / den
    ci_s = jnp.where(first, -ci, ci)
    bcat, bswap = bcat_ref[0], bswap_ref[0]
    bb = cr * bcat + ci_s * bswap
    bb_swap = cr * bswap - ci_s * bcat
    ccat, cswap = ccat_ref[0], cswap_ref[0]

    row_e = lax.broadcasted_iota(I32, (LANES, 2 * sp), 0)
    col_e = lax.broadcasted_iota(I32, (LANES, 2 * sp), 1)
    target = jnp.where(col_e >= sp, n, 0) + (col_e & (n - 1))
    place = jnp.where(row_e == target, 1.0, 0.0).astype(BF16)
    own = lax.shift_right_logical(row_e, 4) == lax.shift_right_logical(col_e & (sp - 1), 6)
    row_k = lax.broadcasted_iota(I32, (LANES, LANES), 0)
    own_k = lax.shift_right_logical(row_k, 4) == lax.shift_right_logical(lane, 4)

    def expand(blk):
        return jnp.where(own, _dot(blk.astype(BF16), place), 0.0).astype(BF16)

    pr = jnp.ones((LANES, LANES), F32)
    pi = jnp.zeros((LANES, LANES), F32)
    outs = []
    for m in range(S5_Q + 1):
        pr_m = jnp.where(first, pr, -pr)
        outs.append(pr_m * ccat - pi * cswap)
        if m < S5_Q:
            t = S5_Q - 1 - m
            pi_s = jnp.where(first, -pi, pi)
            wst_ref[0, t * LANES:(t + 1) * LANES, :] = expand(pr * bb + pi_s * bb_swap)
        if m >= 1:
            vt_ref[0, (m - 1) * LANES:m * LANES, :] = expand(outs[m])
        pr, pi = pr * ar - pi * ai, pr * ai + pi * ar

    taps = jnp.concatenate([outs[S5_Q - 1 - t] for t in range(S5_Q)], axis=0)
    kk = lax.dot_general(bb, taps, (((1,), (1,)), ((), ())),
                         precision=lax.Precision.HIGHEST, preferred_element_type=F32)
    for t in range(S5_Q):
        blk = kk[:, t * LANES:(t + 1) * LANES]
        krev_ref[0, t * LANES:(t + 1) * LANES, :] = jnp.where(own_k, blk, 0.0).astype(BF16)

    rlr, rli, rls = row_ref[0, 0:1, :], row_ref[0, 1:2, :], row_ref[0, 2:3, :]
    rstep = jnp.exp(rls)
    e16 = jnp.exp((float(S5_Q) * rlr) * rstep)
    ang = (float(S5_Q) * rli) * rstep
    aq_ref[0] = jnp.concatenate([e16 * jnp.cos(ang), e16 * jnp.sin(ang)], axis=1)


def s5_prep(lam_re, lam_im, log_step, b_re, b_im, c_re, c_im):
    np_ = S5_N_PACKS
    sp = S5_PACK_STATE

    def compact(m):
        return m.reshape(np_, LANES, S5_STATE)

    def per_row(v):
        return compact(jnp.broadcast_to(v[:, None, :], (S5_N_GROUPS, S5_GROUP_CH, S5_STATE)))

    def two(a, b):
        return jnp.concatenate([a, b], axis=-1)

    step2d = jnp.broadcast_to(log_step[:, None], (S5_N_GROUPS, S5_STATE))
    lam = jnp.stack([two(per_row(v), per_row(v)) for v in (lam_re, lam_im, step2d)], axis=1)
    rows = jnp.stack([v.reshape(np_, sp) for v in (lam_re, lam_im, step2d)], axis=1)
    bre, bim = compact(b_re.transpose(0, 2, 1)), compact(b_im.transpose(0, 2, 1))
    cre, cim = compact(c_re), compact(c_im)
    mat = pl.BlockSpec((1, LANES, LANES), lambda i: (i, 0, 0))
    big = pl.BlockSpec((1, S5_Q * LANES, 2 * sp), lambda i: (i, 0, 0))
    return pl.pallas_call(
        _s5_prep_kernel,
        out_shape=(jax.ShapeDtypeStruct((np_, S5_Q * LANES, 2 * sp), BF16),
                   jax.ShapeDtypeStruct((np_, S5_Q * LANES, 2 * sp), BF16),
                   jax.ShapeDtypeStruct((np_, S5_Q * LANES, LANES), BF16),
                   jax.ShapeDtypeStruct((np_, 1, 2 * sp), F32)),
        grid=(np_,),
        in_specs=[pl.BlockSpec((1, 3, LANES, LANES), lambda i: (i, 0, 0, 0)),
                  pl.BlockSpec((1, 3, sp), lambda i: (i, 0, 0)), mat, mat, mat, mat],
        out_specs=(big, big, pl.BlockSpec((1, S5_Q * LANES, LANES), lambda i: (i, 0, 0)),
                   pl.BlockSpec((1, 1, 2 * sp), lambda i: (i, 0, 0))),
        compiler_params=_cparams(("parallel",)),
        name="s5_prep",
    )(lam, rows, two(bre, bim), two(bim, bre), two(cre, cim), two(cim, cre))


def _s5_kernel(x_ref, wst_ref, vt_ref, krev_ref, aq_ref, d_ref, h0_ref, g_ref, hout_ref, *scratch,
               nblk, nseq, chain):
    q = S5_Q
    sp = S5_PACK_STATE
    us = [x_ref[pl.ds(t, nblk, stride=q), :] for t in range(q)]
    ucat = jnp.concatenate([u.astype(BF16) for u in us], axis=1)
    s_in = _dot(ucat, wst_ref[0])
    aq = aq_ref[0]
    aqr, aqi = aq[:, :sp], aq[:, sp:]

    def advance(h, s):
        hr, hi = h[:, :sp], h[:, sp:]
        return jnp.concatenate([aqr * hr - aqi * hi + s[:, :sp], aqr * hi + aqi * hr + s[:, sp:]], axis=1)

    tmat = scratch[-1]

    @pl.when(pl.program_id(1) == 0)
    def _():
        for s in range(q):
            for t in range(s, q):
                tmat[s * LANES:(s + 1) * LANES, t * LANES:(t + 1) * LANES] = (
                    krev_ref[0, (q - 1 - (t - s)) * LANES:(q - (t - s)) * LANES, :])
        for p in range(q // 2):
            tmat[(2 * p + 1) * LANES:(2 * p + 2) * LANES, 2 * p * LANES:(2 * p + 1) * LANES] = (
                jnp.zeros((LANES, LANES), BF16))

    if chain:
        hin_s, s_s, y_s = scratch[:3]
        per_seq = nblk // nseq
        s_s[...] = s_in

        def body(c, hs):
            out = []
            for s in range(nseq):
                row = s * per_seq + c
                hin_s[pl.ds(row, 1), :] = hs[s]
                out.append(advance(hs[s], s_s[pl.ds(row, 1), :]))
            return tuple(out)

        h0 = h0_ref[0, 0]
        hs = lax.fori_loop(0, per_seq, body, tuple(h0[s:s + 1] for s in range(nseq)), unroll=8)
        hout_ref[0, 0] = jnp.concatenate(hs, axis=0)
        hin = hin_s[...]
    else:
        y_s = scratch[0]
        hin = h0_ref[0, 0]
        hout_ref[0, 0] = advance(hin, s_in)
    hinb = hin.astype(BF16)
    d = d_ref[...]
    for p in range(q // 2):
        c0, c1 = 2 * p * LANES, (2 * p + 2) * LANES
        y2 = _dot_nt(hinb, vt_ref[0, c0:c1, :])
        y2 = y2 + _dot(ucat[:, :c1], tmat[:c1, c0:c1])
        for j in range(2):
            t = 2 * p + j
            y = y2[:, j * LANES:(j + 1) * LANES] + d * us[t]
            y_s[pl.ds(t, nblk, stride=q), :] = _gelu_exact(y)
    g_ref[...] = y_s[...].astype(BF16)


def s5_scan(x, wst, vt, krev, aq, dskip, h0, *, row0, nseq, seqlen, chain, out_alias=None):
    t, d = x.shape
    q = S5_Q
    sp2 = 2 * S5_PACK_STATE
    spb = S5_CHAIN_SEQS if chain else nseq
    nstep = nseq // spb
    rows = spb * seqlen
    nblk = rows // q
    assert chain or seqlen == q
    grid = (S5_N_PACKS, nstep)
    rb0 = row0 // rows
    xmap = lambda p, b: (rb0 + b, p)
    wmap = lambda p, b: (p, 0, 0)
    dmap = lambda p, b: (0, p)
    hspec = pl.BlockSpec((1, 1, spb, sp2), lambda p, b: (p, b, 0, 0))
    hshape = (S5_N_PACKS, nstep, spb, sp2)
    scratch = [pltpu.VMEM((rows, LANES), F32), pltpu.VMEM((q * LANES, q * LANES), BF16)]
    if chain:
        scratch = [pltpu.VMEM((nblk, sp2), F32), pltpu.VMEM((nblk, sp2), F32)] + scratch
    sem = ("parallel", "arbitrary")
    in_specs = [pl.BlockSpec((rows, LANES), xmap),
                pl.BlockSpec((1, q * LANES, sp2), wmap),
                pl.BlockSpec((1, q * LANES, sp2), wmap),
                pl.BlockSpec((1, q * LANES, LANES), wmap),
                pl.BlockSpec((1, 1, sp2), wmap),
                pl.BlockSpec((1, LANES), dmap),
                hspec]
    args = [x, wst, vt, krev, aq, dskip, h0]
    kern = functools.partial(_s5_kernel, nblk=nblk, nseq=spb, chain=chain)
    aliases = {}
    if out_alias is not None:
        in_specs.append(pl.BlockSpec(memory_space=pl.ANY))
        args.append(out_alias)
        aliases = {len(args) - 1: 0}
        base = kern
        kern = lambda *refs: base(*refs[:7], *refs[8:])
    return pl.pallas_call(
        kern,
        out_shape=(jax.ShapeDtypeStruct((t, d), BF16), jax.ShapeDtypeStruct(hshape, F32)),
        grid=grid,
        in_specs=in_specs,
        out_specs=(pl.BlockSpec((rows, LANES), xmap), hspec),
        scratch_shapes=scratch,
        input_output_aliases=aliases,
        compiler_params=_cparams(sem),
        name="s5_scan",
    )(*args)


def _s5_pack_state(re, im):
    b = re.shape[0]
    return jnp.concatenate([re.reshape(b, S5_N_PACKS, S5_PACK_STATE), im.reshape(b, S5_N_PACKS, S5_PACK_STATE)], axis=-1)


def _s5_unpack_state(h):
    b = h.shape[0]
    return (h[..., :S5_PACK_STATE].reshape(b, S5_N_GROUPS, S5_STATE),
            h[..., S5_PACK_STATE:].reshape(b, S5_N_GROUPS, S5_STATE))


def kernel(x_prompt, x_sample, cache_ssd_conv, state_ssd, state_s5_re, state_s5_im, ssd_w_in, ssd_conv_w, ssd_conv_b, ssd_dt_bias, ssd_a_log, ssd_d, ssd_norm_w, ssd_w_out, s5_lambda_re, s5_lambda_im, s5_log_step, s5_b_re, s5_b_im, s5_c_re, s5_c_im, s5_d, s5_w_glu, ln_mix_g, ln_mix_b, ln_ffn_g, ln_ffn_b, router_w, router_b, moe_w_gate, moe_w_up, moe_w_down):
    d = D_MODEL
    x0_parts = [x_prompt.reshape(T_PROMPT, d), x_sample.reshape(T_SAMPLE, d)]
    x0b = cast_rows(x0_parts, BF16, tm=T_SAMPLE)
    rw = jnp.zeros((d, LANES), BF16).at[:, :N_EXPERTS].set(router_w.astype(BF16))
    rb = jnp.zeros((1, LANES), F32).at[0, :N_EXPERTS].set(router_b)

    w_in = ssd_w_in[0].astype(BF16)
    z = matmul(x0b, w_in, tm=MM_TM, tn=1024, col0=0, ncols=SSD_D_INNER)
    xbc = matmul(x0b, w_in, tm=MM_TM, tn=1024, col0=SSD_D_INNER, ncols=SSD_CONV_DIM)
    dt_raw = matmul(x0b, w_in, tm=MM_TM, tn=LANES, col0=SSD_D_INNER + SSD_CONV_DIM, ncols=SSD_N_HEADS)

    conv_w = ssd_conv_w[0]
    conv_b = ssd_conv_b[0].reshape(1, SSD_CONV_DIM)
    prev_p = jnp.zeros((BATCH, SUBLANES, SSD_CONV_DIM), F32)
    prev_s = jnp.concatenate([jnp.zeros((DEC_BATCH, SUBLANES - (SSD_CONV_W - 1), SSD_CONV_DIM), F32),
                              cache_ssd_conv[0]], axis=1)

    dt_sp, dta = dt_prep(dt_raw, ssd_dt_bias[0].reshape(1, SSD_N_HEADS), ssd_a_log[0].reshape(1, SSD_N_HEADS))
    dskip = jnp.repeat(ssd_d[0], SSD_HEAD_DIM).reshape(1, SSD_D_INNER)
    norm_w = ssd_norm_w[0].reshape(1, SSD_D_INNER)
    h0_p = jnp.zeros((BATCH, SSD_N_GROUPS, SSD_GROUP_CH, SSD_D_STATE), F32)
    h0_s = state_ssd[0].reshape(DEC_BATCH, SSD_N_GROUPS, SSD_GROUP_CH, SSD_D_STATE)
    lay_p = _dt_layouts(dt_sp, 0, BATCH, SEQ, SSD_Q_PROMPT) + _dt_layouts(dta, 0, BATCH, SEQ, SSD_Q_PROMPT)
    lay_s = (_dt_layouts(dt_sp, T_PROMPT, DEC_BATCH, DEC_SEQ, DEC_SEQ)
             + _dt_layouts(dta, T_PROMPT, DEC_BATCH, DEC_SEQ, DEC_SEQ))
    ymix, ssd_p = ssd_scan(xbc, prev_p, conv_w, conv_b, z, lay_p[0], lay_p[2], lay_p[1], lay_p[3], h0_p, dskip,
                           norm_w, row0=0, nseq=BATCH, seqlen=SEQ, q=SSD_Q_PROMPT)
    ymix, ssd_s = ssd_scan(xbc, prev_s, conv_w, conv_b, z, lay_s[0], lay_s[2], lay_s[1], lay_s[3], h0_s, dskip,
                           norm_w, row0=T_PROMPT, nseq=DEC_BATCH, seqlen=DEC_SEQ, q=DEC_SEQ, out_alias=ymix)
    mix = matmul(ymix, ssd_w_out[0].astype(BF16), tm=MM_TM, tn=512)

    wg = cast_weights(moe_w_gate, BF16, rows_per_block=D_MODEL // 2)
    wu = cast_weights(moe_w_up, BF16, rows_per_block=D_MODEL // 2)
    wd = moe_w_down
    x1, idx, gate = ln_router(x0_parts, mix, ln_mix_g[0].reshape(1, d), ln_mix_b[0].reshape(1, d), rw, rb)
    pos, y_sorted = moe_layer(x1, idx, wg, wu, wd, 0)
    (x2,) = combine_ln(pos, x1, gate, y_sorted, ln_ffn_g[0].reshape(1, d), ln_ffn_b[0].reshape(1, d))

    wst, vt, krev, aq = s5_prep(s5_lambda_re[0], s5_lambda_im[0], s5_log_step[0], s5_b_re[0], s5_b_im[0],
                                s5_c_re[0], s5_c_im[0])
    s5_dskip = s5_d[0].reshape(1, d)
    sp2 = 2 * S5_PACK_STATE
    hs_p = jnp.zeros((S5_N_PACKS, BATCH // S5_CHAIN_SEQS, S5_CHAIN_SEQS, sp2), F32)
    hs_s = _s5_pack_state(state_s5_re[0], state_s5_im[0]).transpose(1, 0, 2)[:, None]
    gact, s5_p = s5_scan(x2, wst, vt, krev, aq, s5_dskip, hs_p, row0=0, nseq=BATCH, seqlen=SEQ, chain=True)
    gact, s5_s = s5_scan(x2, wst, vt, krev, aq, s5_dskip, hs_s, row0=T_PROMPT, nseq=DEC_BATCH, seqlen=DEC_SEQ,
                         chain=False, out_alias=gact)
    mix = glu_matmul(gact, s5_w_glu[0].astype(BF16), tm=MM_TM, tn=512)

    x3, idx, gate = ln_router([x2], mix, ln_mix_g[1].reshape(1, d), ln_mix_b[1].reshape(1, d), rw, rb)
    pos, y_sorted = moe_layer(x3, idx, wg, wu, wd, 1)
    g1 = ln_ffn_g[1].reshape(1, d)
    b1 = ln_ffn_b[1].reshape(1, d)
    (y_p,) = combine_ln(pos, x3, gate, y_sorted, g1, b1, row0=0, nrows=T_PROMPT)
    (y_s,) = combine_ln(pos, x3, gate, y_sorted, g1, b1, row0=T_PROMPT, nrows=T_SAMPLE)

    keep = SSD_CONV_W - 1
    conv_p = jnp.stack([xbc[(b + 1) * SEQ - keep:(b + 1) * SEQ] for b in range(BATCH)])[None]
    conv_s = xbc[T_PROMPT:].reshape(DEC_BATCH, DEC_SEQ, SSD_CONV_DIM)[:, DEC_SEQ - keep:][None]
    ssd_state_p = ssd_p.reshape(1, BATCH, SSD_N_HEADS, SSD_HEAD_DIM, SSD_D_STATE)
    ssd_state_s = ssd_s.reshape(1, DEC_BATCH, SSD_N_HEADS, SSD_HEAD_DIM, SSD_D_STATE)
    re_p, im_p = _s5_unpack_state(s5_p.reshape(S5_N_PACKS, BATCH, sp2).transpose(1, 0, 2))
    re_s, im_s = _s5_unpack_state(s5_s[:, 0].transpose(1, 0, 2))
    return (y_p.reshape(BATCH, SEQ, d), y_s.reshape(DEC_BATCH, DEC_SEQ, d),
            conv_p, ssd_state_p, re_p[None], im_p[None],
            conv_s, ssd_state_s, re_s[None], im_s[None])
```

```python
import functools
import math

import jax
import jax.numpy as jnp
from jax import lax
from jax.experimental import pallas as pl
from jax.experimental.pallas import tpu as pltpu

F32 = jnp.float32
BF16 = jnp.bfloat16
I32 = jnp.int32

D_MODEL = 4096
BATCH, SEQ = 4, 4096
DEC_BATCH, DEC_SEQ = 32, 16
T_PROMPT = BATCH * SEQ
T_SAMPLE = DEC_BATCH * DEC_SEQ
T_ALL = T_PROMPT + T_SAMPLE
SSD_D_INNER = 8192
SSD_HEAD_DIM = 64
SSD_N_HEADS = 128
SSD_N_GROUPS = 8
SSD_HEADS_PER_GROUP = 16
SSD_D_STATE = 128
SSD_GROUP_CH = SSD_D_INNER // SSD_N_GROUPS
SSD_BC_DIM = SSD_N_GROUPS * SSD_D_STATE
SSD_CONV_DIM = SSD_D_INNER + 2 * SSD_BC_DIM
SSD_CONV_W = 4
S5_GROUP_CH = 16
S5_N_GROUPS = 256
S5_STATE = 64
S5_PACK_GROUPS = 8
S5_N_PACKS = S5_N_GROUPS // S5_PACK_GROUPS
S5_PACK_STATE = S5_PACK_GROUPS * S5_STATE
S5_Q = 16
S5_CHAIN_SEQS = 2
N_EXPERTS = 16
EXPERTS_PER_GROUP = 4
D_EXPERT = 1024
DEPTH = 2
DEEPNORM_ALPHA = (2 * DEPTH) ** 0.25
LN_EPS = 1e-5
RMS_EPS = 1e-5

LANES = 128
SUBLANES = 8
VMEM_LIMIT_BYTES = 56 * 1024 * 1024

MM_TM = 768
MOE_TM = 256
MOE_ROWS = 2 * T_ALL + N_EXPERTS * MOE_TM
MOE_TILES = MOE_ROWS // MOE_TM
LN_TM = 256
SSD_Q_PROMPT = 128
SSD_GROUPS_PER_STEP = 4


def _cparams(semantics):
    return pltpu.CompilerParams(dimension_semantics=semantics, vmem_limit_bytes=VMEM_LIMIT_BYTES)


def _dot(a, b):
    return jnp.dot(a, b, preferred_element_type=F32)


def _dot_nt(a, b):
    return lax.dot_general(a, b, (((1,), (1,)), ((), ())), preferred_element_type=F32)


def _dot_tn(a, b):
    return lax.dot_general(a, b, (((0,), (0,)), ((), ())), preferred_element_type=F32)


def _split3(v):
    hi = v.astype(BF16)
    r = v - hi.astype(F32)
    mid = r.astype(BF16)
    lo = (r - mid.astype(F32)).astype(BF16)
    return hi, mid, lo


def _dot_sel(sel, v):
    hi, mid, lo = _split3(v)
    return _dot(sel, hi) + _dot(sel, mid) + _dot(sel, lo)


def _dot_sel_r(v, sel):
    hi, mid, lo = _split3(v)
    return _dot(hi, sel) + _dot(mid, sel) + _dot(lo, sel)


def _dot_sel_r2(v, sel):
    hi = v.astype(BF16)
    lo = (v - hi.astype(F32)).astype(BF16)
    return _dot(hi, sel) + _dot(lo, sel)


def _silu(x):
    return x * jax.nn.sigmoid(x)


def _gelu_exact(x):
    return 0.5 * x * (1.0 + lax.erf(x * (1.0 / math.sqrt(2.0))))


def _mm_kernel(x_ref, w_ref, o_ref):
    o_ref[...] = _dot(x_ref[...], w_ref[...]).astype(o_ref.dtype)


def matmul(x, w, *, tm, tn, col0=0, ncols=None, out_dtype=F32):
    m, k = x.shape
    n = w.shape[1] if ncols is None else ncols
    c0 = col0 // tn
    return pl.pallas_call(
        _mm_kernel,
        out_shape=jax.ShapeDtypeStruct((m, n), out_dtype),
        grid=(n // tn, m // tm),
        in_specs=[pl.BlockSpec((tm, k), lambda j, i: (i, 0)),
                  pl.BlockSpec((k, tn), lambda j, i: (0, c0 + j))],
        out_specs=pl.BlockSpec((tm, tn), lambda j, i: (i, j)),
        compiler_params=_cparams(("parallel", "parallel")),
        name="matmul",
    )(x, w)


def _glu_kernel(x_ref, w1_ref, w2_ref, o_ref):
    x = x_ref[...]
    z1 = _dot(x, w1_ref[...])
    z2 = _dot(x, w2_ref[...])
    o_ref[...] = z1 * jax.nn.sigmoid(z2)


def glu_matmul(x, w, *, tm, tn):
    m, k = x.shape
    n = w.shape[1] // 2
    half = n // tn
    return pl.pallas_call(
        _glu_kernel,
        out_shape=jax.ShapeDtypeStruct((m, n), F32),
        grid=(n // tn, m // tm),
        in_specs=[pl.BlockSpec((tm, k), lambda j, i: (i, 0)),
                  pl.BlockSpec((k, tn), lambda j, i: (0, j)),
                  pl.BlockSpec((k, tn), lambda j, i: (0, half + j))],
        out_specs=pl.BlockSpec((tm, tn), lambda j, i: (i, j)),
        compiler_params=_cparams(("parallel", "parallel")),
        name="glu_matmul",
    )(x, w, w)


def _layer_norm(v, g, b):
    mu = jnp.mean(v, axis=-1, keepdims=True)
    d = v - mu
    var = jnp.mean(d * d, axis=-1, keepdims=True)
    return d * lax.rsqrt(var + LN_EPS) * g + b


def _route_rows(xb, rw, rb):
    tm = xb.shape[0]
    lane = lax.broadcasted_iota(I32, (tm, LANES), 1)
    live = lane < N_EXPERTS
    logits = _dot(xb, rw) + rb
    logits = jnp.where(live, logits, -jnp.inf)
    m = jnp.max(logits, axis=-1, keepdims=True)
    e = jnp.exp(logits - m)
    probs = e / jnp.sum(e, axis=-1, keepdims=True)
    best = None
    for g in range(N_EXPERTS // EXPERTS_PER_GROUP):
        in_g = (lane >= g * EXPERTS_PER_GROUP) & (lane < (g + 1) * EXPERTS_PER_GROUP)
        score = jnp.max(jnp.where(in_g, probs, -1.0), axis=-1, keepdims=True)
        if best is None:
            best, best_g = score, jnp.zeros((tm, 1), I32)
        else:
            upd = score > best
            best = jnp.where(upd, score, best)
            best_g = jnp.where(upd, g, best_g)
    lo = best_g * EXPERTS_PER_GROUP
    in_best = (lane >= lo) & (lane < lo + EXPERTS_PER_GROUP)
    cand = jnp.where(in_best, probs, -1.0)
    p1 = jnp.max(cand, axis=-1, keepdims=True)
    i1 = jnp.min(jnp.where(cand == p1, lane, LANES), axis=-1, keepdims=True)
    cand2 = jnp.where(lane == i1, -2.0, cand)
    p2 = jnp.max(cand2, axis=-1, keepdims=True)
    i2 = jnp.min(jnp.where(cand2 == p2, lane, LANES), axis=-1, keepdims=True)
    tot = p1 + p2
    idx = jnp.where(lane == 0, i1, jnp.where(lane == 1, i2, 0))
    gate = jnp.where(lane == 0, p1 / tot, jnp.where(lane == 1, p2 / tot, 0.0))
    return idx, gate


def _row_part_specs(parts, tm, d):
    specs, firsts, t0 = [], [], 0
    for p in parts:
        nt = p.shape[0] // tm
        specs.append(pl.BlockSpec((tm, d), lambda i, *_, t0=t0, nt=nt: (jnp.clip(i - t0, 0, nt - 1), 0)))
        firsts.append(t0)
        t0 += nt
    return specs, firsts


def _select_part(i, refs, firsts, store):
    for k, ref in enumerate(refs):
        hi = firsts[k + 1] if k + 1 < len(refs) else None
        cond = i >= firsts[k] if hi is None else (i >= firsts[k]) & (i < hi)

        @pl.when(cond)
        def _(ref=ref):
            store(ref[...])


def _ln_router_kernel(*refs, firsts):
    n = len(firsts)
    x_refs = refs[:n]
    mix_ref, g_ref, b_ref, rw_ref, rb_ref, xo_ref, idx_ref, gate_ref = refs[n:]

    def finish(x):
        y = _layer_norm(DEEPNORM_ALPHA * x + mix_ref[...], g_ref[...], b_ref[...])
        xo_ref[...] = y
        idx, gate = _route_rows(y.astype(BF16), rw_ref[...], rb_ref[...])
        idx_ref[...] = idx
        gate_ref[...] = gate

    _select_part(pl.program_id(0), x_refs, firsts, finish)


def ln_router(x_parts, mix, g, b, rw, rb):
    t, d = mix.shape
    tm = LN_TM
    x_specs, firsts = _row_part_specs(x_parts, tm, d)
    row = pl.BlockSpec((tm, d), lambda i: (i, 0))
    vec = pl.BlockSpec((1, d), lambda i: (0, 0))
    small = pl.BlockSpec((tm, LANES), lambda i: (i, 0))
    return pl.pallas_call(
        functools.partial(_ln_router_kernel, firsts=tuple(firsts)),
        out_shape=(jax.ShapeDtypeStruct((t, d), F32),
                   jax.ShapeDtypeStruct((t, LANES), I32), jax.ShapeDtypeStruct((t, LANES), F32)),
        grid=(t // tm,),
        in_specs=x_specs + [row, vec, vec, pl.BlockSpec((d, LANES), lambda i: (0, 0)),
                            pl.BlockSpec((1, LANES), lambda i: (0, 0))],
        out_specs=(row, small, small),
        compiler_params=_cparams(("parallel",)),
        name="ln_router",
    )(*x_parts, mix, g, b, rw, rb)


def _cast_rows_kernel(*refs, firsts):
    o_ref = refs[-1]

    def store(v):
        o_ref[...] = v.astype(o_ref.dtype)

    _select_part(pl.program_id(0), refs[:-1], firsts, store)


def cast_rows(x_parts, dtype, tm):
    d = x_parts[0].shape[1]
    t = sum(p.shape[0] for p in x_parts)
    x_specs, firsts = _row_part_specs(x_parts, tm, d)
    return pl.pallas_call(
        functools.partial(_cast_rows_kernel, firsts=tuple(firsts)),
        out_shape=jax.ShapeDtypeStruct((t, d), dtype),
        grid=(t // tm,),
        in_specs=x_specs,
        out_specs=pl.BlockSpec((tm, d), lambda i: (i, 0)),
        compiler_params=_cparams(("parallel",)),
        name="cast_rows",
    )(*x_parts)


def _cast_kernel(x_ref, o_ref):
    o_ref[...] = x_ref[...].astype(o_ref.dtype)


def cast_weights(w, dtype, rows_per_block):
    nl, ne, r, c = w.shape
    spec = pl.BlockSpec((1, 1, rows_per_block, c), lambda l, e, j: (l, e, j, 0))
    return pl.pallas_call(
        _cast_kernel,
        out_shape=jax.ShapeDtypeStruct(w.shape, dtype),
        grid=(nl, ne, r // rows_per_block),
        in_specs=[spec],
        out_specs=spec,
        compiler_params=_cparams(("parallel", "parallel", "parallel")),
        name="cast_weights",
    )(w)


def _row_gather_start(src_hbm, row, dst, dst_row, sem, priority=0):
    pltpu.make_async_copy(src_hbm.at[pl.ds(row, 1)], dst.at[pl.ds(dst_row, 1)], sem).start(priority=priority)


def _slot_wait(buf, slot, sem):
    pltpu.make_async_copy(buf.at[slot], buf.at[slot], sem.at[slot]).wait()


def _moe_up_kernel(te_ref, pos_ref, nv_ref, pad_ref, x_hbm, wg_ref, wu_ref, h_ref, xbuf, sem, src_ref):
    del te_ref
    i = pl.program_id(0)
    slot = lax.rem(i, 2)
    nvalid = nv_ref[0]

    @pl.when(i == 0)
    def _():
        def clear(r, carry):
            src_ref[r] = 0
            return carry
        for e in range(N_EXPERTS + 1):
            lax.fori_loop(pad_ref[e], pad_ref[N_EXPERTS + 1 + e], clear, 0)

        def put(s, carry):
            src_ref[pos_ref[s]] = lax.shift_right_logical(s, 1)
            return carry
        lax.fori_loop(0, pos_ref.shape[0], put, 0, unroll=8)

        def body(r, carry):
            _row_gather_start(x_hbm, src_ref[r], xbuf.at[0], r, sem.at[0])
            return carry
        lax.fori_loop(0, MOE_TM, body, 0)

    @pl.when(i <= nvalid)
    def _():
        _slot_wait(xbuf, slot, sem)

    @pl.when(i < nvalid)
    def _():
        base = (i + 1) * MOE_TM
        for r in range(MOE_TM):
            _row_gather_start(x_hbm, src_ref[base + r], xbuf.at[1 - slot], r, sem.at[1 - slot], priority=r % 2)
        xb = xbuf[slot].astype(BF16)
        gate = _dot(xb, wg_ref[0, 0])
        up = _dot(xb, wu_ref[0, 0])
        h_ref[...] = (_silu(gate) * up).astype(BF16)

    @pl.when(i >= nvalid)
    def _():
        h_ref[...] = jnp.zeros_like(h_ref)


def moe_up(tile_expert, pos, nvalid, pad_ranges, x, wg, wu, layer):
    d = x.shape[1]
    assert (2 * x.shape[0] + N_EXPERTS * (MOE_TM - 1)) // MOE_TM < MOE_TILES
    wspec = pl.BlockSpec((1, 1, d, D_EXPERT), lambda i, te, pos, nv, pad: (layer, te[i], 0, 0))
    grid_spec = pltpu.PrefetchScalarGridSpec(
        num_scalar_prefetch=4,
        grid=(MOE_TILES,),
        in_specs=[pl.BlockSpec(memory_space=pl.ANY), wspec, wspec],
        out_specs=pl.BlockSpec((MOE_TM, D_EXPERT), lambda i, te, pos, nv, pad: (i, 0)),
        scratch_shapes=[pltpu.VMEM((2, MOE_TM, d), F32), pltpu.SemaphoreType.DMA((2,)),
                        pltpu.SMEM((MOE_ROWS,), I32)],
    )
    return pl.pallas_call(
        _moe_up_kernel,
        out_shape=jax.ShapeDtypeStruct((MOE_ROWS, D_EXPERT), BF16),
        grid_spec=grid_spec,
        compiler_params=_cparams(("arbitrary",)),
        name="moe_up",
    )(tile_expert, pos, nvalid, pad_ranges, x, wg, wu)


def _moe_down_kernel(te_ref, nv_ref, h_ref, wd_ref, y_ref, wbf):
    i = pl.program_id(0)
    new_expert = (i == 0) | (te_ref[i] != te_ref[jnp.maximum(i - 1, 0)])

    @pl.when(new_expert & (i < nv_ref[0]))
    def _():
        wbf[...] = wd_ref[0, 0].astype(BF16)

    @pl.when(i < nv_ref[0])
    def _():
        y_ref[...] = _dot(h_ref[...], wbf[...])

    @pl.when(i >= nv_ref[0])
    def _():
        y_ref[...] = jnp.zeros_like(y_ref)


def moe_down(tile_expert, nvalid, h, wd, layer):
    d = wd.shape[3]
    grid_spec = pltpu.PrefetchScalarGridSpec(
        num_scalar_prefetch=2,
        grid=(MOE_TILES,),
        in_specs=[pl.BlockSpec((MOE_TM, D_EXPERT), lambda i, te, nv: (i, 0)),
                  pl.BlockSpec((1, 1, D_EXPERT, d), lambda i, te, nv: (layer, te[i], 0, 0))],
        out_specs=pl.BlockSpec((MOE_TM, d), lambda i, te, nv: (i, 0)),
        scratch_shapes=[pltpu.VMEM((D_EXPERT, d), BF16)],
    )
    return pl.pallas_call(
        _moe_down_kernel,
        out_shape=jax.ShapeDtypeStruct((MOE_ROWS, d), F32),
        grid_spec=grid_spec,
        compiler_params=_cparams(("arbitrary",)),
        name="moe_down",
    )(tile_expert, nvalid, h, wd)


def _combine_ln_kernel(pos_ref, x_ref, gsel_ref, y_hbm, g_ref, b_ref, *rest, tm, tile0, ntiles, with_router):
    if with_router:
        rw_ref, rb_ref, xo_ref, xb_ref, idx_ref, gate_ref, ybuf, sem = rest
    else:
        xo_ref, ybuf, sem = rest
    i = pl.program_id(0)
    slot = lax.rem(i, 2)

    @pl.when(i == 0)
    def _():
        def body(r, carry):
            _row_gather_start(y_hbm, pos_ref[2 * tile0 * tm + 2 * r], ybuf.at[0, 0], r, sem.at[0])
            _row_gather_start(y_hbm, pos_ref[2 * tile0 * tm + 2 * r + 1], ybuf.at[0, 1], r, sem.at[0])
            return carry
        lax.fori_loop(0, tm, body, 0)

    _slot_wait(ybuf, slot, sem)
    base = 2 * (tile0 + jnp.minimum(i + 1, ntiles - 1)) * tm
    for r in range(tm):
        _row_gather_start(y_hbm, pos_ref[base + 2 * r], ybuf.at[1 - slot, 0], r, sem.at[1 - slot], priority=0)
        _row_gather_start(y_hbm, pos_ref[base + 2 * r + 1], ybuf.at[1 - slot, 1], r, sem.at[1 - slot], priority=1)
    gsel = gsel_ref[...]
    ffn = gsel[:, 0:1] * ybuf[slot, 0] + gsel[:, 1:2] * ybuf[slot, 1]
    y = _layer_norm(DEEPNORM_ALPHA * x_ref[...] + ffn, g_ref[...], b_ref[...])
    xo_ref[...] = y
    if with_router:
        yb = y.astype(BF16)
        xb_ref[...] = yb
        idx, gate = _route_rows(yb, rw_ref[...], rb_ref[...])
        idx_ref[...] = idx
        gate_ref[...] = gate

    @pl.when(i == ntiles - 1)
    def _():
        _slot_wait(ybuf, 1 - slot, sem)


def combine_ln(pos, x, gate_sel, y_sorted, g, b, *, row0=0, nrows=None, router=None):
    d = x.shape[1]
    tm = LN_TM
    nrows = x.shape[0] if nrows is None else nrows
    tile0 = row0 // tm
    ntiles = nrows // tm
    row_in = pl.BlockSpec((tm, d), lambda i, p: (tile0 + i, 0))
    row_out = pl.BlockSpec((tm, d), lambda i, p: (i, 0))
    vec = pl.BlockSpec((1, d), lambda i, p: (0, 0))
    small = pl.BlockSpec((tm, LANES), lambda i, p: (i, 0))
    in_specs = [row_in, pl.BlockSpec((tm, LANES), lambda i, p: (tile0 + i, 0)),
                pl.BlockSpec(memory_space=pl.ANY), vec, vec]
    args = [pos, x, gate_sel, y_sorted, g, b]
    out_shape = [jax.ShapeDtypeStruct((nrows, d), F32)]
    out_specs = [row_out]
    if router is not None:
        in_specs += [pl.BlockSpec((d, LANES), lambda i, p: (0, 0)), pl.BlockSpec((1, LANES), lambda i, p: (0, 0))]
        args += list(router)
        out_shape += [jax.ShapeDtypeStruct((nrows, d), BF16), jax.ShapeDtypeStruct((nrows, LANES), I32),
                      jax.ShapeDtypeStruct((nrows, LANES), F32)]
        out_specs += [row_out, small, small]
    grid_spec = pltpu.PrefetchScalarGridSpec(
        num_scalar_prefetch=1,
        grid=(ntiles,),
        in_specs=in_specs,
        out_specs=tuple(out_specs),
        scratch_shapes=[pltpu.VMEM((2, 2, tm, d), F32), pltpu.SemaphoreType.DMA((2,))],
    )
    kern = functools.partial(_combine_ln_kernel, tm=tm, tile0=tile0, ntiles=ntiles,
                             with_router=router is not None)
    return pl.pallas_call(
        kern,
        out_shape=tuple(out_shape),
        grid_spec=grid_spec,
        compiler_params=_cparams(("arbitrary",)),
        name="combine_ln",
    )(*args)


def _route_tables(idx):
    t = idx.shape[0]
    e_flat = idx[:, :2].reshape(-1)
    onehot = (e_flat[:, None] == jnp.arange(N_EXPERTS, dtype=I32)[None, :]).astype(I32)
    csum = jnp.cumsum(onehot, axis=0)
    rank = jnp.take_along_axis(csum, e_flat[:, None], axis=1)[:, 0] - 1
    counts = csum[-1]
    padded = ((counts + MOE_TM - 1) // MOE_TM) * MOE_TM
    ends = jnp.cumsum(padded)
    starts = ends - padded
    pos = (starts[e_flat] + rank).astype(I32)
    nvalid = (ends[-1] // MOE_TM).astype(I32)
    tile_start = jnp.arange(MOE_TILES, dtype=I32) * MOE_TM
    te = jnp.sum((tile_start[:, None] >= ends[None, :]).astype(I32), axis=1)
    te = jnp.minimum(te, te[nvalid - 1]).astype(I32)
    pad_lo = jnp.concatenate([starts + counts, ends[-1:]])
    pad_hi = jnp.concatenate([ends, ends[-1:] + MOE_TM])
    return pos, te, nvalid.reshape(1), jnp.concatenate([pad_lo, pad_hi]).astype(I32)


def moe_layer(x_f32, idx, wg, wu, wd, layer):
    pos, te, nvalid, pad_ranges = _route_tables(idx)
    h = moe_up(te, pos, nvalid, pad_ranges, x_f32, wg, wu, layer)
    y_sorted = moe_down(te, nvalid, h, wd, layer)
    return pos, y_sorted


def _conv_silu(x, prev8, w, b):
    row8 = lax.broadcasted_iota(I32, prev8.shape, 0)
    acc = jnp.broadcast_to(b, x.shape)
    for k in range(SSD_CONV_W):
        s = SSD_CONV_W - 1 - k
        if s == 0:
            xs = x
        else:
            xs = pltpu.roll(x, s, axis=0)
            top = jnp.where(row8 < s, pltpu.roll(prev8, s, axis=0), xs[:SUBLANES])
            xs = jnp.concatenate([top, xs[SUBLANES:]], axis=0)
        acc = acc + xs * w[k:k + 1, :]
    return _silu(acc)


def _dt_kernel(dt_ref, bias_ref, alog_ref, dt_o, dta_o):
    v = dt_ref[...] + bias_ref[...]
    sp = jnp.maximum(v, 0.0) + jnp.log1p(jnp.exp(-jnp.abs(v)))
    dt_o[...] = sp
    dta_o[...] = sp * (-jnp.exp(alog_ref[...]))


def dt_prep(dt_raw, bias, a_log):
    t, h = dt_raw.shape
    tr = t // 8
    row = pl.BlockSpec((tr, h), lambda i: (i, 0))
    vec = pl.BlockSpec((1, h), lambda i: (0, 0))
    return pl.pallas_call(
        _dt_kernel,
        out_shape=(jax.ShapeDtypeStruct((t, h), F32), jax.ShapeDtypeStruct((t, h), F32)),
        grid=(8,),
        in_specs=[row, vec, vec],
        out_specs=(row, row),
        compiler_params=_cparams(("parallel",)),
        name="dt_prep",
    )(dt_raw, bias, a_log)


def _ssd_kernel(x_ref, b_ref, c_ref, px_ref, pb_ref, pc_ref, wx_ref, wb_ref, wc_ref, bx_ref, bb_ref, bc_ref,
                z_ref, dt_ref, dta_ref, dtt_ref, dtat_ref, h0_ref, dskip_ref, nw_ref,
                y_ref, hout_ref, state, halo_x, halo_b, halo_c, *, q, nchunks):
    c = pl.program_id(2)
    gc = SSD_GROUP_CH
    n = SSD_D_STATE
    streams = ((x_ref, px_ref, wx_ref, bx_ref, halo_x), (b_ref, pb_ref, wb_ref, bb_ref, halo_b),
               (c_ref, pc_ref, wc_ref, bc_ref, halo_c))

    @pl.when(c == 0)
    def _():
        for k in range(SSD_GROUPS_PER_STEP):
            state[k] = h0_ref[0, k].T
        for _, prev_ref, _, _, halo in streams:
            halo[...] = prev_ref[0]

    acts = []
    for raw_ref, _, w_ref, bias_ref, halo in streams:
        raw = raw_ref[...]
        acts.append(_conv_silu(raw, halo[...], w_ref[...], bias_ref[...]))
        halo[...] = raw[q - SUBLANES:, :]
    xa, ba, ca = acts

    new_states = []
    for k in range(SSD_GROUPS_PER_STEP):
        cols = slice(k * gc, (k + 1) * gc)
        y, s_new = _ssd_group(
            xa[:, cols], ba[:, k * n:(k + 1) * n], ca[:, k * n:(k + 1) * n], z_ref[:, cols],
            dt_ref[k, 0], dta_ref[k, 0], dtt_ref[k, 0], dtat_ref[k, 0], state[k],
            dskip_ref[:, cols], nw_ref[:, cols], q=q)
        y_ref[:, cols] = y
        state[k] = s_new
        new_states.append(s_new)

    @pl.when(c == nchunks - 1)
    def _():
        for k in range(SSD_GROUPS_PER_STEP):
            hout_ref[0, k] = new_states[k].T


def _ssd_group(x, bm, cm, z, dt, dta, dtt, dtat, s_old, dskip, nw, *, q):
    hg = SSD_HEADS_PER_GROUP
    p = SSD_HEAD_DIM
    xb = x.astype(BF16)
    bm = bm.astype(BF16)
    cm = cm.astype(BF16)

    ri = lax.broadcasted_iota(I32, (q, q), 0)
    ci = lax.broadcasted_iota(I32, (q, q), 1)
    causal = ri >= ci
    lower = jnp.where(causal, 1.0, 0.0).astype(BF16)
    upper = jnp.where(ri <= ci, 1.0, 0.0).astype(BF16)
    log2e = math.log2(math.e)
    acum = _dot_sel(lower, dta) * log2e
    acumt = _dot_sel_r(dtat, upper) * log2e
    alast = acum[q - 1:q, :]

    cb = _dot_nt(cm, bm)
    lane = lax.broadcasted_iota(I32, (q, 2 * p), 1)
    y_pairs = []
    for hp in range(hg // 2):
        ms = []
        for h in (2 * hp, 2 * hp + 1):
            seg = acum[:, h:h + 1] - acumt[h:h + 1, :]
            decay = jnp.exp2(jnp.where(causal, seg, -jnp.inf))
            ms.append((cb * decay * dtt[h:h + 1, :]).astype(BF16))
        lhs = jnp.concatenate(ms, axis=1)
        xp = xb[:, 2 * p * hp:2 * p * (hp + 1)]
        zero = jnp.zeros_like(xp)
        rhs = jnp.concatenate([jnp.where(lane < p, xp, zero), jnp.where(lane >= p, xp, zero)], axis=0)
        y_pairs.append(_dot(lhs, rhs))
    y = jnp.concatenate(y_pairs, axis=1)

    hi = lax.broadcasted_iota(I32, (hg, hg * p), 0)
    li = lax.broadcasted_iota(I32, (hg, hg * p), 1)
    widen = jnp.where((li >= hi * p) & (li < (hi + 1) * p), 1.0, 0.0).astype(BF16)
    scales = jnp.concatenate([jnp.exp2(acum), jnp.exp2(alast - acum) * dt,
                              jnp.broadcast_to(jnp.exp2(alast), (SUBLANES, hg))], axis=0)
    wide = _dot_sel_r2(scales, widen)
    e_in = wide[:q]
    e_out = wide[q:2 * q]
    e_all = wide[2 * q:2 * q + 1]

    y = y + _dot(cm, s_old.astype(BF16)) * e_in
    xw = (x * e_out).astype(BF16)
    s_new = s_old * e_all + _dot_tn(bm, xw)

    y = y + dskip * x
    y = y * _silu(z)
    y = y * lax.rsqrt(jnp.mean(y * y, axis=-1, keepdims=True) + RMS_EPS)
    return (y * nw).astype(BF16), s_new


def ssd_scan(xbc, conv_prev, conv_w, conv_b, z, dt4, dta4, dtt4, dtat4, h0, dskip, norm_w, *,
             row0, nseq, seqlen, q, out_alias=None):
    t = xbc.shape[0]
    gps = SSD_GROUPS_PER_STEP
    g = SSD_N_GROUPS
    gc = gps * SSD_GROUP_CH
    n = SSD_D_STATE
    nchunks = seqlen // q
    rb0 = row0 // q
    bcol0 = SSD_D_INNER // (gps * n)
    ccol0 = (SSD_D_INNER + SSD_BC_DIM) // (gps * n)

    def rows(b, gg, c):
        return rb0 + b * nchunks + c

    def col_specs(shape_of, index_of):
        return [pl.BlockSpec(shape_of(gc), index_of(0)), pl.BlockSpec(shape_of(gps * n), index_of(bcol0)),
                pl.BlockSpec(shape_of(gps * n), index_of(ccol0))]

    in_specs = (col_specs(lambda w: (q, w), lambda c0: lambda b, gg, c: (rows(b, gg, c), c0 + gg))
                + col_specs(lambda w: (1, SUBLANES, w), lambda c0: lambda b, gg, c: (b, 0, c0 + gg))
                + col_specs(lambda w: (SSD_CONV_W, w), lambda c0: lambda b, gg, c: (0, c0 + gg))
                + col_specs(lambda w: (1, w), lambda c0: lambda b, gg, c: (0, c0 + gg)))
    in_specs += [pl.BlockSpec((q, gc), lambda b, gg, c: (rows(b, gg, c), gg)),
                pl.BlockSpec((gps, 1, q, 16), lambda b, gg, c: (gg, b * nchunks + c, 0, 0)),
                pl.BlockSpec((gps, 1, q, 16), lambda b, gg, c: (gg, b * nchunks + c, 0, 0)),
                pl.BlockSpec((gps, 1, 16, q), lambda b, gg, c: (gg, b * nchunks + c, 0, 0)),
                pl.BlockSpec((gps, 1, 16, q), lambda b, gg, c: (gg, b * nchunks + c, 0, 0)),
                pl.BlockSpec((1, gps, SSD_GROUP_CH, n), lambda b, gg, c: (b, gg, 0, 0)),
                pl.BlockSpec((1, gc), lambda b, gg, c: (0, gg)),
                pl.BlockSpec((1, gc), lambda b, gg, c: (0, gg))]
    args = [xbc] * 3 + [conv_prev] * 3 + [conv_w] * 3 + [conv_b] * 3 + [z, dt4, dta4, dtt4, dtat4, h0, dskip, norm_w]
    nin = len(args)
    kern = functools.partial(_ssd_kernel, q=q, nchunks=nchunks)
    aliases = {}
    if out_alias is not None:
        in_specs.append(pl.BlockSpec(memory_space=pl.ANY))
        args.append(out_alias)
        aliases = {nin: 0}
        base = kern
        kern = lambda *refs: base(*refs[:nin], *refs[nin + 1:])
    return pl.pallas_call(
        kern,
        out_shape=(jax.ShapeDtypeStruct((t, SSD_D_INNER), BF16),
                   jax.ShapeDtypeStruct((nseq, g, SSD_GROUP_CH, n), F32)),
        grid=(nseq, g // gps, nchunks),
        in_specs=in_specs,
        out_specs=(pl.BlockSpec((q, gc), lambda b, gg, c: (rows(b, gg, c), gg)),
                   pl.BlockSpec((1, gps, SSD_GROUP_CH, n), lambda b, gg, c: (b, gg, 0, 0))),
        scratch_shapes=[pltpu.VMEM((gps, n, SSD_GROUP_CH), F32), pltpu.VMEM((SUBLANES, gc), F32),
                        pltpu.VMEM((SUBLANES, gps * n), F32), pltpu.VMEM((SUBLANES, gps * n), F32)],
        input_output_aliases=aliases,
        compiler_params=_cparams(("parallel", "parallel", "arbitrary")),
        name="ssd_scan",
    )(*args)


def _dt_layouts(v, row0, nseq, seqlen, q):
    nb = nseq * seqlen // q
    part = v[row0:row0 + nseq * seqlen].reshape(nb, q, SSD_N_GROUPS, SSD_HEADS_PER_GROUP)
    return part.transpose(2, 0, 1, 3), part.transpose(2, 0, 3, 1)


def _s5_prep_kernel(lam_ref, row_ref, bcat_ref, bswap_ref, ccat_ref, cswap_ref, wst_ref, vt_ref, krev_ref, aq_ref):
    n = S5_STATE
    sp = S5_PACK_STATE
    lr, li, ls = lam_ref[0, 0], lam_ref[0, 1], lam_ref[0, 2]
    lane = lax.broadcasted_iota(I32, (LANES, LANES), 1)
    first = lane < n
    dstep = jnp.exp(ls)
    mag = jnp.exp(lr * dstep)
    ar = mag * jnp.cos(li * dstep)
    ai = mag * jnp.sin(li * dstep)
    den = lr * lr + li * li
    cr = ((ar - 1.0) * lr + ai * li) / den
    ci = (ai * lr - (ar - 1.0) * li) / den
    ci_s = jnp.where(first, -ci, ci)
    bcat, bswap = bcat_ref[0], bswap_ref[0]
    bb = cr * bcat + ci_s * bswap
    bb_swap = cr * bswap - ci_s * bcat
    ccat, cswap = ccat_ref[0], cswap_ref[0]

    row_e = lax.broadcasted_iota(I32, (LANES, 2 * sp), 0)
    col_e = lax.broadcasted_iota(I32, (LANES, 2 * sp), 1)
    target = jnp.where(col_e >= sp, n, 0) + (col_e & (n - 1))
    place = jnp.where(row_e == target, 1.0, 0.0).astype(BF16)
    own = lax.shift_right_logical(row_e, 4) == lax.shift_right_logical(col_e & (sp - 1), 6)
    row_k = lax.broadcasted_iota(I32, (LANES, LANES), 0)
    own_k = lax.shift_right_logical(row_k, 4) == lax.shift_right_logical(lane, 4)

    def expand(blk):
        return jnp.where(own, _dot(blk.astype(BF16), place), 0.0).astype(BF16)

    pr = jnp.ones((LANES, LANES), F32)
    pi = jnp.zeros((LANES, LANES), F32)
    outs = []
    for m in range(S5_Q + 1):
        pr_m = jnp.where(first, pr, -pr)
        outs.append(pr_m * ccat - pi * cswap)
        if m < S5_Q:
            t = S5_Q - 1 - m
            pi_s = jnp.where(first, -pi, pi)
            wst_ref[0, t * LANES:(t + 1) * LANES, :] = expand(pr * bb + pi_s * bb_swap)
        if m >= 1:
            vt_ref[0, (m - 1) * LANES:m * LANES, :] = expand(outs[m])
        pr, pi = pr * ar - pi * ai, pr * ai + pi * ar

    taps = jnp.concatenate([outs[S5_Q - 1 - t] for t in range(S5_Q)], axis=0)
    kk = lax.dot_general(bb, taps, (((1,), (1,)), ((), ())),
                         precision=lax.Precision.HIGHEST, preferred_element_type=F32)
    for t in range(S5_Q):
        blk = kk[:, t * LANES:(t + 1) * LANES]
        krev_ref[0, t * LANES:(t + 1) * LANES, :] = jnp.where(own_k, blk, 0.0).astype(BF16)

    rlr, rli, rls = row_ref[0, 0:1, :], row_ref[0, 1:2, :], row_ref[0, 2:3, :]
    rstep = jnp.exp(rls)
    e16 = jnp.exp((float(S5_Q) * rlr) * rstep)
    ang = (float(S5_Q) * rli) * rstep
    aq_ref[0] = jnp.concatenate([e16 * jnp.cos(ang), e16 * jnp.sin(ang)], axis=1)


def s5_prep(lam_re, lam_im, log_step, b_re, b_im, c_re, c_im):
    np_ = S5_N_PACKS
    sp = S5_PACK_STATE

    def compact(m):
        return m.reshape(np_, LANES, S5_STATE)

    def per_row(v):
        return compact(jnp.broadcast_to(v[:, None, :], (S5_N_GROUPS, S5_GROUP_CH, S5_STATE)))

    def two(a, b):
        return jnp.concatenate([a, b], axis=-1)

    step2d = jnp.broadcast_to(log_step[:, None], (S5_N_GROUPS, S5_STATE))
    lam = jnp.stack([two(per_row(v), per_row(v)) for v in (lam_re, lam_im, step2d)], axis=1)
    rows = jnp.stack([v.reshape(np_, sp) for v in (lam_re, lam_im, step2d)], axis=1)
    bre, bim = compact(b_re.transpose(0, 2, 1)), compact(b_im.transpose(0, 2, 1))
    cre, cim = compact(c_re), compact(c_im)
    mat = pl.BlockSpec((1, LANES, LANES), lambda i: (i, 0, 0))
    big = pl.BlockSpec((1, S5_Q * LANES, 2 * sp), lambda i: (i, 0, 0))
    return pl.pallas_call(
        _s5_prep_kernel,
        out_shape=(jax.ShapeDtypeStruct((np_, S5_Q * LANES, 2 * sp), BF16),
                   jax.ShapeDtypeStruct((np_, S5_Q * LANES, 2 * sp), BF16),
                   jax.ShapeDtypeStruct((np_, S5_Q * LANES, LANES), BF16),
                   jax.ShapeDtypeStruct((np_, 1, 2 * sp), F32)),
        grid=(np_,),
        in_specs=[pl.BlockSpec((1, 3, LANES, LANES), lambda i: (i, 0, 0, 0)),
                  pl.BlockSpec((1, 3, sp), lambda i: (i, 0, 0)), mat, mat, mat, mat],
        out_specs=(big, big, pl.BlockSpec((1, S5_Q * LANES, LANES), lambda i: (i, 0, 0)),
                   pl.BlockSpec((1, 1, 2 * sp), lambda i: (i, 0, 0))),
        compiler_params=_cparams(("parallel",)),
        name="s5_prep",
    )(lam, rows, two(bre, bim), two(bim, bre), two(cre, cim), two(cim, cre))


def _s5_kernel(x_ref, wst_ref, vt_ref, krev_ref, aq_ref, d_ref, h0_ref, g_ref, hout_ref, *scratch,
               nblk, nseq, chain):
    q = S5_Q
    sp = S5_PACK_STATE
    us = [x_ref[pl.ds(t, nblk, stride=q), :] for t in range(q)]
    ucat = jnp.concatenate([u.astype(BF16) for u in us], axis=1)
    s_in = _dot(ucat, wst_ref[0])
    aq = aq_ref[0]
    aqr, aqi = aq[:, :sp], aq[:, sp:]

    def advance(h, s):
        hr, hi = h[:, :sp], h[:, sp:]
        return jnp.concatenate([aqr * hr - aqi * hi + s[:, :sp], aqr * hi + aqi * hr + s[:, sp:]], axis=1)

    tmat = scratch[-1]

    @pl.when(pl.program_id(1) == 0)
    def _():
        for s in range(q):
            for t in range(s, q):
                tmat[s * LANES:(s + 1) * LANES, t * LANES:(t + 1) * LANES] = (
                    krev_ref[0, (q - 1 - (t - s)) * LANES:(q - (t - s)) * LANES, :])
        for p in range(q // 2):
            tmat[(2 * p + 1) * LANES:(2 * p + 2) * LANES, 2 * p * LANES:(2 * p + 1) * LANES] = (
                jnp.zeros((LANES, LANES), BF16))

    if chain:
        hin_s, s_s, y_s = scratch[:3]
        per_seq = nblk // nseq
        s_s[...] = s_in

        def body(c, hs):
            out = []
            for s in range(nseq):
                row = s * per_seq + c
                hin_s[pl.ds(row, 1), :] = hs[s]
                out.append(advance(hs[s], s_s[pl.ds(row, 1), :]))
            return tuple(out)

        h0 = h0_ref[0, 0]
        hs = lax.fori_loop(0, per_seq, body, tuple(h0[s:s + 1] for s in range(nseq)), unroll=8)
        hout_ref[0, 0] = jnp.concatenate(hs, axis=0)
        hin = hin_s[...]
    else:
        y_s = scratch[0]
        hin = h0_ref[0, 0]
        hout_ref[0, 0] = advance(hin, s_in)
    hinb = hin.astype(BF16)
    d = d_ref[...]
    for p in range(q // 2):
        c0, c1 = 2 * p * LANES, (2 * p + 2) * LANES
        y2 = _dot_nt(hinb, vt_ref[0, c0:c1, :])
        y2 = y2 + _dot(ucat[:, :c1], tmat[:c1, c0:c1])
        for j in range(2):
            t = 2 * p + j
            y = y2[:, j * LANES:(j + 1) * LANES] + d * us[t]
            y_s[pl.ds(t, nblk, stride=q), :] = _gelu_exact(y)
    g_ref[...] = y_s[...].astype(BF16)


def s5_scan(x, wst, vt, krev, aq, dskip, h0, *, row0, nseq, seqlen, chain, out_alias=None):
    t, d = x.shape
    q = S5_Q
    sp2 = 2 * S5_PACK_STATE
    spb = S5_CHAIN_SEQS if chain else nseq
    nstep = nseq // spb
    rows = spb * seqlen
    nblk = rows // q
    assert chain or seqlen == q
    grid = (S5_N_PACKS, nstep)
    rb0 = row0 // rows
    xmap = lambda p, b: (rb0 + b, p)
    wmap = lambda p, b: (p, 0, 0)
    dmap = lambda p, b: (0, p)
    hspec = pl.BlockSpec((1, 1, spb, sp2), lambda p, b: (p, b, 0, 0))
    hshape = (S5_N_PACKS, nstep, spb, sp2)
    scratch = [pltpu.VMEM((rows, LANES), F32), pltpu.VMEM((q * LANES, q * LANES), BF16)]
    if chain:
        scratch = [pltpu.VMEM((nblk, sp2), F32), pltpu.VMEM((nblk, sp2), F32)] + scratch
    sem = ("parallel", "arbitrary")
    in_specs = [pl.BlockSpec((rows, LANES), xmap),
                pl.BlockSpec((1, q * LANES, sp2), wmap),
                pl.BlockSpec((1, q * LANES, sp2), wmap),
                pl.BlockSpec((1, q * LANES, LANES), wmap),
                pl.BlockSpec((1, 1, sp2), wmap),
                pl.BlockSpec((1, LANES), dmap),
                hspec]
    args = [x, wst, vt, krev, aq, dskip, h0]
    kern = functools.partial(_s5_kernel, nblk=nblk, nseq=spb, chain=chain)
    aliases = {}
    if out_alias is not None:
        in_specs.append(pl.BlockSpec(memory_space=pl.ANY))
        args.append(out_alias)
        aliases = {len(args) - 1: 0}
        base = kern
        kern = lambda *refs: base(*refs[:7], *refs[8:])
    return pl.pallas_call(
        kern,
        out_shape=(jax.ShapeDtypeStruct((t, d), BF16), jax.ShapeDtypeStruct(hshape, F32)),
        grid=grid,
        in_specs=in_specs,
        out_specs=(pl.BlockSpec((rows, LANES), xmap), hspec),
        scratch_shapes=scratch,
        input_output_aliases=aliases,
        compiler_params=_cparams(sem),
        name="s5_scan",
    )(*args)


def _s5_pack_state(re, im):
    b = re.shape[0]
    return jnp.concatenate([re.reshape(b, S5_N_PACKS, S5_PACK_STATE), im.reshape(b, S5_N_PACKS, S5_PACK_STATE)], axis=-1)


def _s5_unpack_state(h):
    b = h.shape[0]
    return (h[..., :S5_PACK_STATE].reshape(b, S5_N_GROUPS, S5_STATE),
            h[..., S5_PACK_STATE:].reshape(b, S5_N_GROUPS, S5_STATE))


def kernel(x_prompt, x_sample, cache_ssd_conv, state_ssd, state_s5_re, state_s5_im, ssd_w_in, ssd_conv_w, ssd_conv_b, ssd_dt_bias, ssd_a_log, ssd_d, ssd_norm_w, ssd_w_out, s5_lambda_re, s5_lambda_im, s5_log_step, s5_b_re, s5_b_im, s5_c_re, s5_c_im, s5_d, s5_w_glu, ln_mix_g, ln_mix_b, ln_ffn_g, ln_ffn_b, router_w, router_b, moe_w_gate, moe_w_up, moe_w_down):
    d = D_MODEL
    x0_parts = [x_prompt.reshape(T_PROMPT, d), x_sample.reshape(T_SAMPLE, d)]
    x0b = cast_rows(x0_parts, BF16, tm=T_SAMPLE)
    rw = jnp.zeros((d, LANES), BF16).at[:, :N_EXPERTS].set(router_w.astype(BF16))
    rb = jnp.zeros((1, LANES), F32).at[0, :N_EXPERTS].set(router_b)

    w_in = ssd_w_in[0].astype(BF16)
    z = matmul(x0b, w_in, tm=MM_TM, tn=1024, col0=0, ncols=SSD_D_INNER)
    xbc = matmul(x0b, w_in, tm=MM_TM, tn=1024, col0=SSD_D_INNER, ncols=SSD_CONV_DIM)
    dt_raw = matmul(x0b, w_in, tm=MM_TM, tn=LANES, col0=SSD_D_INNER + SSD_CONV_DIM, ncols=SSD_N_HEADS)

    conv_w = ssd_conv_w[0]
    conv_b = ssd_conv_b[0].reshape(1, SSD_CONV_DIM)
    prev_p = jnp.zeros((BATCH, SUBLANES, SSD_CONV_DIM), F32)
    prev_s = jnp.concatenate([jnp.zeros((DEC_BATCH, SUBLANES - (SSD_CONV_W - 1), SSD_CONV_DIM), F32),
                              cache_ssd_conv[0]], axis=1)

    dt_sp, dta = dt_prep(dt_raw, ssd_dt_bias[0].reshape(1, SSD_N_HEADS), ssd_a_log[0].reshape(1, SSD_N_HEADS))
    dskip = jnp.repeat(ssd_d[0], SSD_HEAD_DIM).reshape(1, SSD_D_INNER)
    norm_w = ssd_norm_w[0].reshape(1, SSD_D_INNER)
    h0_p = jnp.zeros((BATCH, SSD_N_GROUPS, SSD_GROUP_CH, SSD_D_STATE), F32)
    h0_s = state_ssd[0].reshape(DEC_BATCH, SSD_N_GROUPS, SSD_GROUP_CH, SSD_D_STATE)
    lay_p = _dt_layouts(dt_sp, 0, BATCH, SEQ, SSD_Q_PROMPT) + _dt_layouts(dta, 0, BATCH, SEQ, SSD_Q_PROMPT)
    lay_s = (_dt_layouts(dt_sp, T_PROMPT, DEC_BATCH, DEC_SEQ, DEC_SEQ)
             + _dt_layouts(dta, T_PROMPT, DEC_BATCH, DEC_SEQ, DEC_SEQ))
    ymix, ssd_p = ssd_scan(xbc, prev_p, conv_w, conv_b, z, lay_p[0], lay_p[2], lay_p[1], lay_p[3], h0_p, dskip,
                           norm_w, row0=0, nseq=BATCH, seqlen=SEQ, q=SSD_Q_PROMPT)
    ymix, ssd_s = ssd_scan(xbc, prev_s, conv_w, conv_b, z, lay_s[0], lay_s[2], lay_s[1], lay_s[3], h0_s, dskip,
                           norm_w, row0=T_PROMPT, nseq=DEC_BATCH, seqlen=DEC_SEQ, q=DEC_SEQ, out_alias=ymix)
    mix = matmul(ymix, ssd_w_out[0].astype(BF16), tm=MM_TM, tn=512)

    wg = cast_weights(moe_w_gate, BF16, rows_per_block=D_MODEL // 2)
    wu = cast_weights(moe_w_up, BF16, rows_per_block=D_MODEL // 2)
    wd = moe_w_down
    x1, idx, gate = ln_router(x0_parts, mix, ln_mix_g[0].reshape(1, d), ln_mix_b[0].reshape(1, d), rw, rb)
    pos, y_sorted = moe_layer(x1, idx, wg, wu, wd, 0)
    (x2,) = combine_ln(pos, x1, gate, y_sorted, ln_ffn_g[0].reshape(1, d), ln_ffn_b[0].reshape(1, d))

    wst, vt, krev, aq = s5_prep(s5_lambda_re[0], s5_lambda_im[0], s5_log_step[0], s5_b_re[0], s5_b_im[0],
                                s5_c_re[0], s5_c_im[0])
    s5_dskip = s5_d[0].reshape(1, d)
    sp2 = 2 * S5_PACK_STATE
    hs_p = jnp.zeros((S5_N_PACKS, BATCH // S5_CHAIN_SEQS, S5_CHAIN_SEQS, sp2), F32)
    hs_s = _s5_pack_state(state_s5_re[0], state_s5_im[0]).transpose(1, 0, 2)[:, None]
    gact, s5_p = s5_scan(x2, wst, vt, krev, aq, s5_dskip, hs_p, row0=0, nseq=BATCH, seqlen=SEQ, chain=True)
    gact, s5_s = s5_scan(x2, wst, vt, krev, aq, s5_dskip, hs_s, row0=T_PROMPT, nseq=DEC_BATCH, seqlen=DEC_SEQ,
                         chain=False, out_alias=gact)
    mix = glu_matmul(gact, s5_w_glu[0].astype(BF16), tm=MM_TM, tn=512)

    x3, idx, gate = ln_router([x2], mix, ln_mix_g[1].reshape(1, d), ln_mix_b[1].reshape(1, d), rw, rb)
    pos, y_sorted = moe_layer(x3, idx, wg, wu, wd, 1)
    g1 = ln_ffn_g[1].reshape(1, d)
    b1 = ln_ffn_b[1].reshape(1, d)
    (y_p,) = combine_ln(pos, x3, gate, y_sorted, g1, b1, row0=0, nrows=T_PROMPT)
    (y_s,) = combine_ln(pos, x3, gate, y_sorted, g1, b1, row0=T_PROMPT, nrows=T_SAMPLE)

    keep = SSD_CONV_W - 1
    conv_p = jnp.stack([xbc[(b + 1) * SEQ - keep:(b + 1) * SEQ] for b in range(BATCH)])[None]
    conv_s = xbc[T_PROMPT:].reshape(DEC_BATCH, DEC_SEQ, SSD_CONV_DIM)[:, DEC_SEQ - keep:][None]
    ssd_state_p = ssd_p.reshape(1, BATCH, SSD_N_HEADS, SSD_HEAD_DIM, SSD_D_STATE)
    ssd_state_s = ssd_s.reshape(1, DEC_BATCH, SSD_N_HEADS, SSD_HEAD_DIM, SSD_D_STATE)
    re_p, im_p = _s5_unpack_state(s5_p.reshape(S5_N_PACKS, BATCH, sp2).transpose(1, 0, 2))
    re_s, im_s = _s5_unpack_state(s5_s[:, 0].transpose(1, 0, 2))
    return (y_p.reshape(BATCH, SEQ, d), y_s.reshape(DEC_BATCH, DEC_SEQ, d),
            conv_p, ssd_state_p, re_p[None], im_p[None],
            conv_s, ssd_state_s, re_s[None], im_s[None])
```

```python
import functools
import math

import jax
import jax.numpy as jnp
from jax import lax
from jax.experimental import pallas as pl
from jax.experimental.pallas import tpu as pltpu

F32 = jnp.float32
BF16 = jnp.bfloat16
I32 = jnp.int32

D_MODEL = 4096
BATCH, SEQ = 4, 4096
DEC_BATCH, DEC_SEQ = 32, 16
T_PROMPT = BATCH * SEQ
T_SAMPLE = DEC_BATCH * DEC_SEQ
T_ALL = T_PROMPT + T_SAMPLE
SSD_D_INNER = 8192
SSD_HEAD_DIM = 64
SSD_N_HEADS = 128
SSD_N_GROUPS = 8
SSD_HEADS_PER_GROUP = 16
SSD_D_STATE = 128
SSD_GROUP_CH = SSD_D_INNER // SSD_N_GROUPS
SSD_BC_DIM = SSD_N_GROUPS * SSD_D_STATE
SSD_CONV_DIM = SSD_D_INNER + 2 * SSD_BC_DIM
SSD_CONV_W = 4
S5_GROUP_CH = 16
S5_N_GROUPS = 256
S5_STATE = 64
S5_PACK_GROUPS = 8
S5_N_PACKS = S5_N_GROUPS // S5_PACK_GROUPS
S5_PACK_STATE = S5_PACK_GROUPS * S5_STATE
S5_Q = 16
S5_CHAIN_SEQS = 2
N_EXPERTS = 16
EXPERTS_PER_GROUP = 4
D_EXPERT = 1024
DEPTH = 2
DEEPNORM_ALPHA = (2 * DEPTH) ** 0.25
LN_EPS = 1e-5
RMS_EPS = 1e-5

LANES = 128
SUBLANES = 8
VMEM_LIMIT_BYTES = 56 * 1024 * 1024

MM_TM = 768
MOE_TM = 256
MOE_ROWS = 2 * T_ALL + N_EXPERTS * MOE_TM
MOE_TILES = MOE_ROWS // MOE_TM
LN_TM = 128
SSD_Q_PROMPT = 128
SSD_GROUPS_PER_STEP = 4


def _cparams(semantics):
    return pltpu.CompilerParams(dimension_semantics=semantics, vmem_limit_bytes=VMEM_LIMIT_BYTES)


def _dot(a, b):
    return jnp.dot(a, b, preferred_element_type=F32)


def _dot_nt(a, b):
    return lax.dot_general(a, b, (((1,), (1,)), ((), ())), preferred_element_type=F32)


def _dot_tn(a, b):
    return lax.dot_general(a, b, (((0,), (0,)), ((), ())), preferred_element_type=F32)


def _split3(v):
    hi = v.astype(BF16)
    r = v - hi.astype(F32)
    mid = r.astype(BF16)
    lo = (r - mid.astype(F32)).astype(BF16)
    return hi, mid, lo


def _dot_sel(sel, v):
    hi, mid, lo = _split3(v)
    return _dot(sel, hi) + _dot(sel, mid) + _dot(sel, lo)


def _dot_sel_r(v, sel):
    hi, mid, lo = _split3(v)
    return _dot(hi, sel) + _dot(mid, sel) + _dot(lo, sel)


def _dot_sel_r2(v, sel):
    hi = v.astype(BF16)
    lo = (v - hi.astype(F32)).astype(BF16)
    return _dot(hi, sel) + _dot(lo, sel)


def _silu(x):
    return x * jax.nn.sigmoid(x)


def _gelu_exact(x):
    return 0.5 * x * (1.0 + lax.erf(x * (1.0 / math.sqrt(2.0))))


def _mm_kernel(x_ref, w_ref, o_ref):
    o_ref[...] = _dot(x_ref[...], w_ref[...]).astype(o_ref.dtype)


def matmul(x, w, *, tm, tn, col0=0, ncols=None, out_dtype=F32):
    m, k = x.shape
    n = w.shape[1] if ncols is None else ncols
    c0 = col0 // tn
    return pl.pallas_call(
        _mm_kernel,
        out_shape=jax.ShapeDtypeStruct((m, n), out_dtype),
        grid=(n // tn, m // tm),
        in_specs=[pl.BlockSpec((tm, k), lambda j, i: (i, 0)),
                  pl.BlockSpec((k, tn), lambda j, i: (0, c0 + j))],
        out_specs=pl.BlockSpec((tm, tn), lambda j, i: (i, j)),
        compiler_params=_cparams(("parallel", "parallel")),
        name="matmul",
    )(x, w)


def _glu_kernel(x_ref, w1_ref, w2_ref, o_ref):
    x = x_ref[...]
    z1 = _dot(x, w1_ref[...])
    z2 = _dot(x, w2_ref[...])
    o_ref[...] = z1 * jax.nn.sigmoid(z2)


def glu_matmul(x, w, *, tm, tn):
    m, k = x.shape
    n = w.shape[1] // 2
    half = n // tn
    return pl.pallas_call(
        _glu_kernel,
        out_shape=jax.ShapeDtypeStruct((m, n), F32),
        grid=(n // tn, m // tm),
        in_specs=[pl.BlockSpec((tm, k), lambda j, i: (i, 0)),
                  pl.BlockSpec((k, tn), lambda j, i: (0, j)),
                  pl.BlockSpec((k, tn), lambda j, i: (0, half + j))],
        out_specs=pl.BlockSpec((tm, tn), lambda j, i: (i, j)),
        compiler_params=_cparams(("parallel", "parallel")),
        name="glu_matmul",
    )(x, w, w)


def _layer_norm(v, g, b):
    mu = jnp.mean(v, axis=-1, keepdims=True)
    d = v - mu
    var = jnp.mean(d * d, axis=-1, keepdims=True)
    return d * lax.rsqrt(var + LN_EPS) * g + b


def _route_rows(xb, rw, rb):
    tm = xb.shape[0]
    lane = lax.broadcasted_iota(I32, (tm, LANES), 1)
    live = lane < N_EXPERTS
    logits = _dot(xb, rw) + rb
    logits = jnp.where(live, logits, -jnp.inf)
    m = jnp.max(logits, axis=-1, keepdims=True)
    e = jnp.exp(logits - m)
    probs = e / jnp.sum(e, axis=-1, keepdims=True)
    best = None
    for g in range(N_EXPERTS // EXPERTS_PER_GROUP):
        in_g = (lane >= g * EXPERTS_PER_GROUP) & (lane < (g + 1) * EXPERTS_PER_GROUP)
        score = jnp.max(jnp.where(in_g, probs, -1.0), axis=-1, keepdims=True)
        if best is None:
            best, best_g = score, jnp.zeros((tm, 1), I32)
        else:
            upd = score > best
            best = jnp.where(upd, score, best)
            best_g = jnp.where(upd, g, best_g)
    lo = best_g * EXPERTS_PER_GROUP
    in_best = (lane >= lo) & (lane < lo + EXPERTS_PER_GROUP)
    cand = jnp.where(in_best, probs, -1.0)
    p1 = jnp.max(cand, axis=-1, keepdims=True)
    i1 = jnp.min(jnp.where(cand == p1, lane, LANES), axis=-1, keepdims=True)
    cand2 = jnp.where(lane == i1, -2.0, cand)
    p2 = jnp.max(cand2, axis=-1, keepdims=True)
    i2 = jnp.min(jnp.where(cand2 == p2, lane, LANES), axis=-1, keepdims=True)
    tot = p1 + p2
    idx = jnp.where(lane == 0, i1, jnp.where(lane == 1, i2, 0))
    gate = jnp.where(lane == 0, p1 / tot, jnp.where(lane == 1, p2 / tot, 0.0))
    return idx, gate


def _row_part_specs(parts, tm, d):
    specs, firsts, t0 = [], [], 0
    for p in parts:
        nt = p.shape[0] // tm
        specs.append(pl.BlockSpec((tm, d), lambda i, *_, t0=t0, nt=nt: (jnp.clip(i - t0, 0, nt - 1), 0)))
        firsts.append(t0)
        t0 += nt
    return specs, firsts


def _select_part(i, refs, firsts, store):
    for k, ref in enumerate(refs):
        hi = firsts[k + 1] if k + 1 < len(refs) else None
        cond = i >= firsts[k] if hi is None else (i >= firsts[k]) & (i < hi)

        @pl.when(cond)
        def _(ref=ref):
            store(ref[...])


def _ln_router_kernel(*refs, firsts):
    n = len(firsts)
    x_refs = refs[:n]
    mix_ref, g_ref, b_ref, rw_ref, rb_ref, xo_ref, idx_ref, gate_ref = refs[n:]

    def finish(x):
        y = _layer_norm(DEEPNORM_ALPHA * x + mix_ref[...], g_ref[...], b_ref[...])
        xo_ref[...] = y
        idx, gate = _route_rows(y.astype(BF16), rw_ref[...], rb_ref[...])
        idx_ref[...] = idx
        gate_ref[...] = gate

    _select_part(pl.program_id(0), x_refs, firsts, finish)


def ln_router(x_parts, mix, g, b, rw, rb):
    t, d = mix.shape
    tm = LN_TM
    x_specs, firsts = _row_part_specs(x_parts, tm, d)
    row = pl.BlockSpec((tm, d), lambda i: (i, 0))
    vec = pl.BlockSpec((1, d), lambda i: (0, 0))
    small = pl.BlockSpec((tm, LANES), lambda i: (i, 0))
    return pl.pallas_call(
        functools.partial(_ln_router_kernel, firsts=tuple(firsts)),
        out_shape=(jax.ShapeDtypeStruct((t, d), F32),
                   jax.ShapeDtypeStruct((t, LANES), I32), jax.ShapeDtypeStruct((t, LANES), F32)),
        grid=(t // tm,),
        in_specs=x_specs + [row, vec, vec, pl.BlockSpec((d, LANES), lambda i: (0, 0)),
                            pl.BlockSpec((1, LANES), lambda i: (0, 0))],
        out_specs=(row, small, small),
        compiler_params=_cparams(("parallel",)),
        name="ln_router",
    )(*x_parts, mix, g, b, rw, rb)


def _cast_rows_kernel(*refs, firsts):
    o_ref = refs[-1]

    def store(v):
        o_ref[...] = v.astype(o_ref.dtype)

    _select_part(pl.program_id(0), refs[:-1], firsts, store)


def cast_rows(x_parts, dtype, tm):
    d = x_parts[0].shape[1]
    t = sum(p.shape[0] for p in x_parts)
    x_specs, firsts = _row_part_specs(x_parts, tm, d)
    return pl.pallas_call(
        functools.partial(_cast_rows_kernel, firsts=tuple(firsts)),
        out_shape=jax.ShapeDtypeStruct((t, d), dtype),
        grid=(t // tm,),
        in_specs=x_specs,
        out_specs=pl.BlockSpec((tm, d), lambda i: (i, 0)),
        compiler_params=_cparams(("parallel",)),
        name="cast_rows",
    )(*x_parts)


def _cast_kernel(x_ref, o_ref):
    o_ref[...] = x_ref[...].astype(o_ref.dtype)


def cast_weights(w, dtype, rows_per_block):
    nl, ne, r, c = w.shape
    spec = pl.BlockSpec((1, 1, rows_per_block, c), lambda l, e, j: (l, e, j, 0))
    return pl.pallas_call(
        _cast_kernel,
        out_shape=jax.ShapeDtypeStruct(w.shape, dtype),
        grid=(nl, ne, r // rows_per_block),
        in_specs=[spec],
        out_specs=spec,
        compiler_params=_cparams(("parallel", "parallel", "parallel")),
        name="cast_weights",
    )(w)


def _row_gather_start(src_hbm, row, dst, dst_row, sem, priority=0):
    pltpu.make_async_copy(src_hbm.at[pl.ds(row, 1)], dst.at[pl.ds(dst_row, 1)], sem).start(priority=priority)


def _slot_wait(buf, slot, sem):
    pltpu.make_async_copy(buf.at[slot], buf.at[slot], sem.at[slot]).wait()


def _moe_up_kernel(te_ref, pos_ref, nv_ref, pad_ref, x_hbm, wg_ref, wu_ref, h_ref, xbuf, sem, src_ref):
    del te_ref
    i = pl.program_id(0)
    slot = lax.rem(i, 2)
    nvalid = nv_ref[0]

    @pl.when(i == 0)
    def _():
        def clear(r, carry):
            src_ref[r] = 0
            return carry
        for e in range(N_EXPERTS + 1):
            lax.fori_loop(pad_ref[e], pad_ref[N_EXPERTS + 1 + e], clear, 0)

        def put(s, carry):
            src_ref[pos_ref[s]] = lax.shift_right_logical(s, 1)
            return carry
        lax.fori_loop(0, pos_ref.shape[0], put, 0, unroll=8)

        def body(r, carry):
            _row_gather_start(x_hbm, src_ref[r], xbuf.at[0], r, sem.at[0])
            return carry
        lax.fori_loop(0, MOE_TM, body, 0)

    @pl.when(i <= nvalid)
    def _():
        _slot_wait(xbuf, slot, sem)

    @pl.when(i < nvalid)
    def _():
        base = (i + 1) * MOE_TM
        for r in range(MOE_TM):
            _row_gather_start(x_hbm, src_ref[base + r], xbuf.at[1 - slot], r, sem.at[1 - slot], priority=r % 2)
        xb = xbuf[slot].astype(BF16)
        gate = _dot(xb, wg_ref[0, 0])
        up = _dot(xb, wu_ref[0, 0])
        h_ref[...] = (_silu(gate) * up).astype(BF16)

    @pl.when(i >= nvalid)
    def _():
        h_ref[...] = jnp.zeros_like(h_ref)


def moe_up(tile_expert, pos, nvalid, pad_ranges, x, wg, wu, layer):
    d = x.shape[1]
    assert (2 * x.shape[0] + N_EXPERTS * (MOE_TM - 1)) // MOE_TM < MOE_TILES
    wspec = pl.BlockSpec((1, 1, d, D_EXPERT), lambda i, te, pos, nv, pad: (layer, te[i], 0, 0))
    grid_spec = pltpu.PrefetchScalarGridSpec(
        num_scalar_prefetch=4,
        grid=(MOE_TILES,),
        in_specs=[pl.BlockSpec(memory_space=pl.ANY), wspec, wspec],
        out_specs=pl.BlockSpec((MOE_TM, D_EXPERT), lambda i, te, pos, nv, pad: (i, 0)),
        scratch_shapes=[pltpu.VMEM((2, MOE_TM, d), F32), pltpu.SemaphoreType.DMA((2,)),
                        pltpu.SMEM((MOE_ROWS,), I32)],
    )
    return pl.pallas_call(
        _moe_up_kernel,
        out_shape=jax.ShapeDtypeStruct((MOE_ROWS, D_EXPERT), BF16),
        grid_spec=grid_spec,
        compiler_params=_cparams(("arbitrary",)),
        name="moe_up",
    )(tile_expert, pos, nvalid, pad_ranges, x, wg, wu)


def _moe_down_kernel(te_ref, nv_ref, run_ref, h_ref, wd_hbm, y_ref, wbf, stage, sem, *, layer):
    i = pl.program_id(0)
    valid = i < nv_ref[0]
    new_expert = (i == 0) | (te_ref[i] != te_ref[jnp.maximum(i - 1, 0)])
    slot = lax.rem(run_ref[2 * i], 2)
    nxt = run_ref[2 * i + 1]

    def fetch(expert, s):
        return pltpu.make_async_copy(wd_hbm.at[layer, expert], stage.at[s], sem.at[s])

    @pl.when(i == 0)
    def _():
        fetch(te_ref[0], 0).start()

    @pl.when(new_expert & valid)
    def _():
        fetch(te_ref[i], slot).wait()
        wbf[...] = stage[slot].astype(BF16)

        @pl.when(nxt >= 0)
        def _():
            fetch(nxt, 1 - slot).start()

    @pl.when(valid)
    def _():
        y_ref[...] = _dot(h_ref[...], wbf[...])

    @pl.when(i >= nv_ref[0])
    def _():
        y_ref[...] = jnp.zeros_like(y_ref)


def moe_down(tile_expert, nvalid, runs, h, wd, layer):
    d = wd.shape[3]
    grid_spec = pltpu.PrefetchScalarGridSpec(
        num_scalar_prefetch=3,
        grid=(MOE_TILES,),
        in_specs=[pl.BlockSpec((MOE_TM, D_EXPERT), lambda i, te, nv, rn: (i, 0)),
                  pl.BlockSpec(memory_space=pl.ANY)],
        out_specs=pl.BlockSpec((MOE_TM, d), lambda i, te, nv, rn: (i, 0)),
        scratch_shapes=[pltpu.VMEM((D_EXPERT, d), BF16), pltpu.VMEM((2, D_EXPERT, d), F32),
                        pltpu.SemaphoreType.DMA((2,))],
    )
    return pl.pallas_call(
        functools.partial(_moe_down_kernel, layer=layer),
        out_shape=jax.ShapeDtypeStruct((MOE_ROWS, d), F32),
        grid_spec=grid_spec,
        compiler_params=_cparams(("arbitrary",)),
        name="moe_down",
    )(tile_expert, nvalid, runs, h, wd)


def _combine_ln_kernel(pos_ref, x_ref, gsel_ref, y_hbm, g_ref, b_ref, *rest, tm, tile0, ntiles, with_router):
    if with_router:
        rw_ref, rb_ref, xo_ref, xb_ref, idx_ref, gate_ref, ybuf, sem = rest
    else:
        xo_ref, ybuf, sem = rest
    i = pl.program_id(0)
    slot = lax.rem(i, 2)

    @pl.when(i == 0)
    def _():
        def body(r, carry):
            _row_gather_start(y_hbm, pos_ref[2 * tile0 * tm + 2 * r], ybuf.at[0, 0], r, sem.at[0])
            _row_gather_start(y_hbm, pos_ref[2 * tile0 * tm + 2 * r + 1], ybuf.at[0, 1], r, sem.at[0])
            return carry
        lax.fori_loop(0, tm, body, 0)

    _slot_wait(ybuf, slot, sem)
    base = 2 * (tile0 + jnp.minimum(i + 1, ntiles - 1)) * tm
    for r in range(tm):
        _row_gather_start(y_hbm, pos_ref[base + 2 * r], ybuf.at[1 - slot, 0], r, sem.at[1 - slot], priority=0)
        _row_gather_start(y_hbm, pos_ref[base + 2 * r + 1], ybuf.at[1 - slot, 1], r, sem.at[1 - slot], priority=1)
    gsel = gsel_ref[...]
    ffn = gsel[:, 0:1] * ybuf[slot, 0] + gsel[:, 1:2] * ybuf[slot, 1]
    y = _layer_norm(DEEPNORM_ALPHA * x_ref[...] + ffn, g_ref[...], b_ref[...])
    xo_ref[...] = y
    if with_router:
        yb = y.astype(BF16)
        xb_ref[...] = yb
        idx, gate = _route_rows(yb, rw_ref[...], rb_ref[...])
        idx_ref[...] = idx
        gate_ref[...] = gate

    @pl.when(i == ntiles - 1)
    def _():
        _slot_wait(ybuf, 1 - slot, sem)


def combine_ln(pos, x, gate_sel, y_sorted, g, b, *, row0=0, nrows=None, router=None):
    d = x.shape[1]
    tm = LN_TM
    nrows = x.shape[0] if nrows is None else nrows
    tile0 = row0 // tm
    ntiles = nrows // tm
    row_in = pl.BlockSpec((tm, d), lambda i, p: (tile0 + i, 0))
    row_out = pl.BlockSpec((tm, d), lambda i, p: (i, 0))
    vec = pl.BlockSpec((1, d), lambda i, p: (0, 0))
    small = pl.BlockSpec((tm, LANES), lambda i, p: (i, 0))
    in_specs = [row_in, pl.BlockSpec((tm, LANES), lambda i, p: (tile0 + i, 0)),
                pl.BlockSpec(memory_space=pl.ANY), vec, vec]
    args = [pos, x, gate_sel, y_sorted, g, b]
    out_shape = [jax.ShapeDtypeStruct((nrows, d), F32)]
    out_specs = [row_out]
    if router is not None:
        in_specs += [pl.BlockSpec((d, LANES), lambda i, p: (0, 0)), pl.BlockSpec((1, LANES), lambda i, p: (0, 0))]
        args += list(router)
        out_shape += [jax.ShapeDtypeStruct((nrows, d), BF16), jax.ShapeDtypeStruct((nrows, LANES), I32),
                      jax.ShapeDtypeStruct((nrows, LANES), F32)]
        out_specs += [row_out, small, small]
    grid_spec = pltpu.PrefetchScalarGridSpec(
        num_scalar_prefetch=1,
        grid=(ntiles,),
        in_specs=in_specs,
        out_specs=tuple(out_specs),
        scratch_shapes=[pltpu.VMEM((2, 2, tm, d), F32), pltpu.SemaphoreType.DMA((2,))],
    )
    kern = functools.partial(_combine_ln_kernel, tm=tm, tile0=tile0, ntiles=ntiles,
                             with_router=router is not None)
    return pl.pallas_call(
        kern,
        out_shape=tuple(out_shape),
        grid_spec=grid_spec,
        compiler_params=_cparams(("arbitrary",)),
        name="combine_ln",
    )(*args)


def _route_tables(idx):
    t = idx.shape[0]
    e_flat = idx[:, :2].reshape(-1)
    onehot = (e_flat[:, None] == jnp.arange(N_EXPERTS, dtype=I32)[None, :]).astype(I32)
    csum = jnp.cumsum(onehot, axis=0)
    rank = jnp.take_along_axis(csum, e_flat[:, None], axis=1)[:, 0] - 1
    counts = csum[-1]
    padded = ((counts + MOE_TM - 1) // MOE_TM) * MOE_TM
    ends = jnp.cumsum(padded)
    starts = ends - padded
    pos = (starts[e_flat] + rank).astype(I32)
    nvalid = (ends[-1] // MOE_TM).astype(I32)
    tile_start = jnp.arange(MOE_TILES, dtype=I32) * MOE_TM
    te = jnp.sum((tile_start[:, None] >= ends[None, :]).astype(I32), axis=1)
    te = jnp.minimum(te, te[nvalid - 1]).astype(I32)
    pad_lo = jnp.concatenate([starts + counts, ends[-1:]])
    pad_hi = jnp.concatenate([ends, ends[-1:] + MOE_TM])
    owns = padded > 0
    run_of = jnp.cumsum(owns.astype(I32)) - 1
    first_from = lax.cummin(jnp.where(owns, jnp.arange(N_EXPERTS, dtype=I32), N_EXPERTS), reverse=True)
    next_of = jnp.concatenate([first_from[1:], jnp.full((1,), N_EXPERTS, I32)])
    next_of = jnp.where(next_of >= N_EXPERTS, -1, next_of)
    runs = jnp.stack([run_of[te], next_of[te]], axis=1).reshape(-1).astype(I32)
    return pos, te, nvalid.reshape(1), jnp.concatenate([pad_lo, pad_hi]).astype(I32), runs


def moe_layer(x_f32, idx, wg, wu, wd, layer):
    pos, te, nvalid, pad_ranges, runs = _route_tables(idx)
    h = moe_up(te, pos, nvalid, pad_ranges, x_f32, wg, wu, layer)
    y_sorted = moe_down(te, nvalid, runs, h, wd, layer)
    return pos, y_sorted


def _conv_silu(x, prev8, w, b):
    row8 = lax.broadcasted_iota(I32, prev8.shape, 0)
    acc = jnp.broadcast_to(b, x.shape)
    for k in range(SSD_CONV_W):
        s = SSD_CONV_W - 1 - k
        if s == 0:
            xs = x
        else:
            xs = pltpu.roll(x, s, axis=0)
            top = jnp.where(row8 < s, pltpu.roll(prev8, s, axis=0), xs[:SUBLANES])
            xs = jnp.concatenate([top, xs[SUBLANES:]], axis=0)
        acc = acc + xs * w[k:k + 1, :]
    return _silu(acc)


def _dt_kernel(dt_ref, bias_ref, alog_ref, dt_o, dta_o):
    v = dt_ref[...] + bias_ref[...]
    sp = jnp.maximum(v, 0.0) + jnp.log1p(jnp.exp(-jnp.abs(v)))
    dt_o[...] = sp
    dta_o[...] = sp * (-jnp.exp(alog_ref[...]))


def dt_prep(dt_raw, bias, a_log):
    t, h = dt_raw.shape
    tr = t // 8
    row = pl.BlockSpec((tr, h), lambda i: (i, 0))
    vec = pl.BlockSpec((1, h), lambda i: (0, 0))
    return pl.pallas_call(
        _dt_kernel,
        out_shape=(jax.ShapeDtypeStruct((t, h), F32), jax.ShapeDtypeStruct((t, h), F32)),
        grid=(8,),
        in_specs=[row, vec, vec],
        out_specs=(row, row),
        compiler_params=_cparams(("parallel",)),
        name="dt_prep",
    )(dt_raw, bias, a_log)


def _ssd_kernel(x_ref, b_ref, c_ref, px_ref, pb_ref, pc_ref, wx_ref, wb_ref, wc_ref, bx_ref, bb_ref, bc_ref,
                z_ref, dt_ref, dta_ref, dtt_ref, dtat_ref, h0_ref, dskip_ref, nw_ref,
                y_ref, hout_ref, state, halo_x, halo_b, halo_c, *, q, nchunks):
    c = pl.program_id(2)
    gc = SSD_GROUP_CH
    n = SSD_D_STATE
    streams = ((x_ref, px_ref, wx_ref, bx_ref, halo_x), (b_ref, pb_ref, wb_ref, bb_ref, halo_b),
               (c_ref, pc_ref, wc_ref, bc_ref, halo_c))

    @pl.when(c == 0)
    def _():
        for k in range(SSD_GROUPS_PER_STEP):
            state[k] = h0_ref[0, k].T
        for _, prev_ref, _, _, halo in streams:
            halo[...] = prev_ref[0]

    acts = []
    for raw_ref, _, w_ref, bias_ref, halo in streams:
        raw = raw_ref[...]
        acts.append(_conv_silu(raw, halo[...], w_ref[...], bias_ref[...]))
        halo[...] = raw[q - SUBLANES:, :]
    xa, ba, ca = acts

    new_states = []
    for k in range(SSD_GROUPS_PER_STEP):
        cols = slice(k * gc, (k + 1) * gc)
        y, s_new = _ssd_group(
            xa[:, cols], ba[:, k * n:(k + 1) * n], ca[:, k * n:(k + 1) * n], z_ref[:, cols],
            dt_ref[k, 0], dta_ref[k, 0], dtt_ref[k, 0], dtat_ref[k, 0], state[k],
            dskip_ref[:, cols], nw_ref[:, cols], q=q)
        y_ref[:, cols] = y
        state[k] = s_new
        new_states.append(s_new)

    @pl.when(c == nchunks - 1)
    def _():
        for k in range(SSD_GROUPS_PER_STEP):
            hout_ref[0, k] = new_states[k].T


def _ssd_group(x, bm, cm, z, dt, dta, dtt, dtat, s_old, dskip, nw, *, q):
    hg = SSD_HEADS_PER_GROUP
    p = SSD_HEAD_DIM
    xb = x.astype(BF16)
    bm = bm.astype(BF16)
    cm = cm.astype(BF16)

    ri = lax.broadcasted_iota(I32, (q, q), 0)
    ci = lax.broadcasted_iota(I32, (q, q), 1)
    causal = ri >= ci
    lower = jnp.where(causal, 1.0, 0.0).astype(BF16)
    upper = jnp.where(ri <= ci, 1.0, 0.0).astype(BF16)
    log2e = math.log2(math.e)
    acum = _dot_sel(lower, dta) * log2e
    acumt = _dot_sel_r(dtat, upper) * log2e
    alast = acum[q - 1:q, :]

    cb = _dot_nt(cm, bm)
    lane = lax.broadcasted_iota(I32, (q, 2 * p), 1)
    y_pairs = []
    for hp in range(hg // 2):
        ms = []
        for h in (2 * hp, 2 * hp + 1):
            seg = acum[:, h:h + 1] - acumt[h:h + 1, :]
            decay = jnp.exp2(jnp.where(causal, seg, -jnp.inf))
            ms.append((cb * decay * dtt[h:h + 1, :]).astype(BF16))
        lhs = jnp.concatenate(ms, axis=1)
        xp = xb[:, 2 * p * hp:2 * p * (hp + 1)]
        zero = jnp.zeros_like(xp)
        rhs = jnp.concatenate([jnp.where(lane < p, xp, zero), jnp.where(lane >= p, xp, zero)], axis=0)
        y_pairs.append(_dot(lhs, rhs))
    y = jnp.concatenate(y_pairs, axis=1)

    hi = lax.broadcasted_iota(I32, (hg, hg * p), 0)
    li = lax.broadcasted_iota(I32, (hg, hg * p), 1)
    widen = jnp.where((li >= hi * p) & (li < (hi + 1) * p), 1.0, 0.0).astype(BF16)
    scales = jnp.concatenate([jnp.exp2(acum), jnp.exp2(alast - acum) * dt,
                              jnp.broadcast_to(jnp.exp2(alast), (SUBLANES, hg))], axis=0)
    wide = _dot_sel_r2(scales, widen)
    e_in = wide[:q]
    e_out = wide[q:2 * q]
    e_all = wide[2 * q:2 * q + 1]

    y = y + _dot(cm, s_old.astype(BF16)) * e_in
    xw = (x * e_out).astype(BF16)
    s_new = s_old * e_all + _dot_tn(bm, xw)

    y = y + dskip * x
    y = y * _silu(z)
    y = y * lax.rsqrt(jnp.mean(y * y, axis=-1, keepdims=True) + RMS_EPS)
    return (y * nw).astype(BF16), s_new


def ssd_scan(xbc, conv_prev, conv_w, conv_b, z, dt4, dta4, dtt4, dtat4, h0, dskip, norm_w, *,
             row0, nseq, seqlen, q, out_alias=None):
    t = xbc.shape[0]
    gps = SSD_GROUPS_PER_STEP
    g = SSD_N_GROUPS
    gc = gps * SSD_GROUP_CH
    n = SSD_D_STATE
    nchunks = seqlen // q
    rb0 = row0 // q
    bcol0 = SSD_D_INNER // (gps * n)
    ccol0 = (SSD_D_INNER + SSD_BC_DIM) // (gps * n)

    def rows(b, gg, c):
        return rb0 + b * nchunks + c

    def col_specs(shape_of, index_of):
        return [pl.BlockSpec(shape_of(gc), index_of(0)), pl.BlockSpec(shape_of(gps * n), index_of(bcol0)),
                pl.BlockSpec(shape_of(gps * n), index_of(ccol0))]

    in_specs = (col_specs(lambda w: (q, w), lambda c0: lambda b, gg, c: (rows(b, gg, c), c0 + gg))
                + col_specs(lambda w: (1, SUBLANES, w), lambda c0: lambda b, gg, c: (b, 0, c0 + gg))
                + col_specs(lambda w: (SSD_CONV_W, w), lambda c0: lambda b, gg, c: (0, c0 + gg))
                + col_specs(lambda w: (1, w), lambda c0: lambda b, gg, c: (0, c0 + gg)))
    in_specs += [pl.BlockSpec((q, gc), lambda b, gg, c: (rows(b, gg, c), gg)),
                pl.BlockSpec((gps, 1, q, 16), lambda b, gg, c: (gg, b * nchunks + c, 0, 0)),
                pl.BlockSpec((gps, 1, q, 16), lambda b, gg, c: (gg, b * nchunks + c, 0, 0)),
                pl.BlockSpec((gps, 1, 16, q), lambda b, gg, c: (gg, b * nchunks + c, 0, 0)),
                pl.BlockSpec((gps, 1, 16, q), lambda b, gg, c: (gg, b * nchunks + c, 0, 0)),
                pl.BlockSpec((1, gps, SSD_GROUP_CH, n), lambda b, gg, c: (b, gg, 0, 0)),
                pl.BlockSpec((1, gc), lambda b, gg, c: (0, gg)),
                pl.BlockSpec((1, gc), lambda b, gg, c: (0, gg))]
    args = [xbc] * 3 + [conv_prev] * 3 + [conv_w] * 3 + [conv_b] * 3 + [z, dt4, dta4, dtt4, dtat4, h0, dskip, norm_w]
    nin = len(args)
    kern = functools.partial(_ssd_kernel, q=q, nchunks=nchunks)
    aliases = {}
    if out_alias is not None:
        in_specs.append(pl.BlockSpec(memory_space=pl.ANY))
        args.append(out_alias)
        aliases = {nin: 0}
        base = kern
        kern = lambda *refs: base(*refs[:nin], *refs[nin + 1:])
    return pl.pallas_call(
        kern,
        out_shape=(jax.ShapeDtypeStruct((t, SSD_D_INNER), BF16),
                   jax.ShapeDtypeStruct((nseq, g, SSD_GROUP_CH, n), F32)),
        grid=(nseq, g // gps, nchunks),
        in_specs=in_specs,
        out_specs=(pl.BlockSpec((q, gc), lambda b, gg, c: (rows(b, gg, c), gg)),
                   pl.BlockSpec((1, gps, SSD_GROUP_CH, n), lambda b, gg, c: (b, gg, 0, 0))),
        scratch_shapes=[pltpu.VMEM((gps, n, SSD_GROUP_CH), F32), pltpu.VMEM((SUBLANES, gc), F32),
                        pltpu.VMEM((SUBLANES, gps * n), F32), pltpu.VMEM((SUBLANES, gps * n), F32)],
        input_output_aliases=aliases,
        compiler_params=_cparams(("parallel", "parallel", "arbitrary")),
        name="ssd_scan",
    )(*args)


def _dt_layouts(v, row0, nseq, seqlen, q):
    nb = nseq * seqlen // q
    part = v[row0:row0 + nseq * seqlen].reshape(nb, q, SSD_N_GROUPS, SSD_HEADS_PER_GROUP)
    return part.transpose(2, 0, 1, 3), part.transpose(2, 0, 3, 1)


def _s5_prep_kernel(lam_ref, row_ref, bcat_ref, bswap_ref, ccat_ref, cswap_ref, wst_ref, vt_ref, krev_ref, aq_ref):
    n = S5_STATE
    sp = S5_PACK_STATE
    lr, li, ls = lam_ref[0, 0], lam_ref[0, 1], lam_ref[0, 2]
    lane = lax.broadcasted_iota(I32, (LANES, LANES), 1)
    first = lane < n
    dstep = jnp.exp(ls)
    mag = jnp.exp(lr * dstep)
    ar = mag * jnp.cos(li * dstep)
    ai = mag * jnp.sin(li * dstep)
    den = lr * lr + li * li
    cr = ((ar - 1.0) * lr + ai * li) / den
    ci = (ai * lr - (ar - 1.0) * li) / den
    ci_s = jnp.where(first, -ci, ci)
    bcat, bswap = bcat_ref[0], bswap_ref[0]
    bb = cr * bcat + ci_s * bswap
    bb_swap = cr * bswap - ci_s * bcat
    ccat, cswap = ccat_ref[0], cswap_ref[0]

    row_e = lax.broadcasted_iota(I32, (LANES, 2 * sp), 0)
    col_e = lax.broadcasted_iota(I32, (LANES, 2 * sp), 1)
    target = jnp.where(col_e >= sp, n, 0) + (col_e & (n - 1))
    place = jnp.where(row_e == target, 1.0, 0.0).astype(BF16)
    own = lax.shift_right_logical(row_e, 4) == lax.shift_right_logical(col_e & (sp - 1), 6)
    row_k = lax.broadcasted_iota(I32, (LANES, LANES), 0)
    own_k = lax.shift_right_logical(row_k, 4) == lax.shift_right_logical(lane, 4)

    def expand(blk):
        return jnp.where(own, _dot(blk.astype(BF16), place), 0.0).astype(BF16)

    pr = jnp.ones((LANES, LANES), F32)
    pi = jnp.zeros((LANES, LANES), F32)
    outs = []
    for m in range(S5_Q + 1):
        pr_m = jnp.where(first, pr, -pr)
        outs.append(pr_m * ccat - pi * cswap)
        if m < S5_Q:
            t = S5_Q - 1 - m
            pi_s = jnp.where(first, -pi, pi)
            wst_ref[0, t * LANES:(t + 1) * LANES, :] = expand(pr * bb + pi_s * bb_swap)
        if m >= 1:
            vt_ref[0, (m - 1) * LANES:m * LANES, :] = expand(outs[m])
        pr, pi = pr * ar - pi * ai, pr * ai + pi * ar

    taps = jnp.concatenate([outs[S5_Q - 1 - t] for t in range(S5_Q)], axis=0)
    kk = lax.dot_general(bb, taps, (((1,), (1,)), ((), ())),
                         precision=lax.Precision.HIGHEST, preferred_element_type=F32)
    for t in range(S5_Q):
        blk = kk[:, t * LANES:(t + 1) * LANES]
        krev_ref[0, t * LANES:(t + 1) * LANES, :] = jnp.where(own_k, blk, 0.0).astype(BF16)

    rlr, rli, rls = row_ref[0, 0:1, :], row_ref[0, 1:2, :], row_ref[0, 2:3, :]
    rstep = jnp.exp(rls)
    e16 = jnp.exp((float(S5_Q) * rlr) * rstep)
    ang = (float(S5_Q) * rli) * rstep
    aq_ref[0] = jnp.concatenate([e16 * jnp.cos(ang), e16 * jnp.sin(ang)], axis=1)


def s5_prep(lam_re, lam_im, log_step, b_re, b_im, c_re, c_im):
    np_ = S5_N_PACKS
    sp = S5_PACK_STATE

    def compact(m):
        return m.reshape(np_, LANES, S5_STATE)

    def per_row(v):
        return compact(jnp.broadcast_to(v[:, None, :], (S5_N_GROUPS, S5_GROUP_CH, S5_STATE)))

    def two(a, b):
        return jnp.concatenate([a, b], axis=-1)

    step2d = jnp.broadcast_to(log_step[:, None], (S5_N_GROUPS, S5_STATE))
    lam = jnp.stack([two(per_row(v), per_row(v)) for v in (lam_re, lam_im, step2d)], axis=1)
    rows = jnp.stack([v.reshape(np_, sp) for v in (lam_re, lam_im, step2d)], axis=1)
    bre, bim = compact(b_re.transpose(0, 2, 1)), compact(b_im.transpose(0, 2, 1))
    cre, cim = compact(c_re), compact(c_im)
    mat = pl.BlockSpec((1, LANES, LANES), lambda i: (i, 0, 0))
    big = pl.BlockSpec((1, S5_Q * LANES, 2 * sp), lambda i: (i, 0, 0))
    return pl.pallas_call(
        _s5_prep_kernel,
        out_shape=(jax.ShapeDtypeStruct((np_, S5_Q * LANES, 2 * sp), BF16),
                   jax.ShapeDtypeStruct((np_, S5_Q * LANES, 2 * sp), BF16),
                   jax.ShapeDtypeStruct((np_, S5_Q * LANES, LANES), BF16),
                   jax.ShapeDtypeStruct((np_, 1, 2 * sp), F32)),
        grid=(np_,),
        in_specs=[pl.BlockSpec((1, 3, LANES, LANES), lambda i: (i, 0, 0, 0)),
                  pl.BlockSpec((1, 3, sp), lambda i: (i, 0, 0)), mat, mat, mat, mat],
        out_specs=(big, big, pl.BlockSpec((1, S5_Q * LANES, LANES), lambda i: (i, 0, 0)),
                   pl.BlockSpec((1, 1, 2 * sp), lambda i: (i, 0, 0))),
        compiler_params=_cparams(("parallel",)),
        name="s5_prep",
    )(lam, rows, two(bre, bim), two(bim, bre), two(cre, cim), two(cim, cre))


def _s5_kernel(x_ref, wst_ref, vt_ref, krev_ref, aq_ref, d_ref, h0_ref, g_ref, hout_ref, *scratch,
               nblk, nseq, chain):
    q = S5_Q
    sp = S5_PACK_STATE
    us = [x_ref[pl.ds(t, nblk, stride=q), :] for t in range(q)]
    ucat = jnp.concatenate([u.astype(BF16) for u in us], axis=1)
    s_in = _dot(ucat, wst_ref[0])
    aq = aq_ref[0]
    aqr, aqi = aq[:, :sp], aq[:, sp:]

    def advance(h, s):
        hr, hi = h[:, :sp], h[:, sp:]
        return jnp.concatenate([aqr * hr - aqi * hi + s[:, :sp], aqr * hi + aqi * hr + s[:, sp:]], axis=1)

    tmat = scratch[-1]

    @pl.when(pl.program_id(1) == 0)
    def _():
        for s in range(q):
            for t in range(s, q):
                tmat[s * LANES:(s + 1) * LANES, t * LANES:(t + 1) * LANES] = (
                    krev_ref[0, (q - 1 - (t - s)) * LANES:(q - (t - s)) * LANES, :])
        for p in range(q // 2):
            tmat[(2 * p + 1) * LANES:(2 * p + 2) * LANES, 2 * p * LANES:(2 * p + 1) * LANES] = (
                jnp.zeros((LANES, LANES), BF16))

    if chain:
        hin_s, s_s, y_s = scratch[:3]
        per_seq = nblk // nseq
        s_s[...] = s_in

        def body(c, hs):
            out = []
            for s in range(nseq):
                row = s * per_seq + c
                hin_s[pl.ds(row, 1), :] = hs[s]
                out.append(advance(hs[s], s_s[pl.ds(row, 1), :]))
            return tuple(out)

        h0 = h0_ref[0, 0]
        hs = lax.fori_loop(0, per_seq, body, tuple(h0[s:s + 1] for s in range(nseq)), unroll=8)
        hout_ref[0, 0] = jnp.concatenate(hs, axis=0)
        hin = hin_s[...]
    else:
        y_s = scratch[0]
        hin = h0_ref[0, 0]
        hout_ref[0, 0] = advance(hin, s_in)
    hinb = hin.astype(BF16)
    d = d_ref[...]
    for p in range(q // 2):
        c0, c1 = 2 * p * LANES, (2 * p + 2) * LANES
        y2 = _dot_nt(hinb, vt_ref[0, c0:c1, :])
        y2 = y2 + _dot(ucat[:, :c1], tmat[:c1, c0:c1])
        for j in range(2):
            t = 2 * p + j
            y = y2[:, j * LANES:(j + 1) * LANES] + d * us[t]
            y_s[pl.ds(t, nblk, stride=q), :] = _gelu_exact(y)
    g_ref[...] = y_s[...].astype(BF16)


def s5_scan(x, wst, vt, krev, aq, dskip, h0, *, row0, nseq, seqlen, chain, out_alias=None):
    t, d = x.shape
    q = S5_Q
    sp2 = 2 * S5_PACK_STATE
    spb = S5_CHAIN_SEQS if chain else nseq
    nstep = nseq // spb
    rows = spb * seqlen
    nblk = rows // q
    assert chain or seqlen == q
    grid = (S5_N_PACKS, nstep)
    rb0 = row0 // rows
    xmap = lambda p, b: (rb0 + b, p)
    wmap = lambda p, b: (p, 0, 0)
    dmap = lambda p, b: (0, p)
    hspec = pl.BlockSpec((1, 1, spb, sp2), lambda p, b: (p, b, 0, 0))
    hshape = (S5_N_PACKS, nstep, spb, sp2)
    scratch = [pltpu.VMEM((rows, LANES), F32), pltpu.VMEM((q * LANES, q * LANES), BF16)]
    if chain:
        scratch = [pltpu.VMEM((nblk, sp2), F32), pltpu.VMEM((nblk, sp2), F32)] + scratch
    sem = ("parallel", "arbitrary")
    in_specs = [pl.BlockSpec((rows, LANES), xmap),
                pl.BlockSpec((1, q * LANES, sp2), wmap),
                pl.BlockSpec((1, q * LANES, sp2), wmap),
                pl.BlockSpec((1, q * LANES, LANES), wmap),
                pl.BlockSpec((1, 1, sp2), wmap),
                pl.BlockSpec((1, LANES), dmap),
                hspec]
    args = [x, wst, vt, krev, aq, dskip, h0]
    kern = functools.partial(_s5_kernel, nblk=nblk, nseq=spb, chain=chain)
    aliases = {}
    if out_alias is not None:
        in_specs.append(pl.BlockSpec(memory_space=pl.ANY))
        args.append(out_alias)
        aliases = {len(args) - 1: 0}
        base = kern
        kern = lambda *refs: base(*refs[:7], *refs[8:])
    return pl.pallas_call(
        kern,
        out_shape=(jax.ShapeDtypeStruct((t, d), BF16), jax.ShapeDtypeStruct(hshape, F32)),
        grid=grid,
        in_specs=in_specs,
        out_specs=(pl.BlockSpec((rows, LANES), xmap), hspec),
        scratch_shapes=scratch,
        input_output_aliases=aliases,
        compiler_params=_cparams(sem),
        name="s5_scan",
    )(*args)


def _s5_pack_state(re, im):
    b = re.shape[0]
    return jnp.concatenate([re.reshape(b, S5_N_PACKS, S5_PACK_STATE), im.reshape(b, S5_N_PACKS, S5_PACK_STATE)], axis=-1)


def _s5_unpack_state(h):
    b = h.shape[0]
    return (h[..., :S5_PACK_STATE].reshape(b, S5_N_GROUPS, S5_STATE),
            h[..., S5_PACK_STATE:].reshape(b, S5_N_GROUPS, S5_STATE))


def kernel(x_prompt, x_sample, cache_ssd_conv, state_ssd, state_s5_re, state_s5_im, ssd_w_in, ssd_conv_w, ssd_conv_b, ssd_dt_bias, ssd_a_log, ssd_d, ssd_norm_w, ssd_w_out, s5_lambda_re, s5_lambda_im, s5_log_step, s5_b_re, s5_b_im, s5_c_re, s5_c_im, s5_d, s5_w_glu, ln_mix_g, ln_mix_b, ln_ffn_g, ln_ffn_b, router_w, router_b, moe_w_gate, moe_w_up, moe_w_down):
    d = D_MODEL
    x0_parts = [x_prompt.reshape(T_PROMPT, d), x_sample.reshape(T_SAMPLE, d)]
    x0b = cast_rows(x0_parts, BF16, tm=T_SAMPLE)
    rw = jnp.zeros((d, LANES), BF16).at[:, :N_EXPERTS].set(router_w.astype(BF16))
    rb = jnp.zeros((1, LANES), F32).at[0, :N_EXPERTS].set(router_b)

    w_in = ssd_w_in[0].astype(BF16)
    z = matmul(x0b, w_in, tm=MM_TM, tn=1024, col0=0, ncols=SSD_D_INNER)
    xbc = matmul(x0b, w_in, tm=MM_TM, tn=1024, col0=SSD_D_INNER, ncols=SSD_CONV_DIM)
    dt_raw = matmul(x0b, w_in, tm=MM_TM, tn=LANES, col0=SSD_D_INNER + SSD_CONV_DIM, ncols=SSD_N_HEADS)

    conv_w = ssd_conv_w[0]
    conv_b = ssd_conv_b[0].reshape(1, SSD_CONV_DIM)
    prev_p = jnp.zeros((BATCH, SUBLANES, SSD_CONV_DIM), F32)
    prev_s = jnp.concatenate([jnp.zeros((DEC_BATCH, SUBLANES - (SSD_CONV_W - 1), SSD_CONV_DIM), F32),
                              cache_ssd_conv[0]], axis=1)

    dt_sp, dta = dt_prep(dt_raw, ssd_dt_bias[0].reshape(1, SSD_N_HEADS), ssd_a_log[0].reshape(1, SSD_N_HEADS))
    dskip = jnp.repeat(ssd_d[0], SSD_HEAD_DIM).reshape(1, SSD_D_INNER)
    norm_w = ssd_norm_w[0].reshape(1, SSD_D_INNER)
    h0_p = jnp.zeros((BATCH, SSD_N_GROUPS, SSD_GROUP_CH, SSD_D_STATE), F32)
    h0_s = state_ssd[0].reshape(DEC_BATCH, SSD_N_GROUPS, SSD_GROUP_CH, SSD_D_STATE)
    lay_p = _dt_layouts(dt_sp, 0, BATCH, SEQ, SSD_Q_PROMPT) + _dt_layouts(dta, 0, BATCH, SEQ, SSD_Q_PROMPT)
    lay_s = (_dt_layouts(dt_sp, T_PROMPT, DEC_BATCH, DEC_SEQ, DEC_SEQ)
             + _dt_layouts(dta, T_PROMPT, DEC_BATCH, DEC_SEQ, DEC_SEQ))
    ymix, ssd_p = ssd_scan(xbc, prev_p, conv_w, conv_b, z, lay_p[0], lay_p[2], lay_p[1], lay_p[3], h0_p, dskip,
                           norm_w, row0=0, nseq=BATCH, seqlen=SEQ, q=SSD_Q_PROMPT)
    ymix, ssd_s = ssd_scan(xbc, prev_s, conv_w, conv_b, z, lay_s[0], lay_s[2], lay_s[1], lay_s[3], h0_s, dskip,
                           norm_w, row0=T_PROMPT, nseq=DEC_BATCH, seqlen=DEC_SEQ, q=DEC_SEQ, out_alias=ymix)
    mix = matmul(ymix, ssd_w_out[0].astype(BF16), tm=MM_TM, tn=512)

    wg = cast_weights(moe_w_gate, BF16, rows_per_block=D_MODEL // 2)
    wu = cast_weights(moe_w_up, BF16, rows_per_block=D_MODEL // 2)
    wd = moe_w_down
    x1, idx, gate = ln_router(x0_parts, mix, ln_mix_g[0].reshape(1, d), ln_mix_b[0].reshape(1, d), rw, rb)
    pos, y_sorted = moe_layer(x1, idx, wg, wu, wd, 0)
    (x2,) = combine_ln(pos, x1, gate, y_sorted, ln_ffn_g[0].reshape(1, d), ln_ffn_b[0].reshape(1, d))

    wst, vt, krev, aq = s5_prep(s5_lambda_re[0], s5_lambda_im[0], s5_log_step[0], s5_b_re[0], s5_b_im[0],
                                s5_c_re[0], s5_c_im[0])
    s5_dskip = s5_d[0].reshape(1, d)
    sp2 = 2 * S5_PACK_STATE
    hs_p = jnp.zeros((S5_N_PACKS, BATCH // S5_CHAIN_SEQS, S5_CHAIN_SEQS, sp2), F32)
    hs_s = _s5_pack_state(state_s5_re[0], state_s5_im[0]).transpose(1, 0, 2)[:, None]
    gact, s5_p = s5_scan(x2, wst, vt, krev, aq, s5_dskip, hs_p, row0=0, nseq=BATCH, seqlen=SEQ, chain=True)
    gact, s5_s = s5_scan(x2, wst, vt, krev, aq, s5_dskip, hs_s, row0=T_PROMPT, nseq=DEC_BATCH, seqlen=DEC_SEQ,
                         chain=False, out_alias=gact)
    mix = glu_matmul(gact, s5_w_glu[0].astype(BF16), tm=MM_TM, tn=512)

    x3, idx, gate = ln_router([x2], mix, ln_mix_g[1].reshape(1, d), ln_mix_b[1].reshape(1, d), rw, rb)
    pos, y_sorted = moe_layer(x3, idx, wg, wu, wd, 1)
    g1 = ln_ffn_g[1].reshape(1, d)
    b1 = ln_ffn_b[1].reshape(1, d)
    (y_p,) = combine_ln(pos, x3, gate, y_sorted, g1, b1, row0=0, nrows=T_PROMPT)
    (y_s,) = combine_ln(pos, x3, gate, y_sorted, g1, b1, row0=T_PROMPT, nrows=T_SAMPLE)

    keep = SSD_CONV_W - 1
    conv_p = jnp.stack([xbc[(b + 1) * SEQ - keep:(b + 1) * SEQ] for b in range(BATCH)])[None]
    conv_s = xbc[T_PROMPT:].reshape(DEC_BATCH, DEC_SEQ, SSD_CONV_DIM)[:, DEC_SEQ - keep:][None]
    ssd_state_p = ssd_p.reshape(1, BATCH, SSD_N_HEADS, SSD_HEAD_DIM, SSD_D_STATE)
    ssd_state_s = ssd_s.reshape(1, DEC_BATCH, SSD_N_HEADS, SSD_HEAD_DIM, SSD_D_STATE)
    re_p, im_p = _s5_unpack_state(s5_p.reshape(S5_N_PACKS, BATCH, sp2).transpose(1, 0, 2))
    re_s, im_s = _s5_unpack_state(s5_s[:, 0].transpose(1, 0, 2))
    return (y_p.reshape(BATCH, SEQ, d), y_s.reshape(DEC_BATCH, DEC_SEQ, d),
            conv_p, ssd_state_p, re_p[None], im_p[None],
            conv_s, ssd_state_s, re_s[None], im_s[None])
```
